```python
import math
import jax
import jax.numpy as jnp
from jax import lax
import numpy as np

D_MODEL = 2048
BATCH = 4
SEQ = 2048
DEPTH = 4
DEC_BATCH = 128
DEC_SEQ = 4
PAST_LEN = 16384
PAGE_SIZE = 128

N_META = 16
GROUP_WIDTH = D_MODEL // 4
EPS = 1e-6
R_HEAD_DIM = 64
R_HEADS = GROUP_WIDTH // R_HEAD_DIM
R_WIDTH = R_HEADS * R_HEAD_DIM
R_DECAY_RANK = 64
R_ICL_RANK = 64
R_GATE_RANK = 128
R_COLS = 3 * R_WIDTH + R_DECAY_RANK + R_ICL_RANK + R_GATE_RANK
RWKV_DECAY_SCALE = 0.606531
RWKV_GN_EPS = 64e-5
S5_GROUP_CH = 16
S5_GROUPS = GROUP_WIDTH // S5_GROUP_CH
S5_WIDTH = S5_GROUPS * S5_GROUP_CH
S5_STATE = 64
S5_COLS = S5_WIDTH
H_HEAD_DIM = 128
H_HEADS = GROUP_WIDTH // H_HEAD_DIM
H_WIDTH = H_HEADS * H_HEAD_DIM
H_COLS = 4 * H_WIDTH
HGRN_MAX_INPUT = 1.0 - 1e-4
G_VAL_DIM = 128
G_HEADS = GROUP_WIDTH // G_VAL_DIM
G_KEY_DIM = G_VAL_DIM // 2
G_WIDTH = G_HEADS * G_VAL_DIM
G_QK = G_HEADS * G_KEY_DIM
G_GATE_RANK = 16
G_COLS = 2 * G_QK + G_WIDTH + G_GATE_RANK + G_WIDTH
GLA_GATE_NORM = 16.0

MIX_WIDTH = R_WIDTH + S5_WIDTH + H_WIDTH + G_WIDTH
IN_COLS = R_COLS + S5_COLS + H_COLS + G_COLS
CHUNK = 16
D_FF = ((8 * D_MODEL // 3 + 127) // 128) * 128
CONV_W = 3

kernel_name = 'hybrid_rwkv7_s5_hgrn2_gla_decode_step'


def _rmsnorm(x, g):
    xf = x.astype(jnp.float32)
    y = xf * lax.rsqrt(jnp.mean(xf * xf, axis=-1, keepdims=True) + EPS)
    return (y * g.astype(jnp.float32)).astype(x.dtype)


def _head_rmsnorm(o, g):
    y = o * lax.rsqrt(jnp.mean(o * o, axis=-1, keepdims=True) + EPS)
    return y * g.astype(jnp.float32).reshape(o.shape[-2:])


def _rwkv7_mix(p, shift_prev, S0, mu, w0, w_up, a0, a_up, g_up, k_k, k_a, r_k, ln_g):
    f32 = jnp.float32
    Bn, T, _ = p.shape
    pf = p.astype(f32)
    p_prev = jnp.concatenate([shift_prev.astype(f32)[:, None, :], pf[:, :-1]], axis=1)
    ps = pf + (p_prev - pf) * mu.astype(f32)
    W = R_WIDTH
    r, k, v, w_lo, a_lo, g_lo = jnp.split(
        ps, [W, 2 * W, 3 * W, 3 * W + R_DECAY_RANK, 3 * W + R_DECAY_RANK + R_ICL_RANK], axis=-1)
    log_w = -RWKV_DECAY_SCALE * jax.nn.sigmoid(w0.astype(f32) + jnp.tanh(w_lo) @ w_up.astype(f32))
    a = jax.nn.sigmoid(a0.astype(f32) + a_lo @ a_up.astype(f32))
    g = jax.nn.sigmoid(g_lo) @ g_up.astype(f32)
    hs = lambda t: t.reshape(Bn, T, R_HEADS, R_HEAD_DIM)
    r, k, v, w, a = hs(r), hs(k), hs(v), hs(jnp.exp(log_w)), hs(a)
    kk = k * k_k.astype(f32).reshape(R_HEADS, R_HEAD_DIM)
    kk = kk / jnp.maximum(jnp.sqrt(jnp.sum(kk * kk, axis=-1, keepdims=True)), 1e-12)
    k = k * (1.0 + (a - 1.0) * k_a.astype(f32).reshape(R_HEADS, R_HEAD_DIM))
    ka = kk * a

    def step(S, inp):
        r_t, w_t, k_t, v_t, kk_t, ka_t = inp
        sa = jnp.einsum('bhvk,bhk->bhv', S, kk_t)
        S = (S * w_t[:, :, None, :] - sa[..., None] * ka_t[:, :, None, :]
             + v_t[..., None] * k_t[:, :, None, :])
        return S, jnp.einsum('bhvk,bhk->bhv', S, r_t)

    tm = lambda t: jnp.moveaxis(t, 1, 0)
    S, y = lax.scan(step, S0.astype(f32), (tm(r), tm(w), tm(k), tm(v), tm(kk), tm(ka)))
    y = jnp.moveaxis(y, 0, 1)
    mean = jnp.mean(y, axis=-1, keepdims=True)
    var = jnp.mean((y - mean) ** 2, axis=-1, keepdims=True)
    y = (y - mean) * lax.rsqrt(var + RWKV_GN_EPS) * ln_g.astype(f32).reshape(R_HEADS, R_HEAD_DIM)
    y = y + jnp.sum(r * k * r_k.astype(f32), axis=-1, keepdims=True) * v
    out = y.reshape(Bn, T, R_WIDTH) * g
    return out.astype(p.dtype), p[:, -1].astype(shift_prev.dtype), S.astype(S0.dtype)


def _complex_affine_combine(e1, e2):
    a1r, a1i, b1r, b1i = e1
    a2r, a2i, b2r, b2i = e2
    return (a2r * a1r - a2i * a1i,
            a2r * a1i + a2i * a1r,
            a2r * b1r - a2i * b1i + b2r,
            a2r * b1i + a2i * b1r + b2i)


def _s5_mix(u, h0_re, h0_im, A_re, A_im, log_dt, B_re, B_im, C_re, C_im, D, w_glu, b_glu):
    f32 = jnp.float32
    Bn, T, _ = u.shape
    uf = u.astype(f32)
    ug = uf.reshape(Bn, T, S5_GROUPS, S5_GROUP_CH)
    A_re = A_re.astype(f32)
    A_im = A_im.astype(f32)
    dt = jnp.exp(log_dt.astype(f32))[:, None]
    mag = jnp.exp(A_re * dt)
    ab_re = mag * jnp.cos(A_im * dt)
    ab_im = mag * jnp.sin(A_im * dt)
    den = A_re * A_re + A_im * A_im
    n_re = ab_re - 1.0
    co_re = (n_re * A_re + ab_im * A_im) / den
    co_im = (ab_im * A_re - n_re * A_im) / den
    B_re = B_re.astype(f32)
    B_im = B_im.astype(f32)
    bb_re = co_re[..., None] * B_re - co_im[..., None] * B_im
    bb_im = co_re[..., None] * B_im + co_im[..., None] * B_re
    bu_re = jnp.einsum('btgc,gpc->btgp', ug, bb_re)
    bu_im = jnp.einsum('btgc,gpc->btgp', ug, bb_im)
    a_re = jnp.broadcast_to(ab_re, bu_re.shape)
    a_im = jnp.broadcast_to(ab_im, bu_im.shape)
    cum_re, cum_im, h_re, h_im = lax.associative_scan(
        _complex_affine_combine, (a_re, a_im, bu_re, bu_im), axis=1)
    h0r = h0_re.astype(f32)[:, None]
    h0i = h0_im.astype(f32)[:, None]
    h_re = h_re + cum_re * h0r - cum_im * h0i
    h_im = h_im + cum_re * h0i + cum_im * h0r
    y = (jnp.einsum('btgp,gcp->btgc', h_re, C_re.astype(f32))
         - jnp.einsum('btgp,gcp->btgc', h_im, C_im.astype(f32)))
    y = y.reshape(Bn, T, S5_WIDTH) + D.astype(f32) * uf
    y = jax.nn.gelu(y)
    y = y * jax.nn.sigmoid(y @ w_glu.astype(f32) + b_glu.astype(f32))
    return (y.astype(u.dtype), h_re[:, -1].astype(h0_re.dtype), h_im[:, -1].astype(h0_im.dtype))


def _chunked_gla(q, k, v, log_f, S0):
    f32 = jnp.float32
    Bn, T, H, _ = q.shape
    V = v.shape[-1]
    n_chunks = -(-T // CHUNK)
    pad = n_chunks * CHUNK - T

    def to_chunks(a):
        a = jnp.pad(a.astype(f32), ((0, 0), (0, pad), (0, 0), (0, 0)))
        return jnp.moveaxis(a.reshape(Bn, n_chunks, CHUNK, H, a.shape[-1]), 1, 0)

    causal = jnp.tril(jnp.ones((CHUNK, CHUNK), dtype=bool))[None, :, :, None, None]

    def step(S, inp):
        qi, ki, vi, gi = inp
        b = jnp.cumsum(gi, axis=1)
        b_last = b[:, -1]
        o_inter = jnp.einsum('bchk,bhkv->bchv', qi * jnp.exp(b), S)
        diff = jnp.where(causal, b[:, :, None] - b[:, None, :], 0.0)
        decay = jnp.where(causal, jnp.exp(diff), 0.0)
        att = jnp.einsum('bihk,bjhk,bijhk->bhij', qi, ki, decay)
        o_intra = jnp.einsum('bhij,bjhv->bihv', att, vi)
        S_new = (S * jnp.exp(b_last)[..., None]
                 + jnp.einsum('bchk,bchv->bhkv', ki * jnp.exp(b_last[:, None] - b), vi))
        return S_new, o_inter + o_intra

    S, o = lax.scan(step, S0.astype(f32),
                    (to_chunks(q), to_chunks(k), to_chunks(v), to_chunks(log_f)))
    o = jnp.moveaxis(o, 0, 1).reshape(Bn, n_chunks * CHUNK, H, V)[:, :T]
    return o, S


def _hgrn2_mix(p, S0, lb, norm_g):
    f32 = jnp.float32
    Bn, T, _ = p.shape
    q, f_raw, i, g = jnp.split(p.astype(f32), 4, axis=-1)
    lb = lb.astype(f32)
    k = jnp.minimum((1.0 - lb) * jax.nn.sigmoid(-f_raw), HGRN_MAX_INPUT)
    log_f = jnp.log1p(-k)
    hs = lambda t: t.reshape(Bn, T, H_HEADS, H_HEAD_DIM)
    o, S = _chunked_gla(hs(jax.nn.silu(q)), hs(k), hs(i), hs(log_f), S0)
    o = _head_rmsnorm(o, norm_g) * hs(jax.nn.silu(g))
    return o.reshape(Bn, T, H_WIDTH).astype(p.dtype), S.astype(S0.dtype)


def _gla_mix(p, S0, gk_up, gk_b, norm_g):
    f32 = jnp.float32
    Bn, T, _ = p.shape
    q, k, v, gk_lo, gate = jnp.split(
        p.astype(f32), [G_QK, 2 * G_QK, 2 * G_QK + G_WIDTH, 2 * G_QK + G_WIDTH + G_GATE_RANK], axis=-1)
    log_g = jax.nn.log_sigmoid(gk_lo @ gk_up.astype(f32) + gk_b.astype(f32)) / GLA_GATE_NORM
    hk = lambda t: t.reshape(Bn, T, G_HEADS, G_KEY_DIM)
    hv = lambda t: t.reshape(Bn, T, G_HEADS, G_VAL_DIM)
    o, S = _chunked_gla(hk(q) * (G_KEY_DIM ** -0.5), hk(k), hv(v), hk(log_g), S0)
    o = _head_rmsnorm(o, norm_g) * hv(jax.nn.silu(gate))
    return o.reshape(Bn, T, G_WIDTH).astype(p.dtype), S.astype(S0.dtype)


def _conv_ffn(x, buf, w_up, conv_w, conv_b, w_down):
    T = x.shape[1]
    u, gate = jnp.split(x @ w_up, 2, axis=-1)
    u_ext = jnp.concatenate([buf.astype(u.dtype), u], axis=1)
    c = conv_b
    for j in range(CONV_W):
        c = c + conv_w[j] * u_ext[:, j:j + T]
    out = (jax.nn.gelu(c) * gate) @ w_down
    return out.astype(x.dtype), u_ext[:, T:].astype(buf.dtype)


def _layer(x, st, prm, lb):
    rw_S, rw_shift, s5_re, s5_im, hg_S, gl_S, ffn_buf = st
    h = _rmsnorm(x, prm['norm_mix'])
    p = h @ prm['w_in']
    p_r, p_s5, p_h, p_g = jnp.split(
        p, [R_COLS, R_COLS + S5_COLS, R_COLS + S5_COLS + H_COLS], axis=-1)
    y_r, rw_shift_new, rw_S_new = _rwkv7_mix(
        p_r, rw_shift, rw_S, prm['rwkv_mu'], prm['rwkv_w0'], prm['rwkv_w_up'], prm['rwkv_a0'],
        prm['rwkv_a_up'], prm['rwkv_g_up'], prm['rwkv_k_k'], prm['rwkv_k_a'], prm['rwkv_r_k'],
        prm['rwkv_ln'])
    y_s5, s5_re_new, s5_im_new = _s5_mix(
        p_s5, s5_re, s5_im, prm['s5_A_re'], prm['s5_A_im'], prm['s5_log_dt'], prm['s5_B_re'],
        prm['s5_B_im'], prm['s5_C_re'], prm['s5_C_im'], prm['s5_D'], prm['s5_w_glu'], prm['s5_b_glu'])
    y_h, hg_S_new = _hgrn2_mix(p_h, hg_S, lb, prm['hgrn_norm'])
    y_g, gl_S_new = _gla_mix(p_g, gl_S, prm['gla_gk_up'], prm['gla_gk_b'], prm['gla_norm'])
    mix = jnp.concatenate([y_r, y_s5, y_h, y_g], axis=-1) @ prm['w_out']
    x = x + mix.astype(x.dtype)
    y_f, ffn_buf_new = _conv_ffn(_rmsnorm(x, prm['norm_ffn']), ffn_buf, prm['ffn_w_up'],
                                 prm['ffn_conv_w'], prm['ffn_conv_b'], prm['ffn_w_down'])
    x = x + y_f
    return x, (rw_S_new, rw_shift_new, s5_re_new, s5_im_new, hg_S_new, gl_S_new, ffn_buf_new)


def setup_inputs(seed: int = 0) -> dict:
    key = jax.random.key(seed)
    keys = iter(jax.random.split(key, 64))
    f32 = jnp.float32
    L = DEPTH

    def nrm(shape, scale):
        return scale * jax.random.normal(next(keys), shape, f32)

    def unif(shape, lo, hi):
        return jax.random.uniform(next(keys), shape, f32, lo, hi)

    n_idx = jnp.arange(S5_STATE, dtype=f32)
    return {
        'x_prompt': nrm((BATCH, SEQ, D_MODEL), 1.0),
        'x_sample': nrm((DEC_BATCH, DEC_SEQ, D_MODEL), 1.0),
        'state_rwkv': nrm((L, DEC_BATCH, R_HEADS, R_HEAD_DIM, R_HEAD_DIM), 0.5),
        'state_rwkv_shift': nrm((L, DEC_BATCH, R_COLS), 1.0),
        'state_s5_re': nrm((L, DEC_BATCH, S5_GROUPS, S5_STATE), 0.5),
        'state_s5_im': nrm((L, DEC_BATCH, S5_GROUPS, S5_STATE), 0.5),
        'state_hgrn': nrm((L, DEC_BATCH, H_HEADS, H_HEAD_DIM, H_HEAD_DIM), 0.5),
        'state_gla': nrm((L, DEC_BATCH, G_HEADS, G_KEY_DIM, G_VAL_DIM), 0.5),
        'state_ffn_conv': nrm((L, DEC_BATCH, CONV_W - 1, D_FF), 1.0),
        'meta_tokens': nrm((N_META, D_MODEL), 1.0),
        'norm_mix': 1.0 + nrm((L, D_MODEL), 0.02),
        'w_in': nrm((L, D_MODEL, IN_COLS), D_MODEL ** -0.5),
        'w_out': nrm((L, MIX_WIDTH, D_MODEL), MIX_WIDTH ** -0.5),
        'rwkv_mu': unif((L, R_COLS), 0.0, 1.0),
        'rwkv_w0': nrm((L, R_WIDTH), 0.5),
        'rwkv_w_up': nrm((L, R_DECAY_RANK, R_WIDTH), 0.5 * R_DECAY_RANK ** -0.5),
        'rwkv_a0': nrm((L, R_WIDTH), 0.5),
        'rwkv_a_up': nrm((L, R_ICL_RANK, R_WIDTH), 0.5 * R_ICL_RANK ** -0.5),
        'rwkv_g_up': nrm((L, R_GATE_RANK, R_WIDTH), R_GATE_RANK ** -0.5),
        'rwkv_k_k': 0.85 + nrm((L, R_WIDTH), 0.05),
        'rwkv_k_a': 1.0 + nrm((L, R_WIDTH), 0.05),
        'rwkv_r_k': nrm((L, R_HEADS, R_HEAD_DIM), 0.1),
        'rwkv_ln': 1.0 + nrm((L, R_WIDTH), 0.02),
        's5_A_re': -0.5 + nrm((L, S5_GROUPS, S5_STATE), 0.01),
        's5_A_im': math.pi * n_idx + nrm((L, S5_GROUPS, S5_STATE), 0.01),
        's5_log_dt': unif((L, S5_GROUPS), math.log(0.001), math.log(0.1)),
        's5_B_re': nrm((L, S5_GROUPS, S5_STATE, S5_GROUP_CH), (2.0 * S5_GROUP_CH) ** -0.5),
        's5_B_im': nrm((L, S5_GROUPS, S5_STATE, S5_GROUP_CH), (2.0 * S5_GROUP_CH) ** -0.5),
        's5_C_re': nrm((L, S5_GROUPS, S5_GROUP_CH, S5_STATE), (2.0 * S5_STATE) ** -0.5),
        's5_C_im': nrm((L, S5_GROUPS, S5_GROUP_CH, S5_STATE), (2.0 * S5_STATE) ** -0.5),
        's5_D': nrm((L, S5_WIDTH), 1.0),
        's5_w_glu': nrm((L, S5_WIDTH, S5_WIDTH), S5_WIDTH ** -0.5),
        's5_b_glu': nrm((L, S5_WIDTH), 0.02),
        'hgrn_lower_bounds': nrm((L, H_WIDTH), 0.1),
        'hgrn_norm': 1.0 + nrm((L, H_WIDTH), 0.02),
        'gla_gk_up': nrm((L, G_GATE_RANK, G_QK), G_GATE_RANK ** -0.5),
        'gla_gk_b': nrm((L, G_QK), 0.1),
        'gla_norm': 1.0 + nrm((L, G_WIDTH), 0.02),
        'norm_ffn': 1.0 + nrm((L, D_MODEL), 0.02),
        'ffn_w_up': nrm((L, D_MODEL, 2 * D_FF), D_MODEL ** -0.5),
        'ffn_conv_w': nrm((L, CONV_W, D_FF), CONV_W ** -0.5),
        'ffn_conv_b': nrm((L, D_FF), 0.02),
        'ffn_w_down': nrm((L, D_FF, D_MODEL), D_FF ** -0.5),
        'norm_final': 1.0 + nrm((D_MODEL,), 0.02),
    }


def reference(x_prompt, x_sample, state_rwkv, state_rwkv_shift, state_s5_re, state_s5_im,
              state_hgrn, state_gla, state_ffn_conv, meta_tokens, norm_mix, w_in, w_out,
              rwkv_mu, rwkv_w0, rwkv_w_up, rwkv_a0, rwkv_a_up, rwkv_g_up, rwkv_k_k, rwkv_k_a,
              rwkv_r_k, rwkv_ln, s5_A_re, s5_A_im, s5_log_dt, s5_B_re, s5_B_im, s5_C_re, s5_C_im,
              s5_D, s5_w_glu, s5_b_glu, hgrn_lower_bounds, hgrn_norm, gla_gk_up, gla_gk_b,
              gla_norm, norm_ffn, ffn_w_up, ffn_conv_w, ffn_conv_b, ffn_w_down, norm_final):
    f32 = jnp.float32
    lb_soft = jax.nn.softmax(hgrn_lower_bounds.astype(f32), axis=0)
    lower_bound = jnp.cumsum(lb_soft, axis=0) - lb_soft[0]

    Bp = x_prompt.shape[0]
    dt = x_prompt.dtype
    meta = jnp.broadcast_to(meta_tokens.astype(dt)[None], (Bp, N_META, D_MODEL))
    xp = jnp.concatenate([meta, x_prompt], axis=1)
    xs = x_sample
    prompt_init = (
        jnp.zeros((Bp, R_HEADS, R_HEAD_DIM, R_HEAD_DIM), dt),
        jnp.zeros((Bp, R_COLS), dt),
        jnp.zeros((Bp, S5_GROUPS, S5_STATE), dt),
        jnp.zeros((Bp, S5_GROUPS, S5_STATE), dt),
        jnp.zeros((Bp, H_HEADS, H_HEAD_DIM, H_HEAD_DIM), dt),
        jnp.zeros((Bp, G_HEADS, G_KEY_DIM, G_VAL_DIM), dt),
        jnp.zeros((Bp, CONV_W - 1, D_FF), dt),
    )
    prompt_states = []
    sample_states = []
    for l in range(DEPTH):
        prm = {
            'norm_mix': norm_mix[l], 'w_in': w_in[l], 'w_out': w_out[l],
            'rwkv_mu': rwkv_mu[l], 'rwkv_w0': rwkv_w0[l], 'rwkv_w_up': rwkv_w_up[l],
            'rwkv_a0': rwkv_a0[l], 'rwkv_a_up': rwkv_a_up[l], 'rwkv_g_up': rwkv_g_up[l],
            'rwkv_k_k': rwkv_k_k[l], 'rwkv_k_a': rwkv_k_a[l], 'rwkv_r_k': rwkv_r_k[l],
            'rwkv_ln': rwkv_ln[l],
            's5_A_re': s5_A_re[l], 's5_A_im': s5_A_im[l], 's5_log_dt': s5_log_dt[l],
            's5_B_re': s5_B_re[l], 's5_B_im': s5_B_im[l], 's5_C_re': s5_C_re[l],
            's5_C_im': s5_C_im[l], 's5_D': s5_D[l], 's5_w_glu': s5_w_glu[l], 's5_b_glu': s5_b_glu[l],
            'hgrn_norm': hgrn_norm[l],
            'gla_gk_up': gla_gk_up[l], 'gla_gk_b': gla_gk_b[l], 'gla_norm': gla_norm[l],
            'norm_ffn': norm_ffn[l], 'ffn_w_up': ffn_w_up[l], 'ffn_conv_w': ffn_conv_w[l],
            'ffn_conv_b': ffn_conv_b[l], 'ffn_w_down': ffn_w_down[l],
        }
        xp, st_p = _layer(xp, prompt_init, prm, lower_bound[l])
        st_in = (state_rwkv[l], state_rwkv_shift[l], state_s5_re[l], state_s5_im[l],
                 state_hgrn[l], state_gla[l], state_ffn_conv[l])
        xs, st_s = _layer(xs, st_in, prm, lower_bound[l])
        prompt_states.append(st_p)
        sample_states.append(st_s)

    y_prompt = _rmsnorm(xp[:, N_META:], norm_final)
    y_sample = _rmsnorm(xs, norm_final)
    sp = [jnp.stack([st[i] for st in prompt_states]) for i in range(7)]
    ss = [jnp.stack([st[i] for st in sample_states]) for i in range(7)]
    return (y_prompt, y_sample, sp[0], ss[0], sp[1], ss[1], sp[2], ss[2], sp[3], ss[3],
            sp[4], ss[4], sp[5], ss[5], sp[6], ss[6])
```

```python
import functools
import math
from typing import NamedTuple

import jax
import jax.numpy as jnp
from jax import lax
from jax.experimental import pallas as pl
from jax.experimental.pallas import tpu as pltpu

F32 = jnp.float32
BF16 = jnp.bfloat16

D_MODEL = 2048
DEPTH = 4
N_META = 16
GROUP_WIDTH = D_MODEL // 4
EPS = 1e-6
R_HEAD_DIM = 64
R_HEADS = GROUP_WIDTH // R_HEAD_DIM
R_WIDTH = R_HEADS * R_HEAD_DIM
R_DECAY_RANK = 64
R_ICL_RANK = 64
R_GATE_RANK = 128
R_LOW = R_DECAY_RANK + R_ICL_RANK + R_GATE_RANK
R_COLS = 3 * R_WIDTH + R_LOW
RWKV_DECAY_SCALE = 0.606531
RWKV_GN_EPS = 64e-5
S5_GROUP_CH = 16
S5_GROUPS = GROUP_WIDTH // S5_GROUP_CH
S5_WIDTH = S5_GROUPS * S5_GROUP_CH
S5_STATE = 64
S5_FLAT = S5_GROUPS * S5_STATE
H_HEAD_DIM = 128
H_HEADS = GROUP_WIDTH // H_HEAD_DIM
H_WIDTH = H_HEADS * H_HEAD_DIM
HGRN_MAX_INPUT = 1.0 - 1e-4
G_VAL_DIM = 128
G_HEADS = GROUP_WIDTH // G_VAL_DIM
G_KEY_DIM = G_VAL_DIM // 2
G_WIDTH = G_HEADS * G_VAL_DIM
G_QK = G_HEADS * G_KEY_DIM
G_GATE_RANK = 16
GLA_GATE_NORM = 16.0
D_FF = ((8 * D_MODEL // 3 + 127) // 128) * 128
CONV_W = 3

P_R = 0
P_S5 = 3 * R_WIDTH
P_H = P_S5 + S5_WIDTH
P_GV = P_H + 4 * H_WIDTH
P_GG = P_GV + G_WIDTH
P_GQ = P_GG + G_WIDTH
P_GK = P_GQ + G_QK
P_RL = P_GK + G_QK
P_GL = P_RL + R_LOW
GL_PAD = 256
P_COLS = P_GL + GL_PAD
FF_PAD = 5632
S5_CHUNKS = 4
S5_SCAN = 8

VMEM_LIMIT = 56 * 1024 * 1024

NT = (((1,), (1,)), ((), ()))
TN = (((0,), (0,)), ((), ()))


class _Cfg(NamedTuple):
    B: int
    T: int
    t0: int
    tm: int
    g_mix: int
    c_rwkv: int
    c_gla: int
    g_s5: int
    c_s5: int
    inject: bool


def _dot(a, b):
    return jnp.dot(a, b, preferred_element_type=F32)


def _dg(a, b, dims):
    return lax.dot_general(a, b, dims, preferred_element_type=F32)


def _gelu(x):
    c = math.sqrt(2.0 / math.pi)
    return x * (0.5 * (1.0 + jnp.tanh(c * (x + 0.044715 * (x * x * x)))))


def _tloc(G, C, W):
    t = lax.broadcasted_iota(jnp.int32, (C, W), 0)
    return t if G == 1 else jnp.concatenate([t] * G, axis=0)


def _cumsum_rows(x, tl, C):
    sh = 1
    while sh < C:
        x = x + jnp.where(tl >= sh, pltpu.roll(x, sh, 0), 0.0)
        sh *= 2
    return x


def _segsum(x, e):
    x1 = x.astype(BF16)
    r1 = x - x1.astype(F32)
    x2 = r1.astype(BF16)
    x3 = (r1 - x2.astype(F32)).astype(BF16)
    return _dot(x1, e) + _dot(x2, e) + _dot(x3, e)


def _params(sem):
    return pltpu.CompilerParams(dimension_semantics=sem, vmem_limit_bytes=VMEM_LIMIT)


def _norm_matmul_kernel(x_ref, g_ref, w_ref, o_ref, h_scr):
    @pl.when(pl.program_id(1) == 0)
    def _():
        x = x_ref[...]
        r = lax.rsqrt(jnp.mean(x * x, axis=-1, keepdims=True) + EPS)
        h_scr[...] = (x * r * g_ref[...]).astype(BF16)

    o_ref[...] = _dot(h_scr[...], w_ref[...])


def _norm_matmul(x, g_all, w_all, l, tm, tn):
    M, D = x.shape
    N = w_all.shape[-1]
    return pl.pallas_call(
        _norm_matmul_kernel,
        grid=(M // tm, N // tn),
        in_specs=[
            pl.BlockSpec((tm, D), lambda i, j: (i, 0)),
            pl.BlockSpec((None, 1, D), lambda i, j: (l, 0, 0)),
            pl.BlockSpec((None, D, tn), lambda i, j: (l, 0, j)),
        ],
        out_specs=pl.BlockSpec((tm, tn), lambda i, j: (i, j)),
        out_shape=jax.ShapeDtypeStruct((M, N), F32),
        scratch_shapes=[pltpu.VMEM((tm, D), BF16)],
        compiler_params=_params(("arbitrary", "arbitrary")),
        name="in_proj",
    )(x, g_all, w_all)


def _out_proj_kernel(x_ref, y0, y1, y2, y3, w0, w1, w2, w3, o_ref):
    acc = _dot(y0[...].astype(BF16), w0[...])
    acc = acc + _dot(y1[...].astype(BF16), w1[...])
    acc = acc + _dot(y2[...].astype(BF16), w2[...])
    acc = acc + _dot(y3[...].astype(BF16), w3[...])
    o_ref[...] = x_ref[...] + acc


def _out_proj(x, ys, w_all, l, tm, tn):
    M, D = x.shape
    W = GROUP_WIDTH
    yspec = pl.BlockSpec((tm, W), lambda i, j: (i, 0))
    wspecs = [pl.BlockSpec((None, W, tn), functools.partial(lambda i, j, q: (l, q, j), q=q)) for q in range(4)]
    return pl.pallas_call(
        _out_proj_kernel,
        grid=(M // tm, D // tn),
        in_specs=[pl.BlockSpec((tm, tn), lambda i, j: (i, j))] + [yspec] * 4 + wspecs,
        out_specs=pl.BlockSpec((tm, tn), lambda i, j: (i, j)),
        out_shape=jax.ShapeDtypeStruct((M, D), F32),
        compiler_params=_params(("arbitrary", "arbitrary")),
        name="out_proj",
    )(x, *ys, w_all, w_all, w_all, w_all)


def _ffn_up_kernel(x_ref, g_ref, wu_ref, wg_ref, cw_ref, cb_ref, buf_ref, act_ref, ust_ref, h_scr, *car,
                   tm, tn, tiles_per_seq, t0, seq_rows, inject):
    i = pl.program_id(0)
    j = pl.program_id(1)

    @pl.when(j == 0)
    def _():
        x = x_ref[...]
        r = lax.rsqrt(jnp.mean(x * x, axis=-1, keepdims=True) + EPS)
        h_scr[...] = (x * r * g_ref[...]).astype(BF16)

    h = h_scr[...]
    u = _dot(h, wu_ref[...])
    gt = _dot(h, wg_ref[...])
    rows = lax.broadcasted_iota(jnp.int32, (tm, tn), 0)
    if inject:
        tl = rows & (seq_rows - 1)
        u = jnp.where((tl == t0 - 2) | (tl == t0 - 1), buf_ref[...], u)
        p1 = pltpu.roll(u, 1, 0)
        p2 = pltpu.roll(u, 2, 0)
        ust_ref[...] = u
    else:
        (car_scr,) = car

        @pl.when(i % tiles_per_seq == 0)
        def _():
            car_scr[j, 0:2, :] = buf_ref[...]

        c0 = car_scr[j, 0:1, :]
        c1 = car_scr[j, 1:2, :]
        p1 = jnp.where(rows == 0, c1, pltpu.roll(u, 1, 0))
        p2 = jnp.where(rows == 0, c0, jnp.where(rows == 1, c1, pltpu.roll(u, 2, 0)))
        last = u[tm - 2:tm, :]
        car_scr[j, 0:2, :] = last
        ust_ref[...] = last
    cw = cw_ref[...]
    c = cb_ref[...] + cw[0:1, :] * p2
    c = c + cw[1:2, :] * p1
    c = c + cw[2:3, :] * u
    act_ref[...] = (_gelu(c) * gt).astype(BF16)


def _ffn_up(x, g_all, wu_all, wg_all, cw_all, cb_all, buf, l, l_in, cfg, tn):
    M, D = x.shape
    tm = cfg.tm
    N = wu_all.shape[-1]
    tps = cfg.T // tm if not cfg.inject else 1
    kern = functools.partial(_ffn_up_kernel, tm=tm, tn=tn, tiles_per_seq=tps, t0=cfg.t0, seq_rows=cfg.T,
                             inject=cfg.inject)
    if cfg.inject:
        buf_spec = pl.BlockSpec((None, tm, tn), lambda i, j: (l_in, i, j))
        ust_spec = pl.BlockSpec((tm, tn), lambda i, j: (i, j))
        ust_shape = jax.ShapeDtypeStruct((M, N), F32)
        scratch = [pltpu.VMEM((tm, D), BF16)]
    else:
        buf_spec = pl.BlockSpec((None, None, CONV_W - 1, tn), lambda i, j: (l_in, i // tps, 0, j))
        ust_spec = pl.BlockSpec((None, CONV_W - 1, tn), lambda i, j: (i, 0, j))
        ust_shape = jax.ShapeDtypeStruct((M // tm, CONV_W - 1, N), F32)
        scratch = [pltpu.VMEM((tm, D), BF16), pltpu.VMEM((N // tn, 8, tn), F32)]
    return pl.pallas_call(
        kern,
        grid=(M // tm, N // tn),
        in_specs=[
            pl.BlockSpec((tm, D), lambda i, j: (i, 0)),
            pl.BlockSpec((None, 1, D), lambda i, j: (l, 0, 0)),
            pl.BlockSpec((None, D, tn), lambda i, j: (l, 0, j)),
            pl.BlockSpec((None, D, tn), lambda i, j: (l, 0, j)),
            pl.BlockSpec((None, CONV_W, tn), lambda i, j: (l, 0, j)),
            pl.BlockSpec((None, 1, tn), lambda i, j: (l, 0, j)),
            buf_spec,
        ],
        out_specs=[pl.BlockSpec((tm, tn), lambda i, j: (i, j)), ust_spec],
        out_shape=[jax.ShapeDtypeStruct((M, N), BF16), ust_shape],
        scratch_shapes=scratch,
        compiler_params=_params(("arbitrary", "arbitrary")),
        name="ffn_up",
    )(x, g_all, wu_all, wg_all, cw_all, cb_all, buf)


def _ffn_down_kernel(x_ref, act_ref, w_ref, o_ref):
    o_ref[...] = x_ref[...] + _dot(act_ref[...], w_ref[...])


def _ffn_down(x, act, w_all, l, tm, tn):
    M, D = x.shape
    K = act.shape[-1]
    return pl.pallas_call(
        _ffn_down_kernel,
        grid=(M // tm, D // tn),
        in_specs=[
            pl.BlockSpec((tm, tn), lambda i, j: (i, j)),
            pl.BlockSpec((tm, K), lambda i, j: (i, 0)),
            pl.BlockSpec((None, K, tn), lambda i, j: (l, 0, j)),
        ],
        out_specs=pl.BlockSpec((tm, tn), lambda i, j: (i, j)),
        out_shape=jax.ShapeDtypeStruct((M, D), F32),
        compiler_params=_params(("arbitrary", "arbitrary")),
        name="ffn_down",
    )(x, act, w_all)


def _final_norm_kernel(x_ref, g_ref, o_ref):
    x = x_ref[...]
    r = lax.rsqrt(jnp.mean(x * x, axis=-1, keepdims=True) + EPS)
    o_ref[...] = x * r * g_ref[...]


def _final_norm(x, g, tm):
    M, D = x.shape
    return pl.pallas_call(
        _final_norm_kernel,
        grid=(M // tm,),
        in_specs=[pl.BlockSpec((tm, D), lambda i: (i, 0)), pl.BlockSpec((1, D), lambda i: (0, 0))],
        out_specs=pl.BlockSpec((tm, D), lambda i: (i, 0)),
        out_shape=jax.ShapeDtypeStruct((M, D), F32),
        compiler_params=_params(("arbitrary",)),
        name="final_norm",
    )(x, g)


def _rwkv_kernel(prkv_ref, plow_ref, shr_in, shl_in, s0_ref, mur_ref, mul_ref, w0_ref, wup_ref, a0_ref, aup_ref,
                 gup_ref, kk_ref, ka_ref, rk_ref, ln_ref, e_ref,
                 y_ref, sout_ref, shr_out, shl_out,
                 at_scr, rt_scr, kt_scr, bt_scr, v_scr, gam_scr, y_scr, *, G, C, t0):
    R = G * C
    W = R_WIDTH

    @pl.when(pl.program_id(1) == 0)
    def _():
        sout_ref[...] = s0_ref[...]
        shr_out[...] = shr_in[...]
        shl_out[...] = shl_in[...]

    def token_shift(p_ref, car_ref, mu_ref, w):
        p = p_ref[...].reshape(R, w)
        tl = _tloc(G, C, w)
        carry = jnp.concatenate([jnp.broadcast_to(car_ref[g], (C, w)) for g in range(G)], axis=0)
        prev = jnp.where(tl == t0, carry, pltpu.roll(p, 1, 0))
        for g in range(G):
            car_ref[g] = p[g * C + C - 1:g * C + C, :]
        return p + (prev - p) * mu_ref[...]

    ps = token_shift(prkv_ref, shr_out, mur_ref, 3 * W)
    lo = token_shift(plow_ref, shl_out, mul_ref, R_LOW)
    r = ps[:, 0:W]
    k = ps[:, W:2 * W]
    v = ps[:, 2 * W:3 * W]
    lw = -RWKV_DECAY_SCALE * jax.nn.sigmoid(w0_ref[...] + _dot(jnp.tanh(lo).astype(BF16), wup_ref[...]))
    a = jax.nn.sigmoid(a0_ref[...] + _dot(lo.astype(BF16), aup_ref[...]))
    gate = _dot(jax.nn.sigmoid(lo).astype(BF16), gup_ref[...])
    e = e_ref[...]
    kk = k * kk_ref[...]
    kk = kk / jnp.maximum(jnp.sqrt(_segsum(kk * kk, e)), 1e-12)
    k2 = k * (1.0 + (a - 1.0) * ka_ref[...])
    ka = kk * a
    tl = _tloc(G, C, W)
    if t0 > 0:
        valid = tl >= t0
        lw = jnp.where(valid, lw, 0.0)
        kk = jnp.where(valid, kk, 0.0)
        ka = jnp.where(valid, ka, 0.0)
        k2 = jnp.where(valid, k2, 0.0)
        v = jnp.where(valid, v, 0.0)
    cum = _cumsum_rows(lw, tl, C)
    gam = jnp.exp(cum)
    inv = jnp.exp(-cum)
    at_scr[...] = kk * jnp.exp(cum - lw)
    rt_scr[...] = r * gam
    kt_scr[...] = k2 * inv
    bt_scr[...] = ka * inv
    v_scr[...] = v
    gam_scr[...] = gam

    row = lax.broadcasted_iota(jnp.int32, (C, C), 0)
    col = lax.broadcasted_iota(jnp.int32, (C, C), 1)
    low_s = row > col
    low_i = row >= col

    def seq_body(g, carry):
        off = pl.multiple_of(g * C, 8)
        rows = pl.ds(off, C)
        at = at_scr[rows, :]
        rt = rt_scr[rows, :]
        kt = kt_scr[rows, :]
        bt = bt_scr[rows, :]
        vv = v_scr[rows, :]
        gl = gam_scr[pl.ds(off + C - 8, 8), :][7:8, :]
        ys = []
        for h in range(R_HEADS):
            sl = slice(R_HEAD_DIM * h, R_HEAD_DIM * (h + 1))
            ar = jnp.concatenate([at[:, sl], rt[:, sl]], axis=0).astype(BF16)
            ktb = kt[:, sl].astype(BF16)
            btb = bt[:, sl].astype(BF16)
            vb = vv[:, sl].astype(BF16)
            s0 = sout_ref[g, h]
            m_k = _dg(ar, ktb, NT)
            m_b = _dg(ar, btb, NT)
            m_s = _dg(ar, s0.astype(BF16), NT)
            n_mat = jnp.where(low_s, m_b[0:C], 0.0)
            rhs = m_s[0:C] + _dot(jnp.where(low_s, m_k[0:C], 0.0).astype(BF16), vb)
            u = rhs - _dot(n_mat.astype(BF16), rhs.astype(BF16))
            n_pow = n_mat
            span = 2
            while span < C:
                nb = n_pow.astype(BF16)
                n_pow = _dot(nb, nb)
                u = u + _dot(n_pow.astype(BF16), u.astype(BF16))
                span *= 2
            ub = u.astype(BF16)
            yh = m_s[C:2 * C] + _dot(jnp.where(low_i, m_k[C:2 * C], 0.0).astype(BF16), vb)
            yh = yh - _dot(jnp.where(low_i, m_b[C:2 * C], 0.0).astype(BF16), ub)
            ds = _dg(vb, ktb, TN) - _dg(ub, btb, TN)
            sout_ref[g, h] = (s0 + ds) * gl[:, sl]
            ys.append(yh)
        y_scr[rows, :] = jnp.concatenate(ys, axis=1)
        return carry

    lax.fori_loop(0, G, seq_body, 0)

    y = y_scr[...]
    inv_n = 1.0 / R_HEAD_DIM
    d = y - _segsum(y, e) * inv_n
    var = _segsum(d * d, e) * inv_n
    y = d * lax.rsqrt(var + RWKV_GN_EPS) * ln_ref[...]
    y = y + _segsum(r * k2 * rk_ref[...], e) * v
    y_ref[...] = (y * gate).reshape(G, C, W)


def _rwkv(p3, st, l_in, prm, l, cfg):
    B, T, _ = p3.shape
    G, C = cfg.g_mix, cfg.c_rwkv
    W = R_WIDTH
    shr_in, shl_in, s0 = st

    def pspec(w, off):
        return pl.BlockSpec((G, C, w), lambda i, c: (i, c, off // w))

    def lay(*shape):
        return pl.BlockSpec((None,) + shape, lambda i, c: (l,) + (0,) * len(shape))

    kern = functools.partial(_rwkv_kernel, G=G, C=C, t0=cfg.t0)
    R = G * C
    return pl.pallas_call(
        kern,
        grid=(B // G, T // C),
        in_specs=[
            pspec(3 * W, P_R), pspec(R_LOW, P_RL),
            pl.BlockSpec((None, G, 1, 3 * W), lambda i, c: (l_in, i, 0, 0)),
            pl.BlockSpec((None, G, 1, R_LOW), lambda i, c: (l_in, i, 0, 0)),
            pl.BlockSpec((None, G, R_HEADS, R_HEAD_DIM, R_HEAD_DIM), lambda i, c: (l_in, i, 0, 0, 0)),
            lay(1, 3 * W), lay(1, R_LOW), lay(1, W), lay(R_LOW, W), lay(1, W), lay(R_LOW, W), lay(R_LOW, W),
            lay(1, W), lay(1, W), lay(1, W), lay(1, W),
            pl.BlockSpec((W, W), lambda i, c: (0, 0)),
        ],
        out_specs=[
            pl.BlockSpec((G, C, W), lambda i, c: (i, c, 0)),
            pl.BlockSpec((G, R_HEADS, R_HEAD_DIM, R_HEAD_DIM), lambda i, c: (i, 0, 0, 0)),
            pl.BlockSpec((G, 1, 3 * W), lambda i, c: (i, 0, 0)),
            pl.BlockSpec((G, 1, R_LOW), lambda i, c: (i, 0, 0)),
        ],
        out_shape=[
            jax.ShapeDtypeStruct((B, T, W), F32),
            jax.ShapeDtypeStruct((B, R_HEADS, R_HEAD_DIM, R_HEAD_DIM), F32),
            jax.ShapeDtypeStruct((B, 1, 3 * W), F32),
            jax.ShapeDtypeStruct((B, 1, R_LOW), F32),
        ],
        scratch_shapes=[pltpu.VMEM((R, W), F32)] * 7,
        compiler_params=_params(("arbitrary", "arbitrary")),
        name="rwkv7",
    )(p3, p3, shr_in, shl_in, s0, prm["mu_rkv"], prm["mu_low"], prm["rwkv_w0"], prm["rwkv_w_up"], prm["rwkv_a0"],
      prm["rwkv_a_up"], prm["rwkv_g_up"], prm["rwkv_k_k"], prm["rwkv_k_a"], prm["rwkv_r_k"], prm["rwkv_ln"],
      prm["e_head"])


def _s5_kernel(u_ref, h0r_ref, h0i_ref, bbhr_ref, bblr_ref, bbhi_ref, bbli_ref, ccr_ref, cci_ref, ap_ref, pw_ref,
               d_ref, wglu_ref, bglu_ref,
               y_ref, hr_out, hi_out, hre_scr, him_scr, *, G, C, t0):
    R = G * C
    NB = S5_FLAT // S5_CHUNKS
    UB = S5_WIDTH // S5_CHUNKS

    @pl.when(pl.program_id(1) == 0)
    def _():
        hr_out[...] = h0r_ref[...]
        hi_out[...] = h0i_ref[...]

    u = u_ref[...].reshape(R, S5_WIDTH)
    uh = u.astype(BF16)
    ul = (u - uh.astype(F32)).astype(BF16)
    valid = _tloc(G, C, NB) >= t0
    for j in range(S5_CHUNKS):
        us = slice(UB * j, UB * (j + 1))
        br = _dot(uh[:, us], bbhr_ref[j]) + _dot(ul[:, us], bbhr_ref[j]) + _dot(uh[:, us], bblr_ref[j])
        bi = _dot(uh[:, us], bbhi_ref[j]) + _dot(ul[:, us], bbhi_ref[j]) + _dot(uh[:, us], bbli_ref[j])
        if t0 > 0:
            br = jnp.where(valid, br, 0.0)
            bi = jnp.where(valid, bi, 0.0)
        hre_scr[:, NB * j:NB * (j + 1)] = br
        him_scr[:, NB * j:NB * (j + 1)] = bi

    ap = ap_ref[...]
    steps = ((1, ap[0:1, :], ap[3:4, :]), (2, ap[1:2, :], ap[4:5, :]), (4, ap[2:3, :], ap[5:6, :]))
    pwr = pw_ref[0:S5_SCAN, :]
    pwi = pw_ref[S5_SCAN:2 * S5_SCAN, :]
    rid = lax.broadcasted_iota(jnp.int32, (S5_SCAN, S5_FLAT), 0)

    def seq_body(g, carry):
        def tile_body(i, hc):
            hcr, hci = hc
            rows = pl.ds(pl.multiple_of(g * C + i * S5_SCAN, S5_SCAN), S5_SCAN)
            xr = hre_scr[rows, :]
            xi = him_scr[rows, :]
            for sh, ar, ai in steps:
                sr = jnp.where(rid >= sh, pltpu.roll(xr, sh, 0), 0.0)
                si = jnp.where(rid >= sh, pltpu.roll(xi, sh, 0), 0.0)
                xr, xi = xr + ar * sr - ai * si, xi + ar * si + ai * sr
            xr, xi = xr + pwr * hcr - pwi * hci, xi + pwr * hci + pwi * hcr
            hre_scr[rows, :] = xr
            him_scr[rows, :] = xi
            return xr[S5_SCAN - 1:S5_SCAN, :], xi[S5_SCAN - 1:S5_SCAN, :]

        hcr, hci = lax.fori_loop(0, C // S5_SCAN, tile_body, (hr_out[g], hi_out[g]))
        hr_out[g] = hcr
        hi_out[g] = hci
        return carry

    lax.fori_loop(0, G, seq_body, 0)

    ys = []
    for j in range(S5_CHUNKS):
        hs = slice(NB * j, NB * (j + 1))
        ys.append(_dot(hre_scr[:, hs].astype(BF16), ccr_ref[j]) - _dot(him_scr[:, hs].astype(BF16), cci_ref[j]))
    y = jnp.concatenate(ys, axis=1) + d_ref[...] * u
    y = _gelu(y)
    y = y * jax.nn.sigmoid(_dot(y.astype(BF16), wglu_ref[...]) + bglu_ref[...])
    y_ref[...] = y.reshape(G, C, S5_WIDTH)


def _s5(p3, st, l_in, prm, l, cfg):
    B, T, _ = p3.shape
    G, C = cfg.g_s5, cfg.c_s5
    h0r, h0i = st
    NB = S5_FLAT // S5_CHUNKS
    UB = S5_WIDTH // S5_CHUNKS

    def lay(*shape):
        return pl.BlockSpec((None,) + shape, lambda i, c: (l,) + (0,) * len(shape))

    st_spec = pl.BlockSpec((None, G, 1, S5_FLAT), lambda i, c: (l_in, i, 0, 0))
    out_st = pl.BlockSpec((G, 1, S5_FLAT), lambda i, c: (i, 0, 0))
    kern = functools.partial(_s5_kernel, G=G, C=C, t0=cfg.t0)
    pw = prm["s5_pw"][cfg.t0]
    return pl.pallas_call(
        kern,
        grid=(B // G, T // C),
        in_specs=[
            pl.BlockSpec((G, C, S5_WIDTH), lambda i, c: (i, c, P_S5 // S5_WIDTH)),
            st_spec, st_spec,
            lay(S5_CHUNKS, UB, NB), lay(S5_CHUNKS, UB, NB), lay(S5_CHUNKS, UB, NB), lay(S5_CHUNKS, UB, NB),
            lay(S5_CHUNKS, NB, UB), lay(S5_CHUNKS, NB, UB),
            lay(8, S5_FLAT), lay(2 * S5_SCAN, S5_FLAT),
            lay(1, S5_WIDTH), lay(S5_WIDTH, S5_WIDTH), lay(1, S5_WIDTH),
        ],
        out_specs=[pl.BlockSpec((G, C, S5_WIDTH), lambda i, c: (i, c, 0)), out_st, out_st],
        out_shape=[
            jax.ShapeDtypeStruct((B, T, S5_WIDTH), F32),
            jax.ShapeDtypeStruct((B, 1, S5_FLAT), F32),
            jax.ShapeDtypeStruct((B, 1, S5_FLAT), F32),
        ],
        scratch_shapes=[pltpu.VMEM((G * C, S5_FLAT), F32)] * 2,
        compiler_params=_params(("arbitrary", "arbitrary")),
        name="s5",
    )(p3, h0r, h0i, prm["s5_bbh_re"], prm["s5_bbl_re"], prm["s5_bbh_im"], prm["s5_bbl_im"], prm["s5_cc_re"],
      prm["s5_cc_im"], prm["s5_apow"], pw, prm["s5_D"], prm["s5_w_glu"], prm["s5_b_glu"])


def _gla_core(q, kx, gdec, v, s0_ref, sout_ref, st_scr, q_scr, k_scr, b_scr, v_scr, o_scr, *, G, C, t0, K):
    H = G_HEADS
    V = G_VAL_DIM
    HK = H * K
    c = pl.program_id(1)

    @pl.when(c == 0)
    def _():
        for g in range(G):
            for h in range(H):
                st_scr[g, h] = s0_ref[g, h].T

    tl = _tloc(G, C, HK)
    if t0 > 0:
        valid = tl >= t0
        gdec = jnp.where(valid, gdec, 0.0)
        kx = jnp.where(valid, kx, 0.0)
    q_scr[...] = q
    k_scr[...] = kx
    b_scr[...] = _cumsum_rows(gdec, tl, C)
    v_scr[...] = v
    ti = lax.broadcasted_iota(jnp.int32, (C, HK), 0)

    def seq_body(g, carry):
        rows = pl.ds(pl.multiple_of(g * C, 8), C)
        qg = q_scr[rows, :]
        kg = k_scr[rows, :]
        bg = b_scr[rows, :]
        vg = v_scr[rows, :]
        b_last = bg[C - 1:C, :]
        qe = (qg * jnp.exp(bg)).astype(BF16)
        kd = (kg * jnp.exp(b_last - bg)).astype(BF16)
        dec_last = jnp.exp(b_last)
        intra = [jnp.zeros((C, V), F32) for _ in range(H)]
        for j in range(C):
            causal = ti >= j
            e = jnp.where(causal, qg * kg[j:j + 1, :] * jnp.exp(jnp.where(causal, bg - bg[j:j + 1, :], 0.0)), 0.0)
            for h in range(H):
                a = jnp.sum(e[:, K * h:K * (h + 1)], axis=-1, keepdims=True)
                intra[h] = intra[h] + a * vg[j:j + 1, V * h:V * (h + 1)]
        outs = []
        for h in range(H):
            ks = slice(K * h, K * (h + 1))
            vs = slice(V * h, V * (h + 1))
            st = st_scr[g, h]
            outs.append(_dg(qe[:, ks], st.astype(BF16), NT) + intra[h])
            st_scr[g, h] = st * dec_last[:, ks] + _dg(vg[:, vs].astype(BF16), kd[:, ks], TN)
        o_scr[rows, :] = jnp.concatenate(outs, axis=1)
        return carry

    lax.fori_loop(0, G, seq_body, 0)

    @pl.when(c == pl.num_programs(1) - 1)
    def _():
        for g in range(G):
            for h in range(H):
                sout_ref[g, h] = st_scr[g, h].T


def _head_norm_gate(o, norm, gate_raw):
    V = G_VAL_DIM
    ys = []
    for h in range(G_HEADS):
        oh = o[:, V * h:V * (h + 1)]
        ys.append(oh * lax.rsqrt(jnp.mean(oh * oh, axis=-1, keepdims=True) + EPS))
    y = jnp.concatenate(ys, axis=1) * norm
    return y * (gate_raw * jax.nn.sigmoid(gate_raw))


def _hgrn_kernel(q_ref, f_ref, i_ref, g_ref, lb_ref, norm_ref, s0_ref, y_ref, sout_ref,
                 st_scr, q_scr, k_scr, b_scr, v_scr, o_scr, *, G, C, t0, layer):
    R = G * C
    W = H_WIDTH
    lbr = lb_ref[...]
    ex = jnp.exp(lbr - jnp.max(lbr, axis=0, keepdims=True))
    sm = ex / jnp.sum(ex, axis=0, keepdims=True)
    lb = jnp.zeros((1, W), F32)
    for i in range(1, layer + 1):
        lb = lb + sm[i:i + 1, :]
    qr = q_ref[...].reshape(R, W)
    kx = jnp.minimum((1.0 - lb) * jax.nn.sigmoid(-f_ref[...].reshape(R, W)), HGRN_MAX_INPUT)
    _gla_core(qr * jax.nn.sigmoid(qr), kx, jnp.log1p(-kx), i_ref[...].reshape(R, W), s0_ref, sout_ref,
              st_scr, q_scr, k_scr, b_scr, v_scr, o_scr, G=G, C=C, t0=t0, K=H_HEAD_DIM)
    y_ref[...] = _head_norm_gate(o_scr[...], norm_ref[...], g_ref[...].reshape(R, W)).reshape(G, C, W)


def _gla_kernel(v_ref, gate_ref, q_ref, k_ref, gl_ref, gkup_ref, gkb_ref, norm_ref, s0_ref, y_ref, sout_ref,
                st_scr, q_scr, k_scr, b_scr, v_scr, o_scr, *, G, C, t0):
    R = G * C
    z = _dot(gl_ref[...].reshape(R, GL_PAD).astype(BF16), gkup_ref[...]) + gkb_ref[...]
    gdec = -(jnp.maximum(-z, 0.0) + jnp.log1p(jnp.exp(-jnp.abs(z)))) / GLA_GATE_NORM
    _gla_core(q_ref[...].reshape(R, G_QK) * (G_KEY_DIM ** -0.5), k_ref[...].reshape(R, G_QK), gdec,
              v_ref[...].reshape(R, G_WIDTH), s0_ref, sout_ref, st_scr, q_scr, k_scr, b_scr, v_scr, o_scr,
              G=G, C=C, t0=t0, K=G_KEY_DIM)
    y_ref[...] = _head_norm_gate(o_scr[...], norm_ref[...], gate_ref[...].reshape(R, G_WIDTH)).reshape(G, C, G_WIDTH)


def _gla_like(p3, s0, l_in, prm, l, cfg, hgrn):
    B, T, _ = p3.shape
    G, C = cfg.g_mix, cfg.c_gla
    H, V = G_HEADS, G_VAL_DIM
    K = H_HEAD_DIM if hgrn else G_KEY_DIM
    HK = H * K
    R = G * C

    def pspec(w, off):
        return pl.BlockSpec((G, C, w), lambda i, c: (i, c, off // w))

    def lay(*shape):
        return pl.BlockSpec((None,) + shape, lambda i, c: (l,) + (0,) * len(shape))

    st_in = pl.BlockSpec((None, G, H, K, V), lambda i, c: (l_in, i, 0, 0, 0))
    if hgrn:
        kern = functools.partial(_hgrn_kernel, G=G, C=C, t0=cfg.t0, layer=l)
        in_specs = [pspec(HK, P_H), pspec(HK, P_H + HK), pspec(HK, P_H + 2 * HK), pspec(HK, P_H + 3 * HK),
                    pl.BlockSpec((DEPTH, HK), lambda i, c: (0, 0)), lay(1, HK), st_in]
        args = (p3, p3, p3, p3, prm["hgrn_lb_raw"], prm["hgrn_norm"], s0)
        name = "hgrn2"
    else:
        kern = functools.partial(_gla_kernel, G=G, C=C, t0=cfg.t0)
        in_specs = [pspec(G_WIDTH, P_GV), pspec(G_WIDTH, P_GG), pspec(HK, P_GQ), pspec(HK, P_GK),
                    pspec(GL_PAD, P_GL), lay(GL_PAD, HK), lay(1, HK), lay(1, G_WIDTH), st_in]
        args = (p3, p3, p3, p3, p3, prm["gla_gk_up"], prm["gla_gk_b"], prm["gla_norm"], s0)
        name = "gla"
    return pl.pallas_call(
        kern,
        grid=(B // G, T // C),
        in_specs=in_specs,
        out_specs=[pl.BlockSpec((G, C, H * V), lambda i, c: (i, c, 0)),
                   pl.BlockSpec((G, H, K, V), lambda i, c: (i, 0, 0, 0))],
        out_shape=[jax.ShapeDtypeStruct((B, T, H * V), F32), jax.ShapeDtypeStruct((B, H, K, V), F32)],
        scratch_shapes=[pltpu.VMEM((G, H, V, K), F32), pltpu.VMEM((R, HK), F32), pltpu.VMEM((R, HK), F32),
                        pltpu.VMEM((R, HK), F32), pltpu.VMEM((R, H * V), F32), pltpu.VMEM((R, H * V), F32)],
        compiler_params=_params(("arbitrary", "arbitrary")),
        name=name,
    )(*args)


def _s5_tables(A_re, A_im, log_dt, B_re, B_im, C_re, C_im, t0s):
    L = A_re.shape[0]
    A_re = A_re.astype(F32)
    A_im = A_im.astype(F32)
    dt = jnp.exp(log_dt.astype(F32))[..., None]
    mag = jnp.exp(A_re * dt)
    ab_re = mag * jnp.cos(A_im * dt)
    ab_im = mag * jnp.sin(A_im * dt)
    den = A_re * A_re + A_im * A_im
    n_re = ab_re - 1.0
    co_re = (n_re * A_re + ab_im * A_im) / den
    co_im = (ab_im * A_re - n_re * A_im) / den
    B_re = B_re.astype(F32)
    B_im = B_im.astype(F32)
    bb_re = co_re[..., None] * B_re - co_im[..., None] * B_im
    bb_im = co_re[..., None] * B_im + co_im[..., None] * B_re
    gpc = S5_GROUPS // S5_CHUNKS
    eye = jnp.eye(gpc, dtype=F32)

    def block_in(bb):
        t = bb.reshape(L, S5_CHUNKS, gpc, S5_STATE, S5_GROUP_CH).transpose(0, 1, 2, 4, 3)
        t = t[:, :, :, :, None, :] * eye[None, None, :, None, :, None]
        return t.reshape(L, S5_CHUNKS, gpc * S5_GROUP_CH, gpc * S5_STATE)

    def block_out(cc):
        t = cc.astype(F32).reshape(L, S5_CHUNKS, gpc, S5_GROUP_CH, S5_STATE).transpose(0, 1, 2, 4, 3)
        t = t[:, :, :, :, None, :] * eye[None, None, :, None, :, None]
        return t.reshape(L, S5_CHUNKS, gpc * S5_STATE, gpc * S5_GROUP_CH)

    def split(x):
        hi = x.astype(BF16)
        return hi, (x - hi.astype(F32)).astype(BF16)

    bbh_re, bbl_re = split(block_in(bb_re))
    bbh_im, bbl_im = split(block_in(bb_im))

    def power(m):
        mg = jnp.exp(m * dt * A_re)
        return (mg * jnp.cos(m * dt * A_im)).reshape(L, S5_FLAT), (mg * jnp.sin(m * dt * A_im)).reshape(L, S5_FLAT)

    p1, p2, p4 = power(1.0), power(2.0), power(4.0)
    zero = jnp.zeros((L, S5_FLAT), F32)
    apow = jnp.stack([p1[0], p2[0], p4[0], p1[1], p2[1], p4[1], zero, zero], axis=1)
    pws = {}
    for t0 in t0s:
        first = t0 % S5_SCAN
        res, ims = [], []
        for r in range(S5_SCAN):
            if r < first:
                res.append(zero)
                ims.append(zero)
            else:
                pr, pi = power(float(r - first + 1))
                res.append(pr)
                ims.append(pi)
        pws[t0] = jnp.stack(res + ims, axis=1)
    return dict(s5_bbh_re=bbh_re, s5_bbl_re=bbl_re, s5_bbh_im=bbh_im, s5_bbl_im=bbl_im,
                s5_cc_re=block_out(C_re).astype(BF16), s5_cc_im=block_out(C_im).astype(BF16),
                s5_apow=apow, s5_pw=pws)


def _prep_params(raw, t0s):
    L = raw["w_in"].shape[0]
    W = R_WIDTH
    w_in = raw["w_in"]
    o_s5 = R_COLS
    o_h = o_s5 + S5_WIDTH
    o_g = o_h + 4 * H_WIDTH
    o_gv = o_g + 2 * G_QK
    o_gl = o_gv + G_WIDTH
    o_gg = o_gl + G_GATE_RANK
    pad = jnp.zeros(w_in.shape[:2] + (GL_PAD - G_GATE_RANK,), w_in.dtype)
    w_in_p = jnp.concatenate([
        w_in[..., 0:3 * W], w_in[..., o_s5:o_h], w_in[..., o_h:o_g], w_in[..., o_gv:o_gl],
        w_in[..., o_gg:o_gg + G_WIDTH], w_in[..., o_g:o_g + G_QK], w_in[..., o_g + G_QK:o_gv],
        w_in[..., 3 * W:R_COLS], w_in[..., o_gl:o_gg], pad], axis=-1).astype(BF16)
    assert w_in_p.shape[-1] == P_COLS

    def row3(x):
        return x.reshape(L, 1, -1).astype(F32)

    def low_pad(w, off):
        return jnp.pad(w, ((0, 0), (off, R_LOW - off - w.shape[1]), (0, 0))).astype(BF16)

    ff = FF_PAD - D_FF
    w_up = raw["ffn_w_up"]
    head = jnp.arange(W) // R_HEAD_DIM
    prm = dict(
        norm_mix=row3(raw["norm_mix"]), norm_ffn=row3(raw["norm_ffn"]),
        w_in=w_in_p, w_out=raw["w_out"].astype(BF16),
        mu_rkv=row3(raw["rwkv_mu"][:, 0:3 * W]), mu_low=row3(raw["rwkv_mu"][:, 3 * W:]),
        rwkv_w0=row3(raw["rwkv_w0"]), rwkv_a0=row3(raw["rwkv_a0"]),
        rwkv_w_up=low_pad(raw["rwkv_w_up"], 0), rwkv_a_up=low_pad(raw["rwkv_a_up"], R_DECAY_RANK),
        rwkv_g_up=low_pad(raw["rwkv_g_up"], R_DECAY_RANK + R_ICL_RANK),
        rwkv_k_k=row3(raw["rwkv_k_k"]), rwkv_k_a=row3(raw["rwkv_k_a"]), rwkv_r_k=row3(raw["rwkv_r_k"]),
        rwkv_ln=row3(raw["rwkv_ln"]),
        e_head=(head[:, None] == head[None, :]).astype(BF16),
        s5_D=row3(raw["s5_D"]), s5_w_glu=raw["s5_w_glu"].astype(BF16), s5_b_glu=row3(raw["s5_b_glu"]),
        hgrn_lb_raw=raw["hgrn_lower_bounds"].astype(F32), hgrn_norm=row3(raw["hgrn_norm"]),
        gla_gk_up=jnp.pad(raw["gla_gk_up"], ((0, 0), (0, GL_PAD - G_GATE_RANK), (0, 0))).astype(BF16),
        gla_gk_b=row3(raw["gla_gk_b"]), gla_norm=row3(raw["gla_norm"]),
        ffn_wu=jnp.pad(w_up[..., :D_FF], ((0, 0), (0, 0), (0, ff))).astype(BF16),
        ffn_wg=jnp.pad(w_up[..., D_FF:], ((0, 0), (0, 0), (0, ff))).astype(BF16),
        ffn_conv_w=jnp.pad(raw["ffn_conv_w"], ((0, 0), (0, 0), (0, ff))).astype(F32),
        ffn_conv_b=jnp.pad(raw["ffn_conv_b"], ((0, 0), (0, ff))).reshape(L, 1, FF_PAD).astype(F32),
        ffn_w_down=jnp.pad(raw["ffn_w_down"], ((0, 0), (0, ff), (0, 0))).astype(BF16),
    )
    prm.update(_s5_tables(raw["s5_A_re"], raw["s5_A_im"], raw["s5_log_dt"], raw["s5_B_re"], raw["s5_B_im"],
                          raw["s5_C_re"], raw["s5_C_im"], t0s))
    return prm


def _layer(x, st, l_in, prm, l, cfg):
    shr, shl, s_rw, h_re, h_im, s_hg, s_gl, buf = st
    p = _norm_matmul(x, prm["norm_mix"], prm["w_in"], l, cfg.tm, 1024)
    p3 = p.reshape(cfg.B, cfg.T, P_COLS)
    y_r, s_rw_n, shr_n, shl_n = _rwkv(p3, (shr, shl, s_rw), l_in, prm, l, cfg)
    y_s, h_re_n, h_im_n = _s5(p3, (h_re, h_im), l_in, prm, l, cfg)
    y_h, s_hg_n = _gla_like(p3, s_hg, l_in, prm, l, cfg, True)
    y_g, s_gl_n = _gla_like(p3, s_gl, l_in, prm, l, cfg, False)
    M = cfg.B * cfg.T
    ys = [y.reshape(M, GROUP_WIDTH) for y in (y_r, y_s, y_h, y_g)]
    x = _out_proj(x, ys, prm["w_out"], l, cfg.tm, 1024)
    act, ust = _ffn_up(x, prm["norm_ffn"], prm["ffn_wu"], prm["ffn_wg"], prm["ffn_conv_w"], prm["ffn_conv_b"], buf,
                       l, l_in, cfg, 512)
    x = _ffn_down(x, act, prm["ffn_w_down"], l, cfg.tm, 512)
    return x, (shr_n, shl_n, s_rw_n, h_re_n, h_im_n, s_hg_n, s_gl_n, ust)


def _state_outputs(new, cfg):
    shr, shl, s_rw, h_re, h_im, s_hg, s_gl, ust = new
    B = cfg.B
    shift = jnp.concatenate([shr, shl], axis=-1).reshape(B, R_COLS)
    if cfg.inject:
        conv = ust.reshape(B, cfg.T, FF_PAD)[:, cfg.T - (CONV_W - 1):, :D_FF]
    else:
        tps = cfg.T // cfg.tm
        conv = ust[tps - 1::tps, :, :D_FF]
    return (s_rw, shift, h_re.reshape(B, S5_GROUPS, S5_STATE), h_im.reshape(B, S5_GROUPS, S5_STATE), s_hg, s_gl,
            conv)


PROMPT_PAD = 0
SAMPLE_PAD = 4


def kernel(x_prompt, x_sample, state_rwkv, state_rwkv_shift, state_s5_re, state_s5_im, state_hgrn, state_gla, state_ffn_conv, meta_tokens, norm_mix, w_in, w_out, rwkv_mu, rwkv_w0, rwkv_w_up, rwkv_a0, rwkv_a_up, rwkv_g_up, rwkv_k_k, rwkv_k_a, rwkv_r_k, rwkv_ln, s5_A_re, s5_A_im, s5_log_dt, s5_B_re, s5_B_im, s5_C_re, s5_C_im, s5_D, s5_w_glu, s5_b_glu, hgrn_lower_bounds, hgrn_norm, gla_gk_up, gla_gk_b, gla_norm, norm_ffn, ffn_w_up, ffn_conv_w, ffn_conv_b, ffn_w_down, norm_final):
    L = DEPTH
    Bp, Sp, D = x_prompt.shape
    Bs, Ss, _ = x_sample.shape
    Tp = N_META + Sp
    Ts = SAMPLE_PAD + Ss
    raw = dict(norm_mix=norm_mix, w_in=w_in, w_out=w_out, rwkv_mu=rwkv_mu, rwkv_w0=rwkv_w0, rwkv_w_up=rwkv_w_up,
               rwkv_a0=rwkv_a0, rwkv_a_up=rwkv_a_up, rwkv_g_up=rwkv_g_up, rwkv_k_k=rwkv_k_k, rwkv_k_a=rwkv_k_a,
               rwkv_r_k=rwkv_r_k, rwkv_ln=rwkv_ln, s5_A_re=s5_A_re, s5_A_im=s5_A_im, s5_log_dt=s5_log_dt,
               s5_B_re=s5_B_re, s5_B_im=s5_B_im, s5_C_re=s5_C_re, s5_C_im=s5_C_im, s5_D=s5_D, s5_w_glu=s5_w_glu,
               s5_b_glu=s5_b_glu, hgrn_lower_bounds=hgrn_lower_bounds, hgrn_norm=hgrn_norm, gla_gk_up=gla_gk_up,
               gla_gk_b=gla_gk_b, gla_norm=gla_norm, norm_ffn=norm_ffn, ffn_w_up=ffn_w_up, ffn_conv_w=ffn_conv_w,
               ffn_conv_b=ffn_conv_b, ffn_w_down=ffn_w_down)
    prm = _prep_params(raw, (PROMPT_PAD, SAMPLE_PAD))

    cfg_p = _Cfg(B=Bp, T=Tp, t0=PROMPT_PAD, tm=Tp // 3, g_mix=Bp, c_rwkv=48, c_gla=16, g_s5=1, c_s5=Tp // 3,
                 inject=False)
    cfg_s = _Cfg(B=Bs, T=Ts, t0=SAMPLE_PAD, tm=Bs * Ts, g_mix=8, c_rwkv=Ts, c_gla=Ts, g_s5=32, c_s5=Ts,
                 inject=True)

    dt = x_prompt.dtype
    meta = jnp.broadcast_to(meta_tokens.astype(dt)[None], (Bp, N_META, D))
    xp = jnp.concatenate([meta, x_prompt], axis=1).reshape(Bp * Tp, D)
    xs = jnp.pad(x_sample, ((0, 0), (SAMPLE_PAD, 0), (0, 0))).reshape(Bs * Ts, D)

    W = R_WIDTH
    st_p = (jnp.zeros((1, Bp, 1, 3 * W), F32), jnp.zeros((1, Bp, 1, R_LOW), F32),
            jnp.zeros((1, Bp, R_HEADS, R_HEAD_DIM, R_HEAD_DIM), F32),
            jnp.zeros((1, Bp, 1, S5_FLAT), F32), jnp.zeros((1, Bp, 1, S5_FLAT), F32),
            jnp.zeros((1, Bp, H_HEADS, H_HEAD_DIM, H_HEAD_DIM), F32),
            jnp.zeros((1, Bp, G_HEADS, G_KEY_DIM, G_VAL_DIM), F32),
            jnp.zeros((1, Bp, CONV_W - 1, FF_PAD), F32))
    buf_s = jnp.pad(state_ffn_conv, ((0, 0), (0, 0), (SAMPLE_PAD - (CONV_W - 1), Ss), (0, FF_PAD - D_FF)))
    st_s = (state_rwkv_shift[:, :, None, 0:3 * W], state_rwkv_shift[:, :, None, 3 * W:], state_rwkv,
            state_s5_re.reshape(L, Bs, 1, S5_FLAT), state_s5_im.reshape(L, Bs, 1, S5_FLAT),
            state_hgrn, state_gla, buf_s.reshape(L, Bs * Ts, FF_PAD))

    outs_p, outs_s = [], []
    for l in range(L):
        xp, new_p = _layer(xp, st_p, 0, prm, l, cfg_p)
        xs, new_s = _layer(xs, st_s, l, prm, l, cfg_s)
        outs_p.append(_state_outputs(new_p, cfg_p))
        outs_s.append(_state_outputs(new_s, cfg_s))

    g_fin = norm_final.reshape(1, D).astype(F32)
    y_prompt = _final_norm(xp, g_fin, cfg_p.tm).reshape(Bp, Tp, D)[:, N_META:]
    y_sample = _final_norm(xs, g_fin, cfg_s.tm).reshape(Bs, Ts, D)[:, SAMPLE_PAD:]
    sp = [jnp.stack([o[i] for o in outs_p]) for i in range(7)]
    ss = [jnp.stack([o[i] for o in outs_s]) for i in range(7)]
    return (y_prompt, y_sample, sp[0], ss[0], sp[1], ss[1], sp[2], ss[2], sp[3], ss[3],
            sp[4], ss[4], sp[5], ss[5], sp[6], ss[6])
```

```python
import functools
import math
from typing import NamedTuple

import jax
import jax.numpy as jnp
from jax import lax
from jax.experimental import pallas as pl
from jax.experimental.pallas import tpu as pltpu

F32 = jnp.float32
BF16 = jnp.bfloat16

D_MODEL = 2048
DEPTH = 4
N_META = 16
GROUP_WIDTH = D_MODEL // 4
EPS = 1e-6
R_HEAD_DIM = 64
R_HEADS = GROUP_WIDTH // R_HEAD_DIM
R_WIDTH = R_HEADS * R_HEAD_DIM
R_DECAY_RANK = 64
R_ICL_RANK = 64
R_GATE_RANK = 128
R_LOW = R_DECAY_RANK + R_ICL_RANK + R_GATE_RANK
R_COLS = 3 * R_WIDTH + R_LOW
RWKV_DECAY_SCALE = 0.606531
RWKV_GN_EPS = 64e-5
S5_GROUP_CH = 16
S5_GROUPS = GROUP_WIDTH // S5_GROUP_CH
S5_WIDTH = S5_GROUPS * S5_GROUP_CH
S5_STATE = 64
S5_FLAT = S5_GROUPS * S5_STATE
H_HEAD_DIM = 128
H_HEADS = GROUP_WIDTH // H_HEAD_DIM
H_WIDTH = H_HEADS * H_HEAD_DIM
HGRN_MAX_INPUT = 1.0 - 1e-4
G_VAL_DIM = 128
G_HEADS = GROUP_WIDTH // G_VAL_DIM
G_KEY_DIM = G_VAL_DIM // 2
G_WIDTH = G_HEADS * G_VAL_DIM
G_QK = G_HEADS * G_KEY_DIM
G_GATE_RANK = 16
GLA_GATE_NORM = 16.0
D_FF = ((8 * D_MODEL // 3 + 127) // 128) * 128
CONV_W = 3

P_R = 0
P_S5 = 3 * R_WIDTH
P_H = P_S5 + S5_WIDTH
P_GV = P_H + 4 * H_WIDTH
P_GG = P_GV + G_WIDTH
G_KEY_PAD = G_VAL_DIM
G_QK_PAD = G_HEADS * G_KEY_PAD
P_GQ = P_GG + G_WIDTH
P_GK = P_GQ + G_QK_PAD
P_RL = P_GK + G_QK_PAD
P_GL = P_RL + R_LOW
GL_PAD = 256
P_COLS = P_GL + GL_PAD
FF_PAD = 5632
S5_CHUNKS = 4
S5_SCAN = 8

VMEM_LIMIT = 56 * 1024 * 1024

NT = (((1,), (1,)), ((), ()))
TN = (((0,), (0,)), ((), ()))


class _Cfg(NamedTuple):
    B: int
    T: int
    t0: int
    tm: int
    g_mix: int
    u_rwkv: int
    c_rwkv: int
    cb_gla: int
    c_gla: int
    g_s5: int
    c_s5: int
    inject: bool


def _dot(a, b):
    return jnp.dot(a, b, preferred_element_type=F32)


def _dg(a, b, dims):
    return lax.dot_general(a, b, dims, preferred_element_type=F32)


def _gelu(x):
    c = math.sqrt(2.0 / math.pi)
    return x * (0.5 * (1.0 + jnp.tanh(c * (x + 0.044715 * (x * x * x)))))


def _tloc(G, C, W):
    t = lax.broadcasted_iota(jnp.int32, (C, W), 0)
    return t if G == 1 else jnp.concatenate([t] * G, axis=0)


def _cumsum_rows(x, tl, C):
    sh = 1
    while sh < C:
        x = x + jnp.where(tl >= sh, pltpu.roll(x, sh, 0), 0.0)
        sh *= 2
    return x


def _segsum(x, e):
    x1 = x.astype(BF16)
    r1 = x - x1.astype(F32)
    x2 = r1.astype(BF16)
    x3 = (r1 - x2.astype(F32)).astype(BF16)
    return _dot(x1, e) + _dot(x2, e) + _dot(x3, e)


def _params(sem):
    return pltpu.CompilerParams(dimension_semantics=sem, vmem_limit_bytes=VMEM_LIMIT)


def _norm_matmul_kernel(x_ref, g_ref, w_ref, o_ref, h_scr):
    @pl.when(pl.program_id(1) == 0)
    def _():
        x = x_ref[...]
        r = lax.rsqrt(jnp.mean(x * x, axis=-1, keepdims=True) + EPS)
        h_scr[...] = (x * r * g_ref[...]).astype(BF16)

    o_ref[...] = _dot(h_scr[...], w_ref[...])


def _norm_matmul(x, g_all, w_all, l, tm, tn):
    M, D = x.shape
    N = w_all.shape[-1]
    return pl.pallas_call(
        _norm_matmul_kernel,
        grid=(M // tm, N // tn),
        in_specs=[
            pl.BlockSpec((tm, D), lambda i, j: (i, 0)),
            pl.BlockSpec((None, 1, D), lambda i, j: (l, 0, 0)),
            pl.BlockSpec((None, D, tn), lambda i, j: (l, 0, j)),
        ],
        out_specs=pl.BlockSpec((tm, tn), lambda i, j: (i, j)),
        out_shape=jax.ShapeDtypeStruct((M, N), F32),
        scratch_shapes=[pltpu.VMEM((tm, D), BF16)],
        compiler_params=_params(("arbitrary", "arbitrary")),
        name="in_proj",
    )(x, g_all, w_all)


def _out_proj_kernel(x_ref, y0, y1, y2, y3, w0, w1, w2, w3, o_ref):
    acc = _dot(y0[...].astype(BF16), w0[...])
    acc = acc + _dot(y1[...].astype(BF16), w1[...])
    acc = acc + _dot(y2[...].astype(BF16), w2[...])
    acc = acc + _dot(y3[...].astype(BF16), w3[...])
    o_ref[...] = x_ref[...] + acc


def _out_proj(x, ys, w_all, l, tm, tn):
    M, D = x.shape
    W = GROUP_WIDTH
    yspec = pl.BlockSpec((tm, W), lambda i, j: (i, 0))
    wspecs = [pl.BlockSpec((None, W, tn), functools.partial(lambda i, j, q: (l, q, j), q=q)) for q in range(4)]
    return pl.pallas_call(
        _out_proj_kernel,
        grid=(M // tm, D // tn),
        in_specs=[pl.BlockSpec((tm, tn), lambda i, j: (i, j))] + [yspec] * 4 + wspecs,
        out_specs=pl.BlockSpec((tm, tn), lambda i, j: (i, j)),
        out_shape=jax.ShapeDtypeStruct((M, D), F32),
        compiler_params=_params(("arbitrary", "arbitrary")),
        name="out_proj",
    )(x, *ys, w_all, w_all, w_all, w_all)


def _ffn_up_kernel(x_ref, g_ref, wu_ref, wg_ref, cw_ref, cb_ref, buf_ref, act_ref, ust_ref, h_scr, *car,
                   tm, tn, tiles_per_seq, t0, seq_rows, inject):
    i = pl.program_id(0)
    j = pl.program_id(1)

    @pl.when(j == 0)
    def _():
        x = x_ref[...]
        r = lax.rsqrt(jnp.mean(x * x, axis=-1, keepdims=True) + EPS)
        h_scr[...] = (x * r * g_ref[...]).astype(BF16)

    u = _dot(h_scr[...], wu_ref[...])
    rows = lax.broadcasted_iota(jnp.int32, (tm, tn), 0)
    if inject:
        tl = rows & (seq_rows - 1)
        u = jnp.where((tl == t0 - 2) | (tl == t0 - 1), buf_ref[...], u)
        p1 = pltpu.roll(u, 1, 0)
        p2 = pltpu.roll(u, 2, 0)
        ust_ref[...] = u
    else:
        (car_scr,) = car

        @pl.when(i % tiles_per_seq == 0)
        def _():
            car_scr[j, 0:2, :] = buf_ref[...]

        c0 = car_scr[j, 0:1, :]
        c1 = car_scr[j, 1:2, :]
        p1 = jnp.where(rows == 0, c1, pltpu.roll(u, 1, 0))
        p2 = jnp.where(rows == 0, c0, jnp.where(rows == 1, c1, pltpu.roll(u, 2, 0)))
        last = u[tm - 2:tm, :]
        car_scr[j, 0:2, :] = last
        ust_ref[...] = last
    cw = cw_ref[...]
    c = cb_ref[...] + cw[0:1, :] * p2
    c = c + cw[1:2, :] * p1
    c = c + cw[2:3, :] * u
    act_ref[...] = (_gelu(c) * _dot(h_scr[...], wg_ref[...])).astype(BF16)


def _ffn_up(x, g_all, wu_all, wg_all, cw_all, cb_all, buf, l, l_in, cfg, tn):
    M, D = x.shape
    tm = cfg.tm
    N = wu_all.shape[-1]
    tps = cfg.T // tm if not cfg.inject else 1
    kern = functools.partial(_ffn_up_kernel, tm=tm, tn=tn, tiles_per_seq=tps, t0=cfg.t0, seq_rows=cfg.T,
                             inject=cfg.inject)
    if cfg.inject:
        buf_spec = pl.BlockSpec((None, tm, tn), lambda i, j: (l_in, i, j))
        ust_spec = pl.BlockSpec((tm, tn), lambda i, j: (i, j))
        ust_shape = jax.ShapeDtypeStruct((M, N), F32)
        scratch = [pltpu.VMEM((tm, D), BF16)]
    else:
        buf_spec = pl.BlockSpec((None, None, CONV_W - 1, tn), lambda i, j: (l_in, i // tps, 0, j))
        ust_spec = pl.BlockSpec((None, CONV_W - 1, tn), lambda i, j: (i, 0, j))
        ust_shape = jax.ShapeDtypeStruct((M // tm, CONV_W - 1, N), F32)
        scratch = [pltpu.VMEM((tm, D), BF16), pltpu.VMEM((N // tn, 8, tn), F32)]
    return pl.pallas_call(
        kern,
        grid=(M // tm, N // tn),
        in_specs=[
            pl.BlockSpec((tm, D), lambda i, j: (i, 0)),
            pl.BlockSpec((None, 1, D), lambda i, j: (l, 0, 0)),
            pl.BlockSpec((None, D, tn), lambda i, j: (l, 0, j)),
            pl.BlockSpec((None, D, tn), lambda i, j: (l, 0, j)),
            pl.BlockSpec((None, CONV_W, tn), lambda i, j: (l, 0, j)),
            pl.BlockSpec((None, 1, tn), lambda i, j: (l, 0, j)),
            buf_spec,
        ],
        out_specs=[pl.BlockSpec((tm, tn), lambda i, j: (i, j)), ust_spec],
        out_shape=[jax.ShapeDtypeStruct((M, N), BF16), ust_shape],
        scratch_shapes=scratch,
        compiler_params=_params(("arbitrary", "arbitrary")),
        name="ffn_up",
    )(x, g_all, wu_all, wg_all, cw_all, cb_all, buf)


def _ffn_down_kernel(x_ref, act_ref, w_ref, o_ref):
    o_ref[...] = x_ref[...] + _dot(act_ref[...], w_ref[...])


def _ffn_down(x, act, w_all, l, tm, tn):
    M, D = x.shape
    K = act.shape[-1]
    return pl.pallas_call(
        _ffn_down_kernel,
        grid=(M // tm, D // tn),
        in_specs=[
            pl.BlockSpec((tm, tn), lambda i, j: (i, j)),
            pl.BlockSpec((tm, K), lambda i, j: (i, 0)),
            pl.BlockSpec((None, K, tn), lambda i, j: (l, 0, j)),
        ],
        out_specs=pl.BlockSpec((tm, tn), lambda i, j: (i, j)),
        out_shape=jax.ShapeDtypeStruct((M, D), F32),
        compiler_params=_params(("arbitrary", "arbitrary")),
        name="ffn_down",
    )(x, act, w_all)


def _final_norm_kernel(x_ref, g_ref, o_ref):
    x = x_ref[...]
    r = lax.rsqrt(jnp.mean(x * x, axis=-1, keepdims=True) + EPS)
    o_ref[...] = x * r * g_ref[...]


def _final_norm(x, g, tm):
    M, D = x.shape
    return pl.pallas_call(
        _final_norm_kernel,
        grid=(M // tm,),
        in_specs=[pl.BlockSpec((tm, D), lambda i: (i, 0)), pl.BlockSpec((1, D), lambda i: (0, 0))],
        out_specs=pl.BlockSpec((tm, D), lambda i: (i, 0)),
        out_shape=jax.ShapeDtypeStruct((M, D), F32),
        compiler_params=_params(("arbitrary",)),
        name="final_norm",
    )(x, g)


def _rwkv_kernel(prkv_ref, plow_ref, shr_in, shl_in, s0_ref, mur_ref, mul_ref, w0_ref, wup_ref, a0_ref, aup_ref,
                 gup_ref, kk_ref, ka_ref, rk_ref, ln_ref, e_ref,
                 y_ref, sout_ref, shr_out, shl_out,
                 at_scr, rt_scr, kt_scr, bt_scr, v_scr, gam_scr, y_scr, *, G, C, t0, U):
    R = G * C
    W = R_WIDTH

    @pl.when(pl.program_id(1) == 0)
    def _():
        sout_ref[...] = s0_ref[...]
        shr_out[...] = shr_in[...]
        shl_out[...] = shl_in[...]

    def token_shift(p_ref, car_ref, mu_ref, w):
        p = p_ref[...].reshape(R, w)
        tl = _tloc(G, C, w)
        carry = jnp.concatenate([jnp.broadcast_to(car_ref[g], (C, w)) for g in range(G)], axis=0)
        prev = jnp.where(tl == t0, carry, pltpu.roll(p, 1, 0))
        for g in range(G):
            car_ref[g] = p[g * C + C - 1:g * C + C, :]
        return p + (prev - p) * mu_ref[...]

    ps = token_shift(prkv_ref, shr_out, mur_ref, 3 * W)
    lo = token_shift(plow_ref, shl_out, mul_ref, R_LOW)
    r = ps[:, 0:W]
    k = ps[:, W:2 * W]
    v = ps[:, 2 * W:3 * W]
    lw = -RWKV_DECAY_SCALE * jax.nn.sigmoid(w0_ref[...] + _dot(jnp.tanh(lo).astype(BF16), wup_ref[...]))
    a = jax.nn.sigmoid(a0_ref[...] + _dot(lo.astype(BF16), aup_ref[...]))
    gate = _dot(jax.nn.sigmoid(lo).astype(BF16), gup_ref[...])
    e = e_ref[...]
    kk = k * kk_ref[...]
    kk = kk / jnp.maximum(jnp.sqrt(_segsum(kk * kk, e)), 1e-12)
    k2 = k * (1.0 + (a - 1.0) * ka_ref[...])
    ka = kk * a
    tl = _tloc(G, C, W)
    if t0 > 0:
        valid = tl >= t0
        lw = jnp.where(valid, lw, 0.0)
        kk = jnp.where(valid, kk, 0.0)
        ka = jnp.where(valid, ka, 0.0)
        k2 = jnp.where(valid, k2, 0.0)
        v = jnp.where(valid, v, 0.0)
    cum = _cumsum_rows(lw, tl, C)
    gam = jnp.exp(cum)
    inv = jnp.exp(-cum)
    at_scr[...] = kk * jnp.exp(cum - lw)
    rt_scr[...] = r * gam
    kt_scr[...] = k2 * inv
    bt_scr[...] = ka * inv
    v_scr[...] = v
    gam_scr[...] = gam

    row = lax.broadcasted_iota(jnp.int32, (C, C), 0)
    col = lax.broadcasted_iota(jnp.int32, (C, C), 1)
    low_s = row > col
    low_i = row >= col

    def seq_body(it, carry):
        gs = [it * U + i for i in range(U)]
        lanes = [(i, h) for i in range(U) for h in range(R_HEADS)]
        data = []
        for g in gs:
            off = pl.multiple_of(g * C, 8)
            rows = pl.ds(off, C)
            data.append((at_scr[rows, :], rt_scr[rows, :], kt_scr[rows, :], bt_scr[rows, :], v_scr[rows, :],
                         gam_scr[pl.ds(off + C - 8, 8), :][7:8, :]))
        s_in = [sout_ref[gs[i], h] for i, h in lanes]

        def head(x, h):
            return x[:, R_HEAD_DIM * h:R_HEAD_DIM * (h + 1)]

        ar = [jnp.concatenate([head(data[i][0], h), head(data[i][1], h)], axis=0).astype(BF16) for i, h in lanes]
        ktb = [head(data[i][2], h).astype(BF16) for i, h in lanes]
        btb = [head(data[i][3], h).astype(BF16) for i, h in lanes]
        vb = [head(data[i][4], h).astype(BF16) for i, h in lanes]
        n = range(len(lanes))
        m_k = [_dg(ar[j], ktb[j], NT) for j in n]
        m_b = [_dg(ar[j], btb[j], NT) for j in n]
        n_pow = [jnp.where(low_s, m_b[j][0:C], 0.0) for j in n]
        acc = [-n_pow[j] for j in n]
        pre_rhs = [_dot(jnp.where(low_s, m_k[j][0:C], 0.0).astype(BF16), vb[j]) for j in n]
        pre_y = [_dot(jnp.where(low_i, m_k[j][C:2 * C], 0.0).astype(BF16), vb[j]) for j in n]
        pre_ds = [_dg(vb[j], ktb[j], TN) for j in n]
        trb = [jnp.where(low_i, m_b[j][C:2 * C], 0.0).astype(BF16) for j in n]
        span = 2
        nb = [n_pow[j].astype(BF16) for j in n]
        n_pow = [_dot(nb[j], nb[j]) for j in n]
        while span < C:
            nb = [n_pow[j].astype(BF16) for j in n]
            span *= 2
            prod = [_dot(acc[j].astype(BF16), nb[j]) for j in n]
            if span < C:
                nxt = [_dot(nb[j], nb[j]) for j in n]
            acc = [acc[j] + n_pow[j] + prod[j] for j in n]
            if span < C:
                n_pow = nxt
        accb = [acc[j].astype(BF16) for j in n]
        m_s = [_dg(ar[j], s_in[j].astype(BF16), NT) for j in n]
        rhs = [m_s[j][0:C] + pre_rhs[j] for j in n]
        u = [rhs[j] + _dot(accb[j], rhs[j].astype(BF16)) for j in n]
        ub = [u[j].astype(BF16) for j in n]
        yh = [m_s[j][C:2 * C] + pre_y[j] - _dot(trb[j], ub[j]) for j in n]
        ds = [pre_ds[j] - _dg(ub[j], btb[j], TN) for j in n]
        for j, (i, h) in enumerate(lanes):
            sout_ref[gs[i], h] = (s_in[j] + ds[j]) * head(data[i][5], h)
        for i, g in enumerate(gs):
            rows = pl.ds(pl.multiple_of(g * C, 8), C)
            y_scr[rows, :] = jnp.concatenate(yh[i * R_HEADS:(i + 1) * R_HEADS], axis=1)
        return carry

    lax.fori_loop(0, G // U, seq_body, 0)

    y = y_scr[...]
    inv_n = 1.0 / R_HEAD_DIM
    d = y - _segsum(y, e) * inv_n
    var = _segsum(d * d, e) * inv_n
    y = d * lax.rsqrt(var + RWKV_GN_EPS) * ln_ref[...]
    y = y + _segsum(r * k2 * rk_ref[...], e) * v
    y_ref[...] = (y * gate).reshape(G, C, W)


def _rwkv(p3, st, l_in, prm, l, cfg):
    B, T, _ = p3.shape
    G, C = cfg.g_mix, cfg.c_rwkv
    W = R_WIDTH
    shr_in, shl_in, s0 = st

    def pspec(w, off):
        return pl.BlockSpec((G, C, w), lambda i, c: (i, c, off // w))

    def lay(*shape):
        return pl.BlockSpec((None,) + shape, lambda i, c: (l,) + (0,) * len(shape))

    kern = functools.partial(_rwkv_kernel, G=G, C=C, t0=cfg.t0, U=cfg.u_rwkv)
    R = G * C
    return pl.pallas_call(
        kern,
        grid=(B // G, T // C),
        in_specs=[
            pspec(3 * W, P_R), pspec(R_LOW, P_RL),
            pl.BlockSpec((None, G, 1, 3 * W), lambda i, c: (l_in, i, 0, 0)),
            pl.BlockSpec((None, G, 1, R_LOW), lambda i, c: (l_in, i, 0, 0)),
            pl.BlockSpec((None, G, R_HEADS, R_HEAD_DIM, R_HEAD_DIM), lambda i, c: (l_in, i, 0, 0, 0)),
            lay(1, 3 * W), lay(1, R_LOW), lay(1, W), lay(R_LOW, W), lay(1, W), lay(R_LOW, W), lay(R_LOW, W),
            lay(1, W), lay(1, W), lay(1, W), lay(1, W),
            pl.BlockSpec((W, W), lambda i, c: (0, 0)),
        ],
        out_specs=[
            pl.BlockSpec((G, C, W), lambda i, c: (i, c, 0)),
            pl.BlockSpec((G, R_HEADS, R_HEAD_DIM, R_HEAD_DIM), lambda i, c: (i, 0, 0, 0)),
            pl.BlockSpec((G, 1, 3 * W), lambda i, c: (i, 0, 0)),
            pl.BlockSpec((G, 1, R_LOW), lambda i, c: (i, 0, 0)),
        ],
        out_shape=[
            jax.ShapeDtypeStruct((B, T, W), F32),
            jax.ShapeDtypeStruct((B, R_HEADS, R_HEAD_DIM, R_HEAD_DIM), F32),
            jax.ShapeDtypeStruct((B, 1, 3 * W), F32),
            jax.ShapeDtypeStruct((B, 1, R_LOW), F32),
        ],
        scratch_shapes=[pltpu.VMEM((R, W), F32)] * 7,
        compiler_params=_params(("arbitrary", "arbitrary")),
        name="rwkv7",
    )(p3, p3, shr_in, shl_in, s0, prm["mu_rkv"], prm["mu_low"], prm["rwkv_w0"], prm["rwkv_w_up"], prm["rwkv_a0"],
      prm["rwkv_a_up"], prm["rwkv_g_up"], prm["rwkv_k_k"], prm["rwkv_k_a"], prm["rwkv_r_k"], prm["rwkv_ln"],
      prm["e_head"])


def _s5_kernel(u_ref, h0r_ref, h0i_ref, bbhr_ref, bblr_ref, bbhi_ref, bbli_ref, ccr_ref, cci_ref, ap_ref, pw_ref,
               d_ref, wglu_ref, bglu_ref,
               y_ref, hr_out, hi_out, hre_scr, him_scr, *, G, C, t0):
    R = G * C
    NB = S5_FLAT // S5_CHUNKS
    UB = S5_WIDTH // S5_CHUNKS

    @pl.when(pl.program_id(1) == 0)
    def _():
        hr_out[...] = h0r_ref[...]
        hi_out[...] = h0i_ref[...]

    u = u_ref[...].reshape(R, S5_WIDTH)
    uh = u.astype(BF16)
    ul = (u - uh.astype(F32)).astype(BF16)
    valid = _tloc(G, C, NB) >= t0
    for j in range(S5_CHUNKS):
        us = slice(UB * j, UB * (j + 1))
        br = _dot(uh[:, us], bbhr_ref[j]) + _dot(ul[:, us], bbhr_ref[j]) + _dot(uh[:, us], bblr_ref[j])
        bi = _dot(uh[:, us], bbhi_ref[j]) + _dot(ul[:, us], bbhi_ref[j]) + _dot(uh[:, us], bbli_ref[j])
        if t0 > 0:
            br = jnp.where(valid, br, 0.0)
            bi = jnp.where(valid, bi, 0.0)
        hre_scr[:, NB * j:NB * (j + 1)] = br
        him_scr[:, NB * j:NB * (j + 1)] = bi

    ap = ap_ref[...]
    steps = ((1, ap[0:1, :], ap[3:4, :]), (2, ap[1:2, :], ap[4:5, :]), (4, ap[2:3, :], ap[5:6, :]))
    pwr = pw_ref[0:S5_SCAN, :]
    pwi = pw_ref[S5_SCAN:2 * S5_SCAN, :]
    rid = lax.broadcasted_iota(jnp.int32, (S5_SCAN, S5_FLAT), 0)

    def seq_body(g, carry):
        def tile_body(i, hc):
            hcr, hci = hc
            rows = pl.ds(pl.multiple_of(g * C + i * S5_SCAN, S5_SCAN), S5_SCAN)
            xr = hre_scr[rows, :]
            xi = him_scr[rows, :]
            for sh, ar, ai in steps:
                sr = jnp.where(rid >= sh, pltpu.roll(xr, sh, 0), 0.0)
                si = jnp.where(rid >= sh, pltpu.roll(xi, sh, 0), 0.0)
                xr, xi = xr + ar * sr - ai * si, xi + ar * si + ai * sr
            xr, xi = xr + pwr * hcr - pwi * hci, xi + pwr * hci + pwi * hcr
            hre_scr[rows, :] = xr
            him_scr[rows, :] = xi
            return xr[S5_SCAN - 1:S5_SCAN, :], xi[S5_SCAN - 1:S5_SCAN, :]

        hcr, hci = lax.fori_loop(0, C // S5_SCAN, tile_body, (hr_out[g], hi_out[g]))
        hr_out[g] = hcr
        hi_out[g] = hci
        return carry

    lax.fori_loop(0, G, seq_body, 0)

    ys = []
    for j in range(S5_CHUNKS):
        hs = slice(NB * j, NB * (j + 1))
        ys.append(_dot(hre_scr[:, hs].astype(BF16), ccr_ref[j]) - _dot(him_scr[:, hs].astype(BF16), cci_ref[j]))
    y = jnp.concatenate(ys, axis=1) + d_ref[...] * u
    y = _gelu(y)
    y = y * jax.nn.sigmoid(_dot(y.astype(BF16), wglu_ref[...]) + bglu_ref[...])
    y_ref[...] = y.reshape(G, C, S5_WIDTH)


def _s5(p3, st, l_in, prm, l, cfg):
    B, T, _ = p3.shape
    G, C = cfg.g_s5, cfg.c_s5
    h0r, h0i = st
    NB = S5_FLAT // S5_CHUNKS
    UB = S5_WIDTH // S5_CHUNKS

    def lay(*shape):
        return pl.BlockSpec((None,) + shape, lambda i, c: (l,) + (0,) * len(shape))

    st_spec = pl.BlockSpec((None, G, 1, S5_FLAT), lambda i, c: (l_in, i, 0, 0))
    out_st = pl.BlockSpec((G, 1, S5_FLAT), lambda i, c: (i, 0, 0))
    kern = functools.partial(_s5_kernel, G=G, C=C, t0=cfg.t0)
    pw = prm["s5_pw"][cfg.t0]
    return pl.pallas_call(
        kern,
        grid=(B // G, T // C),
        in_specs=[
            pl.BlockSpec((G, C, S5_WIDTH), lambda i, c: (i, c, P_S5 // S5_WIDTH)),
            st_spec, st_spec,
            lay(S5_CHUNKS, UB, NB), lay(S5_CHUNKS, UB, NB), lay(S5_CHUNKS, UB, NB), lay(S5_CHUNKS, UB, NB),
            lay(S5_CHUNKS, NB, UB), lay(S5_CHUNKS, NB, UB),
            lay(8, S5_FLAT), lay(2 * S5_SCAN, S5_FLAT),
            lay(1, S5_WIDTH), lay(S5_WIDTH, S5_WIDTH), lay(1, S5_WIDTH),
        ],
        out_specs=[pl.BlockSpec((G, C, S5_WIDTH), lambda i, c: (i, c, 0)), out_st, out_st],
        out_shape=[
            jax.ShapeDtypeStruct((B, T, S5_WIDTH), F32),
            jax.ShapeDtypeStruct((B, 1, S5_FLAT), F32),
            jax.ShapeDtypeStruct((B, 1, S5_FLAT), F32),
        ],
        scratch_shapes=[pltpu.VMEM((G * C, S5_FLAT), F32)] * 2,
        compiler_params=_params(("arbitrary", "arbitrary")),
        name="s5",
    )(p3, h0r, h0i, prm["s5_bbh_re"], prm["s5_bbl_re"], prm["s5_bbh_im"], prm["s5_bbl_im"], prm["s5_cc_re"],
      prm["s5_cc_im"], prm["s5_apow"], pw, prm["s5_D"], prm["s5_w_glu"], prm["s5_b_glu"])


def _gla_core(q, kx, gdec, v, s0_ref, sout_ref, st_scr, q_scr, k_scr, b_scr, v_scr, o_scr, *, G, CB, C, t0, K, KS):
    assert CB % C == 0 and C & (C - 1) == 0 and (t0 == 0 or CB == C)
    H = G_HEADS
    V = G_VAL_DIM
    HK = H * K
    c = pl.program_id(1)

    @pl.when(c == 0)
    def _():
        for g in range(G):
            for h in range(H):
                s = s0_ref[g, h].T
                if KS < K:
                    s = jnp.concatenate([s, jnp.zeros((V, K - KS), F32)], axis=1)
                st_scr[g, h] = s

    tl = lax.broadcasted_iota(jnp.int32, (G * CB, HK), 0) & (C - 1)
    if t0 > 0:
        valid = tl >= t0
        gdec = jnp.where(valid, gdec, 0.0)
        kx = jnp.where(valid, kx, 0.0)
    q_scr[...] = q
    k_scr[...] = kx
    b_scr[...] = _cumsum_rows(gdec, tl, C)
    v_scr[...] = v
    ti = lax.broadcasted_iota(jnp.int32, (C, HK), 0)

    def seq_body(it, carry):
        j = it // G
        g = it - j * G
        rows = pl.ds(pl.multiple_of(g * CB + j * C, 8), C)
        qg = q_scr[rows, :]
        kg = k_scr[rows, :]
        bg = b_scr[rows, :]
        vg = v_scr[rows, :]
        b_last = bg[C - 1:C, :]
        qe = (qg * jnp.exp(bg)).astype(BF16)
        kd = (kg * jnp.exp(b_last - bg)).astype(BF16)
        dec_last = jnp.exp(b_last)
        intra = [jnp.zeros((C, V), F32) for _ in range(H)]
        for j in range(C):
            causal = ti >= j
            e = jnp.where(causal, qg * kg[j:j + 1, :] * jnp.exp(jnp.where(causal, bg - bg[j:j + 1, :], 0.0)), 0.0)
            for h in range(H):
                a = jnp.sum(e[:, K * h:K * (h + 1)], axis=-1, keepdims=True)
                intra[h] = intra[h] + a * vg[j:j + 1, V * h:V * (h + 1)]
        s_in = [st_scr[g, h] for h in range(H)]
        outs = []
        s_new = []
        for h in range(H):
            ks = slice(K * h, K * (h + 1))
            vs = slice(V * h, V * (h + 1))
            outs.append(_dg(qe[:, ks], s_in[h].astype(BF16), NT) + intra[h])
            s_new.append(s_in[h] * dec_last[:, ks] + _dg(vg[:, vs].astype(BF16), kd[:, ks], TN))
        for h in range(H):
            st_scr[g, h] = s_new[h]
        o_scr[rows, :] = jnp.concatenate(outs, axis=1)
        return carry

    lax.fori_loop(0, G * (CB // C), seq_body, 0)

    @pl.when(c == pl.num_programs(1) - 1)
    def _():
        for g in range(G):
            for h in range(H):
                sout_ref[g, h] = st_scr[g, h][:, 0:KS].T


def _head_norm_gate(o, norm, gate_raw):
    V = G_VAL_DIM
    ys = []
    for h in range(G_HEADS):
        oh = o[:, V * h:V * (h + 1)]
        ys.append(oh * lax.rsqrt(jnp.mean(oh * oh, axis=-1, keepdims=True) + EPS))
    y = jnp.concatenate(ys, axis=1) * norm
    return y * (gate_raw * jax.nn.sigmoid(gate_raw))


def _hgrn_kernel(q_ref, f_ref, i_ref, g_ref, lb_ref, norm_ref, s0_ref, y_ref, sout_ref,
                 st_scr, q_scr, k_scr, b_scr, v_scr, o_scr, *, G, C, chunk, t0, layer):
    R = G * C
    W = H_WIDTH
    lbr = lb_ref[...]
    ex = jnp.exp(lbr - jnp.max(lbr, axis=0, keepdims=True))
    sm = ex / jnp.sum(ex, axis=0, keepdims=True)
    lb = jnp.zeros((1, W), F32)
    for i in range(1, layer + 1):
        lb = lb + sm[i:i + 1, :]
    qr = q_ref[...].reshape(R, W)
    kx = jnp.minimum((1.0 - lb) * jax.nn.sigmoid(-f_ref[...].reshape(R, W)), HGRN_MAX_INPUT)
    _gla_core(qr * jax.nn.sigmoid(qr), kx, jnp.log1p(-kx), i_ref[...].reshape(R, W), s0_ref, sout_ref,
              st_scr, q_scr, k_scr, b_scr, v_scr, o_scr, G=G, CB=C, C=chunk, t0=t0, K=H_HEAD_DIM, KS=H_HEAD_DIM)
    y_ref[...] = _head_norm_gate(o_scr[...], norm_ref[...], g_ref[...].reshape(R, W)).reshape(G, C, W)


def _gla_kernel(v_ref, gate_ref, q_ref, k_ref, gl_ref, gkup_ref, gkb_ref, norm_ref, s0_ref, y_ref, sout_ref,
                st_scr, q_scr, k_scr, b_scr, v_scr, o_scr, *, G, C, chunk, t0):
    R = G * C
    z = _dot(gl_ref[...].reshape(R, GL_PAD).astype(BF16), gkup_ref[...]) + gkb_ref[...]
    gdec = -(jnp.maximum(-z, 0.0) + jnp.log1p(jnp.exp(-jnp.abs(z)))) / GLA_GATE_NORM
    _gla_core(q_ref[...].reshape(R, G_QK_PAD) * (G_KEY_DIM ** -0.5), k_ref[...].reshape(R, G_QK_PAD), gdec,
              v_ref[...].reshape(R, G_WIDTH), s0_ref, sout_ref, st_scr, q_scr, k_scr, b_scr, v_scr, o_scr,
              G=G, CB=C, C=chunk, t0=t0, K=G_KEY_PAD, KS=G_KEY_DIM)
    y_ref[...] = _head_norm_gate(o_scr[...], norm_ref[...], gate_ref[...].reshape(R, G_WIDTH)).reshape(G, C, G_WIDTH)


def _gla_like(p3, s0, l_in, prm, l, cfg, hgrn):
    B, T, _ = p3.shape
    G, C = cfg.g_mix, cfg.cb_gla
    H, V = G_HEADS, G_VAL_DIM
    K = H_HEAD_DIM if hgrn else G_KEY_PAD
    KS = H_HEAD_DIM if hgrn else G_KEY_DIM
    HK = H * K
    R = G * C

    def pspec(w, off):
        return pl.BlockSpec((G, C, w), lambda i, c: (i, c, off // w))

    def lay(*shape):
        return pl.BlockSpec((None,) + shape, lambda i, c: (l,) + (0,) * len(shape))

    st_in = pl.BlockSpec((None, G, H, KS, V), lambda i, c: (l_in, i, 0, 0, 0))
    if hgrn:
        kern = functools.partial(_hgrn_kernel, G=G, C=C, chunk=cfg.c_gla, t0=cfg.t0, layer=l)
        in_specs = [pspec(HK, P_H), pspec(HK, P_H + HK), pspec(HK, P_H + 2 * HK), pspec(HK, P_H + 3 * HK),
                    pl.BlockSpec((DEPTH, HK), lambda i, c: (0, 0)), lay(1, HK), st_in]
        args = (p3, p3, p3, p3, prm["hgrn_lb_raw"], prm["hgrn_norm"], s0)
        name = "hgrn2"
    else:
        kern = functools.partial(_gla_kernel, G=G, C=C, chunk=cfg.c_gla, t0=cfg.t0)
        in_specs = [pspec(G_WIDTH, P_GV), pspec(G_WIDTH, P_GG), pspec(HK, P_GQ), pspec(HK, P_GK),
                    pspec(GL_PAD, P_GL), lay(GL_PAD, HK), lay(1, HK), lay(1, G_WIDTH), st_in]
        args = (p3, p3, p3, p3, p3, prm["gla_gk_up"], prm["gla_gk_b"], prm["gla_norm"], s0)
        name = "gla"
    return pl.pallas_call(
        kern,
        grid=(B // G, T // C),
        in_specs=in_specs,
        out_specs=[pl.BlockSpec((G, C, H * V), lambda i, c: (i, c, 0)),
                   pl.BlockSpec((G, H, KS, V), lambda i, c: (i, 0, 0, 0))],
        out_shape=[jax.ShapeDtypeStruct((B, T, H * V), F32), jax.ShapeDtypeStruct((B, H, KS, V), F32)],
        scratch_shapes=[pltpu.VMEM((G, H, V, K), F32), pltpu.VMEM((R, HK), F32), pltpu.VMEM((R, HK), F32),
                        pltpu.VMEM((R, HK), F32), pltpu.VMEM((R, H * V), F32), pltpu.VMEM((R, H * V), F32)],
        compiler_params=_params(("arbitrary", "arbitrary")),
        name=name,
    )(*args)


def _s5_tables(A_re, A_im, log_dt, B_re, B_im, C_re, C_im, t0s):
    L = A_re.shape[0]
    A_re = A_re.astype(F32)
    A_im = A_im.astype(F32)
    dt = jnp.exp(log_dt.astype(F32))[..., None]
    mag = jnp.exp(A_re * dt)
    ab_re = mag * jnp.cos(A_im * dt)
    ab_im = mag * jnp.sin(A_im * dt)
    den = A_re * A_re + A_im * A_im
    n_re = ab_re - 1.0
    co_re = (n_re * A_re + ab_im * A_im) / den
    co_im = (ab_im * A_re - n_re * A_im) / den
    B_re = B_re.astype(F32)
    B_im = B_im.astype(F32)
    bb_re = co_re[..., None] * B_re - co_im[..., None] * B_im
    bb_im = co_re[..., None] * B_im + co_im[..., None] * B_re
    gpc = S5_GROUPS // S5_CHUNKS
    eye = jnp.eye(gpc, dtype=F32)

    def block_in(bb):
        t = bb.reshape(L, S5_CHUNKS, gpc, S5_STATE, S5_GROUP_CH).transpose(0, 1, 2, 4, 3)
        t = t[:, :, :, :, None, :] * eye[None, None, :, None, :, None]
        return t.reshape(L, S5_CHUNKS, gpc * S5_GROUP_CH, gpc * S5_STATE)

    def block_out(cc):
        t = cc.astype(F32).reshape(L, S5_CHUNKS, gpc, S5_GROUP_CH, S5_STATE).transpose(0, 1, 2, 4, 3)
        t = t[:, :, :, :, None, :] * eye[None, None, :, None, :, None]
        return t.reshape(L, S5_CHUNKS, gpc * S5_STATE, gpc * S5_GROUP_CH)

    def split(x):
        hi = x.astype(BF16)
        return hi, (x - hi.astype(F32)).astype(BF16)

    bbh_re, bbl_re = split(block_in(bb_re))
    bbh_im, bbl_im = split(block_in(bb_im))

    def power(m):
        mg = jnp.exp(m * dt * A_re)
        return (mg * jnp.cos(m * dt * A_im)).reshape(L, S5_FLAT), (mg * jnp.sin(m * dt * A_im)).reshape(L, S5_FLAT)

    p1, p2, p4 = power(1.0), power(2.0), power(4.0)
    zero = jnp.zeros((L, S5_FLAT), F32)
    apow = jnp.stack([p1[0], p2[0], p4[0], p1[1], p2[1], p4[1], zero, zero], axis=1)
    pws = {}
    for t0 in t0s:
        first = t0 % S5_SCAN
        res, ims = [], []
        for r in range(S5_SCAN):
            if r < first:
                res.append(zero)
                ims.append(zero)
            else:
                pr, pi = power(float(r - first + 1))
                res.append(pr)
                ims.append(pi)
        pws[t0] = jnp.stack(res + ims, axis=1)
    return dict(s5_bbh_re=bbh_re, s5_bbl_re=bbl_re, s5_bbh_im=bbh_im, s5_bbl_im=bbl_im,
                s5_cc_re=block_out(C_re).astype(BF16), s5_cc_im=block_out(C_im).astype(BF16),
                s5_apow=apow, s5_pw=pws)


def _prep_params(raw, t0s):
    L = raw["w_in"].shape[0]
    W = R_WIDTH
    w_in = raw["w_in"]
    o_s5 = R_COLS
    o_h = o_s5 + S5_WIDTH
    o_g = o_h + 4 * H_WIDTH
    o_gv = o_g + 2 * G_QK
    o_gl = o_gv + G_WIDTH
    o_gg = o_gl + G_GATE_RANK
    pad = jnp.zeros(w_in.shape[:2] + (GL_PAD - G_GATE_RANK,), w_in.dtype)

    def key_pad(w):
        w = w.reshape(w.shape[:-1] + (G_HEADS, G_KEY_DIM))
        w = jnp.pad(w, ((0, 0),) * (w.ndim - 1) + ((0, G_KEY_PAD - G_KEY_DIM),))
        return w.reshape(w.shape[:-2] + (G_QK_PAD,))

    w_in_p = jnp.concatenate([
        w_in[..., 0:3 * W], w_in[..., o_s5:o_h], w_in[..., o_h:o_g], w_in[..., o_gv:o_gl],
        w_in[..., o_gg:o_gg + G_WIDTH], key_pad(w_in[..., o_g:o_g + G_QK]), key_pad(w_in[..., o_g + G_QK:o_gv]),
        w_in[..., 3 * W:R_COLS], w_in[..., o_gl:o_gg], pad], axis=-1).astype(BF16)
    assert w_in_p.shape[-1] == P_COLS

    def row3(x):
        return x.reshape(L, 1, -1).astype(F32)

    def low_pad(w, off):
        return jnp.pad(w, ((0, 0), (off, R_LOW - off - w.shape[1]), (0, 0))).astype(BF16)

    ff = FF_PAD - D_FF
    w_up = raw["ffn_w_up"]
    head = jnp.arange(W) // R_HEAD_DIM
    prm = dict(
        norm_mix=row3(raw["norm_mix"]), norm_ffn=row3(raw["norm_ffn"]),
        w_in=w_in_p, w_out=raw["w_out"].astype(BF16),
        mu_rkv=row3(raw["rwkv_mu"][:, 0:3 * W]), mu_low=row3(raw["rwkv_mu"][:, 3 * W:]),
        rwkv_w0=row3(raw["rwkv_w0"]), rwkv_a0=row3(raw["rwkv_a0"]),
        rwkv_w_up=low_pad(raw["rwkv_w_up"], 0), rwkv_a_up=low_pad(raw["rwkv_a_up"], R_DECAY_RANK),
        rwkv_g_up=low_pad(raw["rwkv_g_up"], R_DECAY_RANK + R_ICL_RANK),
        rwkv_k_k=row3(raw["rwkv_k_k"]), rwkv_k_a=row3(raw["rwkv_k_a"]), rwkv_r_k=row3(raw["rwkv_r_k"]),
        rwkv_ln=row3(raw["rwkv_ln"]),
        e_head=(head[:, None] == head[None, :]).astype(BF16),
        s5_D=row3(raw["s5_D"]), s5_w_glu=raw["s5_w_glu"].astype(BF16), s5_b_glu=row3(raw["s5_b_glu"]),
        hgrn_lb_raw=raw["hgrn_lower_bounds"].astype(F32), hgrn_norm=row3(raw["hgrn_norm"]),
        gla_gk_up=jnp.pad(key_pad(raw["gla_gk_up"]), ((0, 0), (0, GL_PAD - G_GATE_RANK), (0, 0))).astype(BF16),
        gla_gk_b=row3(key_pad(raw["gla_gk_b"])), gla_norm=row3(raw["gla_norm"]),
        ffn_wu=jnp.pad(w_up[..., :D_FF], ((0, 0), (0, 0), (0, ff))).astype(BF16),
        ffn_wg=jnp.pad(w_up[..., D_FF:], ((0, 0), (0, 0), (0, ff))).astype(BF16),
        ffn_conv_w=jnp.pad(raw["ffn_conv_w"], ((0, 0), (0, 0), (0, ff))).astype(F32),
        ffn_conv_b=jnp.pad(raw["ffn_conv_b"], ((0, 0), (0, ff))).reshape(L, 1, FF_PAD).astype(F32),
        ffn_w_down=jnp.pad(raw["ffn_w_down"], ((0, 0), (0, ff), (0, 0))).astype(BF16),
    )
    prm.update(_s5_tables(raw["s5_A_re"], raw["s5_A_im"], raw["s5_log_dt"], raw["s5_B_re"], raw["s5_B_im"],
                          raw["s5_C_re"], raw["s5_C_im"], t0s))
    return prm


def _layer(x, st, l_in, prm, l, cfg):
    shr, shl, s_rw, h_re, h_im, s_hg, s_gl, buf = st
    p = _norm_matmul(x, prm["norm_mix"], prm["w_in"], l, cfg.tm, P_COLS // 4)
    p3 = p.reshape(cfg.B, cfg.T, P_COLS)
    y_r, s_rw_n, shr_n, shl_n = _rwkv(p3, (shr, shl, s_rw), l_in, prm, l, cfg)
    y_s, h_re_n, h_im_n = _s5(p3, (h_re, h_im), l_in, prm, l, cfg)
    y_h, s_hg_n = _gla_like(p3, s_hg, l_in, prm, l, cfg, True)
    y_g, s_gl_n = _gla_like(p3, s_gl, l_in, prm, l, cfg, False)
    M = cfg.B * cfg.T
    ys = [y.reshape(M, GROUP_WIDTH) for y in (y_r, y_s, y_h, y_g)]
    x = _out_proj(x, ys, prm["w_out"], l, cfg.tm, 1024)
    act, ust = _ffn_up(x, prm["norm_ffn"], prm["ffn_wu"], prm["ffn_wg"], prm["ffn_conv_w"], prm["ffn_conv_b"], buf,
                       l, l_in, cfg, 512)
    x = _ffn_down(x, act, prm["ffn_w_down"], l, cfg.tm, 512)
    return x, (shr_n, shl_n, s_rw_n, h_re_n, h_im_n, s_hg_n, s_gl_n, ust)


def _state_outputs(new, cfg):
    shr, shl, s_rw, h_re, h_im, s_hg, s_gl, ust = new
    B = cfg.B
    shift = jnp.concatenate([shr, shl], axis=-1).reshape(B, R_COLS)
    if cfg.inject:
        conv = ust.reshape(B, cfg.T, FF_PAD)[:, cfg.T - (CONV_W - 1):, :D_FF]
    else:
        tps = cfg.T // cfg.tm
        conv = ust[tps - 1::tps, :, :D_FF]
    return (s_rw, shift, h_re.reshape(B, S5_GROUPS, S5_STATE), h_im.reshape(B, S5_GROUPS, S5_STATE), s_hg, s_gl,
            conv)


PROMPT_PAD = 0
SAMPLE_PAD = 4


def kernel(x_prompt, x_sample, state_rwkv, state_rwkv_shift, state_s5_re, state_s5_im, state_hgrn, state_gla, state_ffn_conv, meta_tokens, norm_mix, w_in, w_out, rwkv_mu, rwkv_w0, rwkv_w_up, rwkv_a0, rwkv_a_up, rwkv_g_up, rwkv_k_k, rwkv_k_a, rwkv_r_k, rwkv_ln, s5_A_re, s5_A_im, s5_log_dt, s5_B_re, s5_B_im, s5_C_re, s5_C_im, s5_D, s5_w_glu, s5_b_glu, hgrn_lower_bounds, hgrn_norm, gla_gk_up, gla_gk_b, gla_norm, norm_ffn, ffn_w_up, ffn_conv_w, ffn_conv_b, ffn_w_down, norm_final):
    L = DEPTH
    Bp, Sp, D = x_prompt.shape
    Bs, Ss, _ = x_sample.shape
    Tp = N_META + Sp
    Ts = SAMPLE_PAD + Ss
    raw = dict(norm_mix=norm_mix, w_in=w_in, w_out=w_out, rwkv_mu=rwkv_mu, rwkv_w0=rwkv_w0, rwkv_w_up=rwkv_w_up,
               rwkv_a0=rwkv_a0, rwkv_a_up=rwkv_a_up, rwkv_g_up=rwkv_g_up, rwkv_k_k=rwkv_k_k, rwkv_k_a=rwkv_k_a,
               rwkv_r_k=rwkv_r_k, rwkv_ln=rwkv_ln, s5_A_re=s5_A_re, s5_A_im=s5_A_im, s5_log_dt=s5_log_dt,
               s5_B_re=s5_B_re, s5_B_im=s5_B_im, s5_C_re=s5_C_re, s5_C_im=s5_C_im, s5_D=s5_D, s5_w_glu=s5_w_glu,
               s5_b_glu=s5_b_glu, hgrn_lower_bounds=hgrn_lower_bounds, hgrn_norm=hgrn_norm, gla_gk_up=gla_gk_up,
               gla_gk_b=gla_gk_b, gla_norm=gla_norm, norm_ffn=norm_ffn, ffn_w_up=ffn_w_up, ffn_conv_w=ffn_conv_w,
               ffn_conv_b=ffn_conv_b, ffn_w_down=ffn_w_down)
    prm = _prep_params(raw, (PROMPT_PAD, SAMPLE_PAD))

    cfg_p = _Cfg(B=Bp, T=Tp, t0=PROMPT_PAD, tm=Tp // 3, g_mix=Bp, u_rwkv=2, c_rwkv=48, cb_gla=48, c_gla=16, g_s5=1, c_s5=Tp // 3,
                 inject=False)
    cfg_s = _Cfg(B=Bs, T=Ts, t0=SAMPLE_PAD, tm=Bs * Ts, g_mix=8, u_rwkv=4, c_rwkv=Ts, cb_gla=Ts, c_gla=Ts, g_s5=32, c_s5=Ts,
                 inject=True)

    dt = x_prompt.dtype
    meta = jnp.broadcast_to(meta_tokens.astype(dt)[None], (Bp, N_META, D))
    xp = jnp.concatenate([meta, x_prompt], axis=1).reshape(Bp * Tp, D)
    xs = jnp.pad(x_sample, ((0, 0), (SAMPLE_PAD, 0), (0, 0))).reshape(Bs * Ts, D)

    W = R_WIDTH
    st_p = (jnp.zeros((1, Bp, 1, 3 * W), F32), jnp.zeros((1, Bp, 1, R_LOW), F32),
            jnp.zeros((1, Bp, R_HEADS, R_HEAD_DIM, R_HEAD_DIM), F32),
            jnp.zeros((1, Bp, 1, S5_FLAT), F32), jnp.zeros((1, Bp, 1, S5_FLAT), F32),
            jnp.zeros((1, Bp, H_HEADS, H_HEAD_DIM, H_HEAD_DIM), F32),
            jnp.zeros((1, Bp, G_HEADS, G_KEY_DIM, G_VAL_DIM), F32),
            jnp.zeros((1, Bp, CONV_W - 1, FF_PAD), F32))
    buf_s = jnp.pad(state_ffn_conv, ((0, 0), (0, 0), (SAMPLE_PAD - (CONV_W - 1), Ss), (0, FF_PAD - D_FF)))
    st_s = (state_rwkv_shift[:, :, None, 0:3 * W], state_rwkv_shift[:, :, None, 3 * W:], state_rwkv,
            state_s5_re.reshape(L, Bs, 1, S5_FLAT), state_s5_im.reshape(L, Bs, 1, S5_FLAT),
            state_hgrn, state_gla, buf_s.reshape(L, Bs * Ts, FF_PAD))

    outs_p, outs_s = [], []
    for l in range(L):
        xp, new_p = _layer(xp, st_p, 0, prm, l, cfg_p)
        xs, new_s = _layer(xs, st_s, l, prm, l, cfg_s)
        outs_p.append(_state_outputs(new_p, cfg_p))
        outs_s.append(_state_outputs(new_s, cfg_s))

    g_fin = norm_final.reshape(1, D).astype(F32)
    y_prompt = _final_norm(xp, g_fin, cfg_p.tm).reshape(Bp, Tp, D)[:, N_META:]
    y_sample = _final_norm(xs, g_fin, cfg_s.tm).reshape(Bs, Ts, D)[:, SAMPLE_PAD:]
    sp = [jnp.stack([o[i] for o in outs_p]) for i in range(7)]
    ss = [jnp.stack([o[i] for o in outs_s]) for i in range(7)]
    return (y_prompt, y_sample, sp[0], ss[0], sp[1], ss[1], sp[2], ss[2], sp[3], ss[3],
            sp[4], ss[4], sp[5], ss[5], sp[6], ss[6])
```

```python
import functools
import math
from typing import NamedTuple

import jax
import jax.numpy as jnp
from jax import lax
from jax.experimental import pallas as pl
from jax.experimental.pallas import tpu as pltpu

F32 = jnp.float32
BF16 = jnp.bfloat16

D_MODEL = 2048
DEPTH = 4
N_META = 16
GROUP_WIDTH = D_MODEL // 4
EPS = 1e-6
R_HEAD_DIM = 64
R_HEADS = GROUP_WIDTH // R_HEAD_DIM
R_WIDTH = R_HEADS * R_HEAD_DIM
R_DECAY_RANK = 64
R_ICL_RANK = 64
R_GATE_RANK = 128
R_LOW = R_DECAY_RANK + R_ICL_RANK + R_GATE_RANK
R_COLS = 3 * R_WIDTH + R_LOW
RWKV_DECAY_SCALE = 0.606531
RWKV_GN_EPS = 64e-5
S5_GROUP_CH = 16
S5_GROUPS = GROUP_WIDTH // S5_GROUP_CH
S5_WIDTH = S5_GROUPS * S5_GROUP_CH
S5_STATE = 64
S5_FLAT = S5_GROUPS * S5_STATE
H_HEAD_DIM = 128
H_HEADS = GROUP_WIDTH // H_HEAD_DIM
H_WIDTH = H_HEADS * H_HEAD_DIM
HGRN_MAX_INPUT = 1.0 - 1e-4
G_VAL_DIM = 128
G_HEADS = GROUP_WIDTH // G_VAL_DIM
G_KEY_DIM = G_VAL_DIM // 2
G_WIDTH = G_HEADS * G_VAL_DIM
G_QK = G_HEADS * G_KEY_DIM
G_GATE_RANK = 16
GLA_GATE_NORM = 16.0
GLA_SAFE_DECAY = 40.0
D_FF = ((8 * D_MODEL // 3 + 127) // 128) * 128
CONV_W = 3

P_R = 0
P_S5 = 3 * R_WIDTH
P_H = P_S5 + S5_WIDTH
P_GV = P_H + 4 * H_WIDTH
P_GG = P_GV + G_WIDTH
G_KEY_PAD = G_VAL_DIM
G_QK_PAD = G_HEADS * G_KEY_PAD
P_GQ = P_GG + G_WIDTH
P_GK = P_GQ + G_QK_PAD
P_RL = P_GK + G_QK_PAD
P_GL = P_RL + R_LOW
GL_PAD = 256
P_COLS = P_GL + GL_PAD
FF_PAD = 5632
S5_CHUNKS = 4
S5_SCAN = 8
FFN_COL_BLOCK = 256

VMEM_LIMIT = 56 * 1024 * 1024

NT = (((1,), (1,)), ((), ()))
TN = (((0,), (0,)), ((), ()))


class _Cfg(NamedTuple):
    B: int
    T: int
    t0: int
    tm: int
    g_mix: int
    u_rwkv: int
    c_rwkv: int
    cb_gla: int
    c_gla: int
    s5_split: bool
    g_s5: int
    c_s5: int
    inject: bool


def _dot(a, b):
    return jnp.dot(a, b, preferred_element_type=F32)


def _dg(a, b, dims):
    return lax.dot_general(a, b, dims, preferred_element_type=F32)


def _gelu(x):
    c = math.sqrt(2.0 / math.pi)
    return x * (0.5 * (1.0 + jnp.tanh(c * (x + 0.044715 * (x * x * x)))))


def _tloc(G, C, W):
    t = lax.broadcasted_iota(jnp.int32, (C, W), 0)
    return t if G == 1 else jnp.concatenate([t] * G, axis=0)


def _cumsum_rows(x, tl, C):
    sh = 1
    while sh < C:
        x = x + jnp.where(tl >= sh, pltpu.roll(x, sh, 0), 0.0)
        sh *= 2
    return x


def _segsum(x, e):
    x1 = x.astype(BF16)
    r1 = x - x1.astype(F32)
    x2 = r1.astype(BF16)
    x3 = (r1 - x2.astype(F32)).astype(BF16)
    return _dot(x1, e) + _dot(x2, e) + _dot(x3, e)


def _params(sem):
    return pltpu.CompilerParams(dimension_semantics=sem, vmem_limit_bytes=VMEM_LIMIT)


def _norm_matmul_kernel(x_ref, g_ref, w_ref, o_ref, h_scr):
    @pl.when(pl.program_id(1) == 0)
    def _():
        x = x_ref[...]
        r = lax.rsqrt(jnp.mean(x * x, axis=-1, keepdims=True) + EPS)
        h_scr[...] = (x * r * g_ref[...]).astype(BF16)

    o_ref[...] = _dot(h_scr[...], w_ref[...])


def _norm_matmul(x, g_all, w_all, l, tm, tn):
    M, D = x.shape
    N = w_all.shape[-1]
    return pl.pallas_call(
        _norm_matmul_kernel,
        grid=(M // tm, N // tn),
        in_specs=[
            pl.BlockSpec((tm, D), lambda i, j: (i, 0)),
            pl.BlockSpec((None, 1, D), lambda i, j: (l, 0, 0)),
            pl.BlockSpec((None, D, tn), lambda i, j: (l, 0, j)),
        ],
        out_specs=pl.BlockSpec((tm, tn), lambda i, j: (i, j)),
        out_shape=jax.ShapeDtypeStruct((M, N), F32),
        scratch_shapes=[pltpu.VMEM((tm, D), BF16)],
        compiler_params=_params(("arbitrary", "arbitrary")),
        name="in_proj",
    )(x, g_all, w_all)


def _out_proj_kernel(x_ref, y0, y1, y2, y3, w0, w1, w2, w3, o_ref):
    acc = _dot(y0[...].astype(BF16), w0[...])
    acc = acc + _dot(y1[...].astype(BF16), w1[...])
    acc = acc + _dot(y2[...].astype(BF16), w2[...])
    acc = acc + _dot(y3[...].astype(BF16), w3[...])
    o_ref[...] = x_ref[...] + acc


def _out_proj(x, ys, w_all, l, tm, tn):
    M, D = x.shape
    W = GROUP_WIDTH
    yspec = pl.BlockSpec((tm, W), lambda i, j: (i, 0))
    wspecs = [pl.BlockSpec((None, W, tn), functools.partial(lambda i, j, q: (l, q, j), q=q)) for q in range(4)]
    return pl.pallas_call(
        _out_proj_kernel,
        grid=(M // tm, D // tn),
        in_specs=[pl.BlockSpec((tm, tn), lambda i, j: (i, j))] + [yspec] * 4 + wspecs,
        out_specs=pl.BlockSpec((tm, tn), lambda i, j: (i, j)),
        out_shape=jax.ShapeDtypeStruct((M, D), F32),
        compiler_params=_params(("arbitrary", "arbitrary")),
        name="out_proj",
    )(x, *ys, w_all, w_all, w_all, w_all)


def _ffn_up_kernel(x_ref, g_ref, wu_ref, wg_ref, cw_ref, cb_ref, buf_ref, act_ref, ust_ref, h_scr, *car,
                   tm, tn, tiles_per_seq, t0, seq_rows, inject):
    i = pl.program_id(0)
    j = pl.program_id(1)

    @pl.when(j == 0)
    def _():
        x = x_ref[...]
        r = lax.rsqrt(jnp.mean(x * x, axis=-1, keepdims=True) + EPS)
        h_scr[...] = (x * r * g_ref[...]).astype(BF16)

    if not inject:
        (car_scr,) = car

        @pl.when(i % tiles_per_seq == 0)
        def _():
            car_scr[j, 0:2, :] = buf_ref[...]

    nb = FFN_COL_BLOCK
    blocks = [slice(nb * q, nb * (q + 1)) for q in range(tn // nb)]
    h = h_scr[...]
    us = [_dot(h, wu_ref[:, cs]) for cs in blocks]
    gts = [_dot(h, wg_ref[:, cs]) for cs in blocks]
    rows = lax.broadcasted_iota(jnp.int32, (tm, nb), 0)
    for cs, u, gt in zip(blocks, us, gts):
        if inject:
            tl = rows & (seq_rows - 1)
            u = jnp.where((tl == t0 - 2) | (tl == t0 - 1), buf_ref[:, cs], u)
            p1 = pltpu.roll(u, 1, 0)
            p2 = pltpu.roll(u, 2, 0)
            ust_ref[:, cs] = u
        else:
            c0 = car_scr[j, 0:1, cs]
            c1 = car_scr[j, 1:2, cs]
            p1 = jnp.where(rows == 0, c1, pltpu.roll(u, 1, 0))
            p2 = jnp.where(rows == 0, c0, jnp.where(rows == 1, c1, pltpu.roll(u, 2, 0)))
            last = u[tm - 2:tm, :]
            car_scr[j, 0:2, cs] = last
            ust_ref[:, cs] = last
        c = cb_ref[:, cs] + cw_ref[0:1, cs] * p2
        c = c + cw_ref[1:2, cs] * p1
        c = c + cw_ref[2:3, cs] * u
        act_ref[:, cs] = (_gelu(c) * gt).astype(BF16)


def _ffn_up(x, g_all, wu_all, wg_all, cw_all, cb_all, buf, l, l_in, cfg, tn):
    M, D = x.shape
    tm = cfg.tm
    N = wu_all.shape[-1]
    tps = cfg.T // tm if not cfg.inject else 1
    kern = functools.partial(_ffn_up_kernel, tm=tm, tn=tn, tiles_per_seq=tps, t0=cfg.t0, seq_rows=cfg.T,
                             inject=cfg.inject)
    if cfg.inject:
        buf_spec = pl.BlockSpec((None, tm, tn), lambda i, j: (l_in, i, j))
        ust_spec = pl.BlockSpec((tm, tn), lambda i, j: (i, j))
        ust_shape = jax.ShapeDtypeStruct((M, N), F32)
        scratch = [pltpu.VMEM((tm, D), BF16)]
    else:
        buf_spec = pl.BlockSpec((None, None, CONV_W - 1, tn), lambda i, j: (l_in, i // tps, 0, j))
        ust_spec = pl.BlockSpec((None, CONV_W - 1, tn), lambda i, j: (i, 0, j))
        ust_shape = jax.ShapeDtypeStruct((M // tm, CONV_W - 1, N), F32)
        scratch = [pltpu.VMEM((tm, D), BF16), pltpu.VMEM((N // tn, 8, tn), F32)]
    return pl.pallas_call(
        kern,
        grid=(M // tm, N // tn),
        in_specs=[
            pl.BlockSpec((tm, D), lambda i, j: (i, 0)),
            pl.BlockSpec((None, 1, D), lambda i, j: (l, 0, 0)),
            pl.BlockSpec((None, D, tn), lambda i, j: (l, 0, j)),
            pl.BlockSpec((None, D, tn), lambda i, j: (l, 0, j)),
            pl.BlockSpec((None, CONV_W, tn), lambda i, j: (l, 0, j)),
            pl.BlockSpec((None, 1, tn), lambda i, j: (l, 0, j)),
            buf_spec,
        ],
        out_specs=[pl.BlockSpec((tm, tn), lambda i, j: (i, j)), ust_spec],
        out_shape=[jax.ShapeDtypeStruct((M, N), BF16), ust_shape],
        scratch_shapes=scratch,
        compiler_params=_params(("arbitrary", "arbitrary")),
        name="ffn_up",
    )(x, g_all, wu_all, wg_all, cw_all, cb_all, buf)


def _ffn_down_kernel(x_ref, act_ref, w_ref, o_ref):
    o_ref[...] = x_ref[...] + _dot(act_ref[...], w_ref[...])


def _ffn_down(x, act, w_all, l, tm, tn):
    M, D = x.shape
    K = act.shape[-1]
    return pl.pallas_call(
        _ffn_down_kernel,
        grid=(M // tm, D // tn),
        in_specs=[
            pl.BlockSpec((tm, tn), lambda i, j: (i, j)),
            pl.BlockSpec((tm, K), lambda i, j: (i, 0)),
            pl.BlockSpec((None, K, tn), lambda i, j: (l, 0, j)),
        ],
        out_specs=pl.BlockSpec((tm, tn), lambda i, j: (i, j)),
        out_shape=jax.ShapeDtypeStruct((M, D), F32),
        compiler_params=_params(("arbitrary", "arbitrary")),
        name="ffn_down",
    )(x, act, w_all)


def _final_norm_kernel(x_ref, g_ref, o_ref):
    x = x_ref[...]
    r = lax.rsqrt(jnp.mean(x * x, axis=-1, keepdims=True) + EPS)
    o_ref[...] = x * r * g_ref[...]


def _final_norm_skip_kernel(x_ref, g_ref, o_ref):
    x = x_ref[0]
    r = lax.rsqrt(jnp.mean(x * x, axis=-1, keepdims=True) + EPS)
    o_ref[0] = x * r * g_ref[...]


def _final_norm_skip(x3, g, skip, tr):
    B, T, D = x3.shape
    assert skip % 8 == 0 and tr % 8 == 0 and (T - skip) % tr == 0
    return pl.pallas_call(
        _final_norm_skip_kernel,
        grid=(B, (T - skip) // tr),
        in_specs=[pl.BlockSpec((pl.Element(1), pl.Element(tr), pl.Element(D)),
                               lambda b, i: (b, (skip // 8 + i * (tr // 8)) * 8, 0)),
                  pl.BlockSpec((1, D), lambda b, i: (0, 0))],
        out_specs=pl.BlockSpec((1, tr, D), lambda b, i: (b, i, 0)),
        out_shape=jax.ShapeDtypeStruct((B, T - skip, D), F32),
        compiler_params=_params(("arbitrary", "arbitrary")),
        name="final_norm_skip",
    )(x3, g)


def _final_norm(x, g, tm):
    M, D = x.shape
    return pl.pallas_call(
        _final_norm_kernel,
        grid=(M // tm,),
        in_specs=[pl.BlockSpec((tm, D), lambda i: (i, 0)), pl.BlockSpec((1, D), lambda i: (0, 0))],
        out_specs=pl.BlockSpec((tm, D), lambda i: (i, 0)),
        out_shape=jax.ShapeDtypeStruct((M, D), F32),
        compiler_params=_params(("arbitrary",)),
        name="final_norm",
    )(x, g)


def _rwkv_kernel(prkv_ref, plow_ref, shr_in, shl_in, s0_ref, mur_ref, mul_ref, w0_ref, wup_ref, a0_ref, aup_ref,
                 gup_ref, kk_ref, ka_ref, rk_ref, ln_ref, e_ref, acc_ref,
                 y_ref, sout_ref, shr_out, shl_out,
                 at_scr, rt_scr, kt_scr, bt_scr, v_scr, gam_scr, y_scr, *, G, C, t0, U):
    R = G * C
    W = R_WIDTH

    @pl.when(pl.program_id(1) == 0)
    def _():
        sout_ref[...] = s0_ref[...]
        shr_out[...] = shr_in[...]
        shl_out[...] = shl_in[...]

    def token_shift(p_ref, car_ref, mu_ref, w):
        p = p_ref[...].reshape(R, w)
        tl = _tloc(G, C, w)
        carry = jnp.concatenate([jnp.broadcast_to(car_ref[g], (C, w)) for g in range(G)], axis=0)
        prev = jnp.where(tl == t0, carry, pltpu.roll(p, 1, 0))
        for g in range(G):
            car_ref[g] = p[g * C + C - 1:g * C + C, :]
        return p + (prev - p) * mu_ref[...]

    ps = token_shift(prkv_ref, shr_out, mur_ref, 3 * W)
    lo = token_shift(plow_ref, shl_out, mul_ref, R_LOW)
    r = ps[:, 0:W]
    k = ps[:, W:2 * W]
    v = ps[:, 2 * W:3 * W]
    lw = -RWKV_DECAY_SCALE * jax.nn.sigmoid(w0_ref[...] + _dot(jnp.tanh(lo).astype(BF16), wup_ref[...]))
    a = jax.nn.sigmoid(a0_ref[...] + _dot(lo.astype(BF16), aup_ref[...]))
    gate = _dot(jax.nn.sigmoid(lo).astype(BF16), gup_ref[...])
    e = e_ref[...]
    kk = k * kk_ref[...]
    kk = kk / jnp.maximum(jnp.sqrt(_segsum(kk * kk, e)), 1e-12)
    k2 = k * (1.0 + (a - 1.0) * ka_ref[...])
    ka = kk * a
    tl = _tloc(G, C, W)
    if t0 > 0:
        valid = tl >= t0
        lw = jnp.where(valid, lw, 0.0)
        kk = jnp.where(valid, kk, 0.0)
        ka = jnp.where(valid, ka, 0.0)
        k2 = jnp.where(valid, k2, 0.0)
        v = jnp.where(valid, v, 0.0)
    cum = _cumsum_rows(lw, tl, C)
    gam = jnp.exp(cum)
    inv = jnp.exp(-cum)
    at_scr[...] = kk * jnp.exp(cum - lw)
    rt_scr[...] = r * gam
    kt_scr[...] = k2 * inv
    bt_scr[...] = ka * inv
    v_scr[...] = v
    gam_scr[...] = gam

    row = lax.broadcasted_iota(jnp.int32, (C, C), 0)
    col = lax.broadcasted_iota(jnp.int32, (C, C), 1)
    low_s = row > col
    low_i = row >= col

    def seq_body(it, carry):
        gs = [it * U + i for i in range(U)]
        lanes = [(i, h) for i in range(U) for h in range(R_HEADS)]
        data = []
        for g in gs:
            off = pl.multiple_of(g * C, 8)
            rows = pl.ds(off, C)
            data.append((at_scr[rows, :], rt_scr[rows, :], kt_scr[rows, :], bt_scr[rows, :], v_scr[rows, :],
                         gam_scr[pl.ds(off + C - 8, 8), :][7:8, :]))
        s_in = [sout_ref[gs[i], h] for i, h in lanes]

        def head(x, h):
            return x[:, R_HEAD_DIM * h:R_HEAD_DIM * (h + 1)]

        ar = [jnp.concatenate([head(data[i][0], h), head(data[i][1], h)], axis=0).astype(BF16) for i, h in lanes]
        ktb = [head(data[i][2], h).astype(BF16) for i, h in lanes]
        btb = [head(data[i][3], h).astype(BF16) for i, h in lanes]
        vb = [head(data[i][4], h).astype(BF16) for i, h in lanes]
        n = range(len(lanes))
        m_k = [_dg(ar[j], ktb[j], NT) for j in n]
        m_b = [_dg(ar[j], btb[j], NT) for j in n]
        n_pow = [jnp.where(low_s, m_b[j][0:C], 0.0) for j in n]
        acc = [-n_pow[j] for j in n]
        pre_rhs = [_dot(jnp.where(low_s, m_k[j][0:C], 0.0).astype(BF16), vb[j]) for j in n]
        pre_y = [_dot(jnp.where(low_i, m_k[j][C:2 * C], 0.0).astype(BF16), vb[j]) for j in n]
        pre_ds = [_dg(vb[j], ktb[j], TN) for j in n]
        trb = [jnp.where(low_i, m_b[j][C:2 * C], 0.0).astype(BF16) for j in n]
        span = 2
        nb = [n_pow[j].astype(BF16) for j in n]
        n_pow = [_dot(nb[j], nb[j]) for j in n]
        while span < C:
            nb = [n_pow[j].astype(BF16) for j in n]
            span *= 2
            prod = [_dot(acc[j].astype(BF16), nb[j]) for j in n]
            if span < C:
                nxt = [_dot(nb[j], nb[j]) for j in n]
            acc = [acc[j] + n_pow[j] + prod[j] for j in n]
            if span < C:
                n_pow = nxt
        accb = [acc[j].astype(BF16) for j in n]
        m_s = [_dg(ar[j], s_in[j].astype(BF16), NT) for j in n]
        rhs = [m_s[j][0:C] + pre_rhs[j] for j in n]
        u = [rhs[j] + _dot(accb[j], rhs[j].astype(BF16)) for j in n]
        ub = [u[j].astype(BF16) for j in n]
        yh = [m_s[j][C:2 * C] + pre_y[j] - _dot(trb[j], ub[j]) for j in n]
        ds = [pre_ds[j] - _dg(ub[j], btb[j], TN) for j in n]
        for j, (i, h) in enumerate(lanes):
            sout_ref[gs[i], h] = (s_in[j] + ds[j]) * head(data[i][5], h)
        for i, g in enumerate(gs):
            rows = pl.ds(pl.multiple_of(g * C, 8), C)
            y_scr[rows, :] = jnp.concatenate(yh[i * R_HEADS:(i + 1) * R_HEADS], axis=1)
        return carry

    lax.fori_loop(0, G // U, seq_body, 0)

    y = y_scr[...]
    inv_n = 1.0 / R_HEAD_DIM
    d = y - _segsum(y, e) * inv_n
    var = _segsum(d * d, e) * inv_n
    y = d * lax.rsqrt(var + RWKV_GN_EPS) * ln_ref[...]
    y = y + _segsum(r * k2 * rk_ref[...], e) * v
    y_ref[...] = (y * gate).reshape(G, C, W)


def _rwkv(p3, st, acc, l_in, prm, l, cfg):
    B, T, _ = p3.shape
    G, C = cfg.g_mix, cfg.c_rwkv
    W = R_WIDTH
    shr_in, shl_in, s0 = st

    def pspec(w, off):
        return pl.BlockSpec((G, C, w), lambda i, c: (i, c, off // w))

    def lay(*shape):
        return pl.BlockSpec((None,) + shape, lambda i, c: (l,) + (0,) * len(shape))

    kern = functools.partial(_rwkv_kernel, G=G, C=C, t0=cfg.t0, U=cfg.u_rwkv)
    R = G * C
    return pl.pallas_call(
        kern,
        grid=(B // G, T // C),
        in_specs=[
            pspec(3 * W, P_R), pspec(R_LOW, P_RL),
            pl.BlockSpec((None, G, 1, 3 * W), lambda i, c: (l_in, i, 0, 0)),
            pl.BlockSpec((None, G, 1, R_LOW), lambda i, c: (l_in, i, 0, 0)),
            pl.BlockSpec((None, G, R_HEADS, R_HEAD_DIM, R_HEAD_DIM), lambda i, c: (l_in, i, 0, 0, 0)),
            lay(1, 3 * W), lay(1, R_LOW), lay(1, W), lay(R_LOW, W), lay(1, W), lay(R_LOW, W), lay(R_LOW, W),
            lay(1, W), lay(1, W), lay(1, W), lay(1, W),
            pl.BlockSpec((W, W), lambda i, c: (0, 0)),
            pl.BlockSpec(memory_space=pl.ANY),
        ],
        input_output_aliases={17: 1},
        out_specs=[
            pl.BlockSpec((G, C, W), lambda i, c: (i, c, 0)),
            pl.BlockSpec((None, G, R_HEADS, R_HEAD_DIM, R_HEAD_DIM), lambda i, c: (l, i, 0, 0, 0)),
            pl.BlockSpec((G, 1, 3 * W), lambda i, c: (i, 0, 0)),
            pl.BlockSpec((G, 1, R_LOW), lambda i, c: (i, 0, 0)),
        ],
        out_shape=[
            jax.ShapeDtypeStruct((B, T, W), F32),
            jax.ShapeDtypeStruct(acc.shape, F32),
            jax.ShapeDtypeStruct((B, 1, 3 * W), F32),
            jax.ShapeDtypeStruct((B, 1, R_LOW), F32),
        ],
        scratch_shapes=[pltpu.VMEM((R, W), F32)] * 7,
        compiler_params=_params(("arbitrary", "arbitrary")),
        name="rwkv7",
    )(p3, p3, shr_in, shl_in, s0, prm["mu_rkv"], prm["mu_low"], prm["rwkv_w0"], prm["rwkv_w_up"], prm["rwkv_a0"],
      prm["rwkv_a_up"], prm["rwkv_g_up"], prm["rwkv_k_k"], prm["rwkv_k_a"], prm["rwkv_r_k"], prm["rwkv_ln"],
      prm["e_head"], acc)


def _s5_kernel(u_ref, h0r_ref, h0i_ref, bbhr_ref, bblr_ref, bbhi_ref, bbli_ref, ccr_ref, cci_ref, ap_ref, pw_ref,
               d_ref, wglu_ref, bglu_ref,
               y_ref, hr_out, hi_out, hre_scr, him_scr, *, G, C, t0, split):
    R = G * C
    NB = S5_FLAT // S5_CHUNKS
    UB = S5_WIDTH // S5_CHUNKS

    @pl.when(pl.program_id(1) == 0)
    def _():
        hr_out[...] = h0r_ref[...]
        hi_out[...] = h0i_ref[...]

    u = u_ref[...].reshape(R, S5_WIDTH)
    uh = u.astype(BF16)
    ul = (u - uh.astype(F32)).astype(BF16)
    valid = _tloc(G, C, NB) >= t0
    for j in range(S5_CHUNKS):
        us = slice(UB * j, UB * (j + 1))
        br = _dot(uh[:, us], bbhr_ref[j])
        bi = _dot(uh[:, us], bbhi_ref[j])
        if split:
            br = br + _dot(ul[:, us], bbhr_ref[j]) + _dot(uh[:, us], bblr_ref[j])
            bi = bi + _dot(ul[:, us], bbhi_ref[j]) + _dot(uh[:, us], bbli_ref[j])
        if t0 > 0:
            br = jnp.where(valid, br, 0.0)
            bi = jnp.where(valid, bi, 0.0)
        hre_scr[:, NB * j:NB * (j + 1)] = br
        him_scr[:, NB * j:NB * (j + 1)] = bi

    steps = [(1 << k, ap_ref[S5_SCAN * 2 * k:S5_SCAN * (2 * k + 1), :],
              ap_ref[S5_SCAN * (2 * k + 1):S5_SCAN * (2 * k + 2), :]) for k in range(3)]
    pwr = pw_ref[0:S5_SCAN, :]
    pwi = pw_ref[S5_SCAN:2 * S5_SCAN, :]

    def seq_body(g, carry):
        def tile_body(i, hc):
            hcr, hci = hc
            rows = pl.ds(pl.multiple_of(g * C + i * S5_SCAN, S5_SCAN), S5_SCAN)
            xr = hre_scr[rows, :]
            xi = him_scr[rows, :]
            for sh, ar, ai in steps:
                sr = pltpu.roll(xr, sh, 0)
                si = pltpu.roll(xi, sh, 0)
                xr, xi = xr + ar * sr - ai * si, xi + ar * si + ai * sr
            xr, xi = xr + pwr * hcr - pwi * hci, xi + pwr * hci + pwi * hcr
            hre_scr[rows, :] = xr
            him_scr[rows, :] = xi
            return xr[S5_SCAN - 1:S5_SCAN, :], xi[S5_SCAN - 1:S5_SCAN, :]

        hcr, hci = lax.fori_loop(0, C // S5_SCAN, tile_body, (hr_out[g], hi_out[g]))
        hr_out[g] = hcr
        hi_out[g] = hci
        return carry

    lax.fori_loop(0, G, seq_body, 0)

    ys = []
    for j in range(S5_CHUNKS):
        hs = slice(NB * j, NB * (j + 1))
        ys.append(_dot(hre_scr[:, hs].astype(BF16), ccr_ref[j]) - _dot(him_scr[:, hs].astype(BF16), cci_ref[j]))
    y = jnp.concatenate(ys, axis=1) + d_ref[...] * u
    y = _gelu(y)
    y = y * jax.nn.sigmoid(_dot(y.astype(BF16), wglu_ref[...]) + bglu_ref[...])
    y_ref[...] = y.reshape(G, C, S5_WIDTH)


def _s5(p3, st, l_in, prm, l, cfg):
    B, T, _ = p3.shape
    G, C = cfg.g_s5, cfg.c_s5
    h0r, h0i = st
    NB = S5_FLAT // S5_CHUNKS
    UB = S5_WIDTH // S5_CHUNKS

    def lay(*shape):
        return pl.BlockSpec((None,) + shape, lambda i, c: (l,) + (0,) * len(shape))

    st_spec = pl.BlockSpec((None, G, 1, S5_FLAT), lambda i, c: (l_in, i, 0, 0))
    out_st = pl.BlockSpec((G, 1, S5_FLAT), lambda i, c: (i, 0, 0))
    kern = functools.partial(_s5_kernel, G=G, C=C, t0=cfg.t0, split=cfg.s5_split)
    pw = prm["s5_pw"][cfg.t0]
    return pl.pallas_call(
        kern,
        grid=(B // G, T // C),
        in_specs=[
            pl.BlockSpec((G, C, S5_WIDTH), lambda i, c: (i, c, P_S5 // S5_WIDTH)),
            st_spec, st_spec,
            lay(S5_CHUNKS, UB, NB), lay(S5_CHUNKS, UB, NB), lay(S5_CHUNKS, UB, NB), lay(S5_CHUNKS, UB, NB),
            lay(S5_CHUNKS, NB, UB), lay(S5_CHUNKS, NB, UB),
            lay(6 * S5_SCAN, S5_FLAT), lay(2 * S5_SCAN, S5_FLAT),
            lay(1, S5_WIDTH), lay(S5_WIDTH, S5_WIDTH), lay(1, S5_WIDTH),
        ],
        out_specs=[pl.BlockSpec((G, C, S5_WIDTH), lambda i, c: (i, c, 0)), out_st, out_st],
        out_shape=[
            jax.ShapeDtypeStruct((B, T, S5_WIDTH), F32),
            jax.ShapeDtypeStruct((B, 1, S5_FLAT), F32),
            jax.ShapeDtypeStruct((B, 1, S5_FLAT), F32),
        ],
        scratch_shapes=[pltpu.VMEM((G * C, S5_FLAT), F32)] * 2,
        compiler_params=_params(("arbitrary", "arbitrary")),
        name="s5",
    )(p3, h0r, h0i, prm["s5_bbh_re"], prm["s5_bbl_re"], prm["s5_bbh_im"], prm["s5_bbl_im"], prm["s5_cc_re"],
      prm["s5_cc_im"], prm["s5_apow"], pw, prm["s5_D"], prm["s5_w_glu"], prm["s5_b_glu"])


def _gla_core(q, kx, gdec, v, s0_ref, sout_ref, st_scr, q_scr, k_scr, b_scr, v_scr, o_scr, *, G, CB, C, t0, K, KS):
    assert CB % C == 0 and C & (C - 1) == 0 and (t0 == 0 or CB == C)
    H = G_HEADS
    V = G_VAL_DIM
    HK = H * K
    c = pl.program_id(1)

    @pl.when(c == 0)
    def _():
        for g in range(G):
            for h in range(H):
                s = s0_ref[g, h].T
                if KS < K:
                    s = jnp.concatenate([s, jnp.zeros((V, K - KS), F32)], axis=1)
                st_scr[g, h] = s

    tl = lax.broadcasted_iota(jnp.int32, (G * CB, HK), 0) & (C - 1)
    if t0 > 0:
        valid = tl >= t0
        gdec = jnp.where(valid, gdec, 0.0)
        kx = jnp.where(valid, kx, 0.0)
    q_scr[...] = q
    k_scr[...] = kx
    b_scr[...] = _cumsum_rows(gdec, tl, C)
    v_scr[...] = v
    ti = lax.broadcasted_iota(jnp.int32, (C, HK), 0)

    row = lax.broadcasted_iota(jnp.int32, (C, C), 0)
    col = lax.broadcasted_iota(jnp.int32, (C, C), 1)
    lanes = [(g, h) for g in range(G) for h in range(H)]

    def ksl(h):
        return slice(K * h, K * (h + 1))

    def vsl(h):
        return slice(V * h, V * (h + 1))

    def chunk_body(j, carry):
        rows = [pl.ds(pl.multiple_of(g * CB + j * C, 8), C) for g in range(G)]
        qg = [q_scr[r, :] for r in rows]
        kg = [k_scr[r, :] for r in rows]
        bg = [b_scr[r, :] for r in rows]
        vg = [v_scr[r, :] for r in rows]
        vb = [x.astype(BF16) for x in vg]
        b_last = [b[C - 1:C, :] for b in bg]
        qe = [(qg[g] * jnp.exp(bg[g])).astype(BF16) for g in range(G)]
        kd = [(kg[g] * jnp.exp(b_last[g] - bg[g])).astype(BF16) for g in range(G)]
        dec_last = [jnp.exp(b) for b in b_last]
        s_in = [st_scr[g, h] for g, h in lanes]
        inter = [_dg(qe[g][:, ksl(h)], s_in[n].astype(BF16), NT) for n, (g, h) in enumerate(lanes)]
        upd = [_dg(vb[g][:, vsl(h)], kd[g][:, ksl(h)], TN) for g, h in lanes]
        b_min = b_last[0]
        for b in b_last[1:]:
            b_min = jnp.minimum(b_min, b)

        def factored():
            kn = [(kg[g] * jnp.exp(-bg[g])).astype(BF16) for g in range(G)]
            att = [_dg(qe[g][:, ksl(h)], kn[g][:, ksl(h)], NT) for g, h in lanes]
            att = [jnp.where(row >= col, a, 0.0).astype(BF16) for a in att]
            outs = [_dot(att[n], vb[g][:, vsl(h)]) for n, (g, h) in enumerate(lanes)]
            return [jnp.concatenate(outs[g * H:(g + 1) * H], axis=1) for g in range(G)]

        def pairwise():
            res = []
            for g in range(G):
                intra = [jnp.zeros((C, V), F32) for _ in range(H)]
                for jj in range(C):
                    causal = ti >= jj
                    e = jnp.where(causal, qg[g] * kg[g][jj:jj + 1, :]
                                  * jnp.exp(jnp.where(causal, bg[g] - bg[g][jj:jj + 1, :], 0.0)), 0.0)
                    for h in range(H):
                        a = jnp.sum(e[:, ksl(h)], axis=-1, keepdims=True)
                        intra[h] = intra[h] + a * vg[g][jj:jj + 1, vsl(h)]
                res.append(jnp.concatenate(intra, axis=1))
            return res

        intra = lax.cond(jnp.min(b_min) >= -GLA_SAFE_DECAY, factored, pairwise)
        for n, (g, h) in enumerate(lanes):
            st_scr[g, h] = s_in[n] * dec_last[g][:, ksl(h)] + upd[n]
        for g in range(G):
            o_scr[rows[g], :] = jnp.concatenate(inter[g * H:(g + 1) * H], axis=1) + intra[g]
        return carry

    lax.fori_loop(0, CB // C, chunk_body, 0)

    @pl.when(c == pl.num_programs(1) - 1)
    def _():
        for g in range(G):
            for h in range(H):
                sout_ref[g, h] = st_scr[g, h][:, 0:KS].T


def _head_norm_gate(o, norm, gate_raw):
    V = G_VAL_DIM
    ys = []
    for h in range(G_HEADS):
        oh = o[:, V * h:V * (h + 1)]
        ys.append(oh * lax.rsqrt(jnp.mean(oh * oh, axis=-1, keepdims=True) + EPS))
    y = jnp.concatenate(ys, axis=1) * norm
    return y * (gate_raw * jax.nn.sigmoid(gate_raw))


def _hgrn_kernel(q_ref, f_ref, i_ref, g_ref, lb_ref, norm_ref, s0_ref, acc_ref, y_ref, sout_ref,
                 st_scr, q_scr, k_scr, b_scr, v_scr, o_scr, *, G, C, chunk, t0, layer):
    R = G * C
    W = H_WIDTH
    lbr = lb_ref[...]
    ex = jnp.exp(lbr - jnp.max(lbr, axis=0, keepdims=True))
    sm = ex / jnp.sum(ex, axis=0, keepdims=True)
    lb = jnp.zeros((1, W), F32)
    for i in range(1, layer + 1):
        lb = lb + sm[i:i + 1, :]
    qr = q_ref[...].reshape(R, W)
    kx = jnp.minimum((1.0 - lb) * jax.nn.sigmoid(-f_ref[...].reshape(R, W)), HGRN_MAX_INPUT)
    _gla_core(qr * jax.nn.sigmoid(qr), kx, jnp.log1p(-kx), i_ref[...].reshape(R, W), s0_ref, sout_ref,
              st_scr, q_scr, k_scr, b_scr, v_scr, o_scr, G=G, CB=C, C=chunk, t0=t0, K=H_HEAD_DIM, KS=H_HEAD_DIM)
    y_ref[...] = _head_norm_gate(o_scr[...], norm_ref[...], g_ref[...].reshape(R, W)).reshape(G, C, W)


def _gla_kernel(v_ref, gate_ref, q_ref, k_ref, gl_ref, gkup_ref, gkb_ref, norm_ref, s0_ref, acc_ref, y_ref,
                sout_ref,
                st_scr, q_scr, k_scr, b_scr, v_scr, o_scr, *, G, C, chunk, t0):
    R = G * C
    z = _dot(gl_ref[...].reshape(R, GL_PAD).astype(BF16), gkup_ref[...]) + gkb_ref[...]
    gdec = -(jnp.maximum(-z, 0.0) + jnp.log1p(jnp.exp(-jnp.abs(z)))) / GLA_GATE_NORM
    _gla_core(q_ref[...].reshape(R, G_QK_PAD) * (G_KEY_DIM ** -0.5), k_ref[...].reshape(R, G_QK_PAD), gdec,
              v_ref[...].reshape(R, G_WIDTH), s0_ref, sout_ref, st_scr, q_scr, k_scr, b_scr, v_scr, o_scr,
              G=G, CB=C, C=chunk, t0=t0, K=G_KEY_PAD, KS=G_KEY_DIM)
    y_ref[...] = _head_norm_gate(o_scr[...], norm_ref[...], gate_ref[...].reshape(R, G_WIDTH)).reshape(G, C, G_WIDTH)


def _gla_like(p3, s0, acc, l_in, prm, l, cfg, hgrn):
    B, T, _ = p3.shape
    G, C = cfg.g_mix, cfg.cb_gla
    H, V = G_HEADS, G_VAL_DIM
    K = H_HEAD_DIM if hgrn else G_KEY_PAD
    KS = H_HEAD_DIM if hgrn else G_KEY_DIM
    HK = H * K
    R = G * C

    def pspec(w, off):
        return pl.BlockSpec((G, C, w), lambda i, c: (i, c, off // w))

    def lay(*shape):
        return pl.BlockSpec((None,) + shape, lambda i, c: (l,) + (0,) * len(shape))

    st_in = pl.BlockSpec((None, G, H, KS, V), lambda i, c: (l_in, i, 0, 0, 0))
    if hgrn:
        kern = functools.partial(_hgrn_kernel, G=G, C=C, chunk=cfg.c_gla, t0=cfg.t0, layer=l)
        in_specs = [pspec(HK, P_H), pspec(HK, P_H + HK), pspec(HK, P_H + 2 * HK), pspec(HK, P_H + 3 * HK),
                    pl.BlockSpec((DEPTH, HK), lambda i, c: (0, 0)), lay(1, HK), st_in]
        args = (p3, p3, p3, p3, prm["hgrn_lb_raw"], prm["hgrn_norm"], s0)
        name = "hgrn2"
    else:
        kern = functools.partial(_gla_kernel, G=G, C=C, chunk=cfg.c_gla, t0=cfg.t0)
        in_specs = [pspec(G_WIDTH, P_GV), pspec(G_WIDTH, P_GG), pspec(HK, P_GQ), pspec(HK, P_GK),
                    pspec(GL_PAD, P_GL), lay(GL_PAD, HK), lay(1, HK), lay(1, G_WIDTH), st_in]
        args = (p3, p3, p3, p3, p3, prm["gla_gk_up"], prm["gla_gk_b"], prm["gla_norm"], s0)
        name = "gla"
    return pl.pallas_call(
        kern,
        grid=(B // G, T // C),
        in_specs=in_specs + [pl.BlockSpec(memory_space=pl.ANY)],
        out_specs=[pl.BlockSpec((G, C, H * V), lambda i, c: (i, c, 0)),
                   pl.BlockSpec((None, G, H, KS, V), lambda i, c: (l, i, 0, 0, 0))],
        out_shape=[jax.ShapeDtypeStruct((B, T, H * V), F32), jax.ShapeDtypeStruct(acc.shape, F32)],
        input_output_aliases={len(args): 1},
        scratch_shapes=[pltpu.VMEM((G, H, V, K), F32), pltpu.VMEM((R, HK), F32), pltpu.VMEM((R, HK), F32),
                        pltpu.VMEM((R, HK), F32), pltpu.VMEM((R, H * V), F32), pltpu.VMEM((R, H * V), F32)],
        compiler_params=_params(("arbitrary", "arbitrary")),
        name=name,
    )(*args, acc)


def _s5_tables(A_re, A_im, log_dt, B_re, B_im, C_re, C_im, t0s):
    L = A_re.shape[0]
    A_re = A_re.astype(F32)
    A_im = A_im.astype(F32)
    dt = jnp.exp(log_dt.astype(F32))[..., None]
    mag = jnp.exp(A_re * dt)
    ab_re = mag * jnp.cos(A_im * dt)
    ab_im = mag * jnp.sin(A_im * dt)
    den = A_re * A_re + A_im * A_im
    n_re = ab_re - 1.0
    co_re = (n_re * A_re + ab_im * A_im) / den
    co_im = (ab_im * A_re - n_re * A_im) / den
    B_re = B_re.astype(F32)
    B_im = B_im.astype(F32)
    bb_re = co_re[..., None] * B_re - co_im[..., None] * B_im
    bb_im = co_re[..., None] * B_im + co_im[..., None] * B_re
    gpc = S5_GROUPS // S5_CHUNKS
    eye = jnp.eye(gpc, dtype=F32)

    def block_in(bb):
        t = bb.reshape(L, S5_CHUNKS, gpc, S5_STATE, S5_GROUP_CH).transpose(0, 1, 2, 4, 3)
        t = t[:, :, :, :, None, :] * eye[None, None, :, None, :, None]
        return t.reshape(L, S5_CHUNKS, gpc * S5_GROUP_CH, gpc * S5_STATE)

    def block_out(cc):
        t = cc.astype(F32).reshape(L, S5_CHUNKS, gpc, S5_GROUP_CH, S5_STATE).transpose(0, 1, 2, 4, 3)
        t = t[:, :, :, :, None, :] * eye[None, None, :, None, :, None]
        return t.reshape(L, S5_CHUNKS, gpc * S5_STATE, gpc * S5_GROUP_CH)

    def split(x):
        hi = x.astype(BF16)
        return hi, (x - hi.astype(F32)).astype(BF16)

    bbh_re, bbl_re = split(block_in(bb_re))
    bbh_im, bbl_im = split(block_in(bb_im))

    def power(m):
        mg = jnp.exp(m * dt * A_re)
        return (mg * jnp.cos(m * dt * A_im)).reshape(L, S5_FLAT), (mg * jnp.sin(m * dt * A_im)).reshape(L, S5_FLAT)

    zero = jnp.zeros((L, S5_FLAT), F32)
    levels = []
    for sh in (1, 2, 4):
        pr, pi = power(float(sh))
        levels += [zero if r < sh else pr for r in range(S5_SCAN)]
        levels += [zero if r < sh else pi for r in range(S5_SCAN)]
    apow = jnp.stack(levels, axis=1)
    pws = {}
    for t0 in t0s:
        first = t0 % S5_SCAN
        res, ims = [], []
        for r in range(S5_SCAN):
            if r < first:
                res.append(zero)
                ims.append(zero)
            else:
                pr, pi = power(float(r - first + 1))
                res.append(pr)
                ims.append(pi)
        pws[t0] = jnp.stack(res + ims, axis=1)
    return dict(s5_bbh_re=bbh_re, s5_bbl_re=bbl_re, s5_bbh_im=bbh_im, s5_bbl_im=bbl_im,
                s5_cc_re=block_out(C_re).astype(BF16), s5_cc_im=block_out(C_im).astype(BF16),
                s5_apow=apow, s5_pw=pws)


def _prep_params(raw, t0s):
    L = raw["w_in"].shape[0]
    W = R_WIDTH
    w_in = raw["w_in"]
    o_s5 = R_COLS
    o_h = o_s5 + S5_WIDTH
    o_g = o_h + 4 * H_WIDTH
    o_gv = o_g + 2 * G_QK
    o_gl = o_gv + G_WIDTH
    o_gg = o_gl + G_GATE_RANK
    pad = jnp.zeros(w_in.shape[:2] + (GL_PAD - G_GATE_RANK,), w_in.dtype)

    def key_pad(w):
        w = w.reshape(w.shape[:-1] + (G_HEADS, G_KEY_DIM))
        w = jnp.pad(w, ((0, 0),) * (w.ndim - 1) + ((0, G_KEY_PAD - G_KEY_DIM),))
        return w.reshape(w.shape[:-2] + (G_QK_PAD,))

    w_in_p = jnp.concatenate([
        w_in[..., 0:3 * W], w_in[..., o_s5:o_h], w_in[..., o_h:o_g], w_in[..., o_gv:o_gl],
        w_in[..., o_gg:o_gg + G_WIDTH], key_pad(w_in[..., o_g:o_g + G_QK]), key_pad(w_in[..., o_g + G_QK:o_gv]),
        w_in[..., 3 * W:R_COLS], w_in[..., o_gl:o_gg], pad], axis=-1).astype(BF16)
    assert w_in_p.shape[-1] == P_COLS

    def row3(x):
        return x.reshape(L, 1, -1).astype(F32)

    def low_pad(w, off):
        return jnp.pad(w, ((0, 0), (off, R_LOW - off - w.shape[1]), (0, 0))).astype(BF16)

    ff = FF_PAD - D_FF
    w_up = raw["ffn_w_up"]
    head = jnp.arange(W) // R_HEAD_DIM
    prm = dict(
        norm_mix=row3(raw["norm_mix"]), norm_ffn=row3(raw["norm_ffn"]),
        w_in=w_in_p, w_out=raw["w_out"].astype(BF16),
        mu_rkv=row3(raw["rwkv_mu"][:, 0:3 * W]), mu_low=row3(raw["rwkv_mu"][:, 3 * W:]),
        rwkv_w0=row3(raw["rwkv_w0"]), rwkv_a0=row3(raw["rwkv_a0"]),
        rwkv_w_up=low_pad(raw["rwkv_w_up"], 0), rwkv_a_up=low_pad(raw["rwkv_a_up"], R_DECAY_RANK),
        rwkv_g_up=low_pad(raw["rwkv_g_up"], R_DECAY_RANK + R_ICL_RANK),
        rwkv_k_k=row3(raw["rwkv_k_k"]), rwkv_k_a=row3(raw["rwkv_k_a"]), rwkv_r_k=row3(raw["rwkv_r_k"]),
        rwkv_ln=row3(raw["rwkv_ln"]),
        e_head=(head[:, None] == head[None, :]).astype(BF16),
        s5_D=row3(raw["s5_D"]), s5_w_glu=raw["s5_w_glu"].astype(BF16), s5_b_glu=row3(raw["s5_b_glu"]),
        hgrn_lb_raw=raw["hgrn_lower_bounds"].astype(F32), hgrn_norm=row3(raw["hgrn_norm"]),
        gla_gk_up=jnp.pad(key_pad(raw["gla_gk_up"]), ((0, 0), (0, GL_PAD - G_GATE_RANK), (0, 0))).astype(BF16),
        gla_gk_b=row3(key_pad(raw["gla_gk_b"])), gla_norm=row3(raw["gla_norm"]),
        ffn_wu=jnp.pad(w_up[..., :D_FF], ((0, 0), (0, 0), (0, ff))).astype(BF16),
        ffn_wg=jnp.pad(w_up[..., D_FF:], ((0, 0), (0, 0), (0, ff))).astype(BF16),
        ffn_conv_w=jnp.pad(raw["ffn_conv_w"], ((0, 0), (0, 0), (0, ff))).astype(F32),
        ffn_conv_b=jnp.pad(raw["ffn_conv_b"], ((0, 0), (0, ff))).reshape(L, 1, FF_PAD).astype(F32),
        ffn_w_down=jnp.pad(raw["ffn_w_down"], ((0, 0), (0, ff), (0, 0))).astype(BF16),
    )
    prm.update(_s5_tables(raw["s5_A_re"], raw["s5_A_im"], raw["s5_log_dt"], raw["s5_B_re"], raw["s5_B_im"],
                          raw["s5_C_re"], raw["s5_C_im"], t0s))
    return prm


def _layer(x, st, acc, l_in, prm, l, cfg):
    shr, shl, s_rw, h_re, h_im, s_hg, s_gl, buf = st
    a_rw, a_hg, a_gl = acc
    p = _norm_matmul(x, prm["norm_mix"], prm["w_in"], l, cfg.tm, P_COLS // 4)
    p3 = p.reshape(cfg.B, cfg.T, P_COLS)
    y_r, a_rw, shr_n, shl_n = _rwkv(p3, (shr, shl, s_rw), a_rw, l_in, prm, l, cfg)
    y_s, h_re_n, h_im_n = _s5(p3, (h_re, h_im), l_in, prm, l, cfg)
    y_h, a_hg = _gla_like(p3, s_hg, a_hg, l_in, prm, l, cfg, True)
    y_g, a_gl = _gla_like(p3, s_gl, a_gl, l_in, prm, l, cfg, False)
    M = cfg.B * cfg.T
    ys = [y.reshape(M, GROUP_WIDTH) for y in (y_r, y_s, y_h, y_g)]
    x = _out_proj(x, ys, prm["w_out"], l, cfg.tm, 1024)
    act, ust = _ffn_up(x, prm["norm_ffn"], prm["ffn_wu"], prm["ffn_wg"], prm["ffn_conv_w"], prm["ffn_conv_b"], buf,
                       l, l_in, cfg, 512)
    x = _ffn_down(x, act, prm["ffn_w_down"], l, cfg.tm, 512)
    return x, (shr_n, shl_n, h_re_n, h_im_n, ust), (a_rw, a_hg, a_gl)


def _state_outputs(new, cfg):
    shr, shl, h_re, h_im, ust = new
    B = cfg.B
    shift = jnp.concatenate([shr, shl], axis=-1).reshape(B, R_COLS)
    if cfg.inject:
        conv = ust.reshape(B, cfg.T, FF_PAD)[:, cfg.T - (CONV_W - 1):, :D_FF]
    else:
        tps = cfg.T // cfg.tm
        conv = ust[tps - 1::tps, :, :D_FF]
    return (shift, h_re.reshape(B, S5_GROUPS, S5_STATE), h_im.reshape(B, S5_GROUPS, S5_STATE), conv)


PROMPT_PAD = 0
SAMPLE_PAD = 4


def kernel(x_prompt, x_sample, state_rwkv, state_rwkv_shift, state_s5_re, state_s5_im, state_hgrn, state_gla, state_ffn_conv, meta_tokens, norm_mix, w_in, w_out, rwkv_mu, rwkv_w0, rwkv_w_up, rwkv_a0, rwkv_a_up, rwkv_g_up, rwkv_k_k, rwkv_k_a, rwkv_r_k, rwkv_ln, s5_A_re, s5_A_im, s5_log_dt, s5_B_re, s5_B_im, s5_C_re, s5_C_im, s5_D, s5_w_glu, s5_b_glu, hgrn_lower_bounds, hgrn_norm, gla_gk_up, gla_gk_b, gla_norm, norm_ffn, ffn_w_up, ffn_conv_w, ffn_conv_b, ffn_w_down, norm_final):
    L = DEPTH
    Bp, Sp, D = x_prompt.shape
    Bs, Ss, _ = x_sample.shape
    Tp = N_META + Sp
    Ts = SAMPLE_PAD + Ss
    raw = dict(norm_mix=norm_mix, w_in=w_in, w_out=w_out, rwkv_mu=rwkv_mu, rwkv_w0=rwkv_w0, rwkv_w_up=rwkv_w_up,
               rwkv_a0=rwkv_a0, rwkv_a_up=rwkv_a_up, rwkv_g_up=rwkv_g_up, rwkv_k_k=rwkv_k_k, rwkv_k_a=rwkv_k_a,
               rwkv_r_k=rwkv_r_k, rwkv_ln=rwkv_ln, s5_A_re=s5_A_re, s5_A_im=s5_A_im, s5_log_dt=s5_log_dt,
               s5_B_re=s5_B_re, s5_B_im=s5_B_im, s5_C_re=s5_C_re, s5_C_im=s5_C_im, s5_D=s5_D, s5_w_glu=s5_w_glu,
               s5_b_glu=s5_b_glu, hgrn_lower_bounds=hgrn_lower_bounds, hgrn_norm=hgrn_norm, gla_gk_up=gla_gk_up,
               gla_gk_b=gla_gk_b, gla_norm=gla_norm, norm_ffn=norm_ffn, ffn_w_up=ffn_w_up, ffn_conv_w=ffn_conv_w,
               ffn_conv_b=ffn_conv_b, ffn_w_down=ffn_w_down)
    prm = _prep_params(raw, (PROMPT_PAD, SAMPLE_PAD))

    cfg_p = _Cfg(B=Bp, T=Tp, t0=PROMPT_PAD, tm=Tp // 3, g_mix=Bp, u_rwkv=2, c_rwkv=48, cb_gla=48, c_gla=16, s5_split=False, g_s5=1, c_s5=Tp // 3,
                 inject=False)
    cfg_s = _Cfg(B=Bs, T=Ts, t0=SAMPLE_PAD, tm=Bs * Ts, g_mix=8, u_rwkv=4, c_rwkv=Ts, cb_gla=Ts, c_gla=Ts, s5_split=True, g_s5=32, c_s5=Ts,
                 inject=True)

    dt = x_prompt.dtype
    meta = jnp.broadcast_to(meta_tokens.astype(dt)[None], (Bp, N_META, D))
    xp = jnp.concatenate([meta, x_prompt], axis=1).reshape(Bp * Tp, D)
    xs = jnp.pad(x_sample, ((0, 0), (SAMPLE_PAD, 0), (0, 0))).reshape(Bs * Ts, D)

    W = R_WIDTH
    st_p = (jnp.zeros((1, Bp, 1, 3 * W), F32), jnp.zeros((1, Bp, 1, R_LOW), F32),
            jnp.zeros((1, Bp, R_HEADS, R_HEAD_DIM, R_HEAD_DIM), F32),
            jnp.zeros((1, Bp, 1, S5_FLAT), F32), jnp.zeros((1, Bp, 1, S5_FLAT), F32),
            jnp.zeros((1, Bp, H_HEADS, H_HEAD_DIM, H_HEAD_DIM), F32),
            jnp.zeros((1, Bp, G_HEADS, G_KEY_DIM, G_VAL_DIM), F32),
            jnp.zeros((1, Bp, CONV_W - 1, FF_PAD), F32))
    buf_s = jnp.pad(state_ffn_conv, ((0, 0), (0, 0), (SAMPLE_PAD - (CONV_W - 1), Ss), (0, FF_PAD - D_FF)))
    st_s = (state_rwkv_shift[:, :, None, 0:3 * W], state_rwkv_shift[:, :, None, 3 * W:], state_rwkv,
            state_s5_re.reshape(L, Bs, 1, S5_FLAT), state_s5_im.reshape(L, Bs, 1, S5_FLAT),
            state_hgrn, state_gla, buf_s.reshape(L, Bs * Ts, FF_PAD))

    def acc_init(B):
        return (jnp.zeros((L, B, R_HEADS, R_HEAD_DIM, R_HEAD_DIM), F32),
                jnp.zeros((L, B, H_HEADS, H_HEAD_DIM, H_HEAD_DIM), F32),
                jnp.zeros((L, B, G_HEADS, G_KEY_DIM, G_VAL_DIM), F32))

    acc_p, acc_s = acc_init(Bp), acc_init(Bs)
    outs_p, outs_s = [], []
    for l in range(L):
        xp, new_p, acc_p = _layer(xp, st_p, acc_p, 0, prm, l, cfg_p)
        xs, new_s, acc_s = _layer(xs, st_s, acc_s, l, prm, l, cfg_s)
        outs_p.append(_state_outputs(new_p, cfg_p))
        outs_s.append(_state_outputs(new_s, cfg_s))

    g_fin = norm_final.reshape(1, D).astype(F32)
    y_prompt = _final_norm_skip(xp.reshape(Bp, Tp, D), g_fin, N_META, 512)
    y_sample = _final_norm(xs, g_fin, cfg_s.tm).reshape(Bs, Ts, D)[:, SAMPLE_PAD:]
    sp = [jnp.stack([o[i] for o in outs_p]) for i in range(4)]
    ss = [jnp.stack([o[i] for o in outs_s]) for i in range(4)]
    return (y_prompt, y_sample, acc_p[0], acc_s[0], sp[0], ss[0], sp[1], ss[1], sp[2], ss[2],
            acc_p[1], acc_s[1], acc_p[2], acc_s[2], sp[3], ss[3])
```

```python
import functools
import math
from typing import NamedTuple

import jax
import jax.numpy as jnp
from jax import lax
from jax.experimental import pallas as pl
from jax.experimental.pallas import tpu as pltpu

F32 = jnp.float32
BF16 = jnp.bfloat16

D_MODEL = 2048
DEPTH = 4
N_META = 16
GROUP_WIDTH = D_MODEL // 4
EPS = 1e-6
R_HEAD_DIM = 64
R_HEADS = GROUP_WIDTH // R_HEAD_DIM
R_WIDTH = R_HEADS * R_HEAD_DIM
R_DECAY_RANK = 64
R_ICL_RANK = 64
R_GATE_RANK = 128
R_LOW = R_DECAY_RANK + R_ICL_RANK + R_GATE_RANK
R_COLS = 3 * R_WIDTH + R_LOW
RWKV_DECAY_SCALE = 0.606531
RWKV_GN_EPS = 64e-5
S5_GROUP_CH = 16
S5_GROUPS = GROUP_WIDTH // S5_GROUP_CH
S5_WIDTH = S5_GROUPS * S5_GROUP_CH
S5_STATE = 64
S5_FLAT = S5_GROUPS * S5_STATE
H_HEAD_DIM = 128
H_HEADS = GROUP_WIDTH // H_HEAD_DIM
H_WIDTH = H_HEADS * H_HEAD_DIM
HGRN_MAX_INPUT = 1.0 - 1e-4
G_VAL_DIM = 128
G_HEADS = GROUP_WIDTH // G_VAL_DIM
G_KEY_DIM = G_VAL_DIM // 2
G_WIDTH = G_HEADS * G_VAL_DIM
G_QK = G_HEADS * G_KEY_DIM
G_GATE_RANK = 16
GLA_GATE_NORM = 16.0
GLA_SAFE_DECAY = 80.0
D_FF = ((8 * D_MODEL // 3 + 127) // 128) * 128
CONV_W = 3

P_R = 0
P_S5 = 3 * R_WIDTH
P_H = P_S5 + S5_WIDTH
P_GV = P_H + 4 * H_WIDTH
P_GG = P_GV + G_WIDTH
G_KEY_PAD = G_VAL_DIM
G_QK_PAD = G_HEADS * G_KEY_PAD
P_GQ = P_GG + G_WIDTH
P_GK = P_GQ + G_QK_PAD
P_RL = P_GK + G_QK_PAD
P_GL = P_RL + R_LOW
GL_PAD = 256
P_COLS = P_GL + GL_PAD
FF_PAD = 5632
S5_CHUNKS = 4
S5_SCAN = 8
FFN_COL_BLOCK = 256

VMEM_LIMIT = 56 * 1024 * 1024

NT = (((1,), (1,)), ((), ()))
TN = (((0,), (0,)), ((), ()))


class _Cfg(NamedTuple):
    B: int
    T: int
    t0: int
    tm: int
    g_mix: int
    u_rwkv: int
    c_rwkv: int
    cb_gla: int
    c_gla: int
    s5_split: bool
    g_s5: int
    c_s5: int
    inject: bool


def _dot(a, b):
    return jnp.dot(a, b, preferred_element_type=F32)


def _dg(a, b, dims):
    return lax.dot_general(a, b, dims, preferred_element_type=F32)


def _gelu(x):
    c = math.sqrt(2.0 / math.pi)
    return x * (0.5 * (1.0 + jnp.tanh(c * (x + 0.044715 * (x * x * x)))))


def _tloc(G, C, W):
    t = lax.broadcasted_iota(jnp.int32, (C, W), 0)
    return t if G == 1 else jnp.concatenate([t] * G, axis=0)


def _cumsum_rows(x, tl, C):
    sh = 1
    while sh < C:
        x = x + jnp.where(tl >= sh, pltpu.roll(x, sh, 0), 0.0)
        sh *= 2
    return x


def _segsum(x, e):
    x1 = x.astype(BF16)
    x2 = (x - x1.astype(F32)).astype(BF16)
    return _dot(x1, e) + _dot(x2, e)


def _params(sem):
    return pltpu.CompilerParams(dimension_semantics=sem, vmem_limit_bytes=VMEM_LIMIT)


def _norm_matmul_kernel(x_ref, g_ref, w_ref, o_ref, h_scr):
    @pl.when(pl.program_id(1) == 0)
    def _():
        x = x_ref[...]
        r = lax.rsqrt(jnp.mean(x * x, axis=-1, keepdims=True) + EPS)
        h_scr[...] = (x * r * g_ref[...]).astype(BF16)

    o_ref[...] = _dg(h_scr[...], w_ref[...], NT)


def _norm_matmul(x, g_all, w_all, l, tm, tn):
    M, D = x.shape
    N = w_all.shape[1]
    return pl.pallas_call(
        _norm_matmul_kernel,
        grid=(M // tm, N // tn),
        in_specs=[
            pl.BlockSpec((tm, D), lambda i, j: (i, 0)),
            pl.BlockSpec((None, 1, D), lambda i, j: (l, 0, 0)),
            pl.BlockSpec((None, tn, D), lambda i, j: (l, j, 0)),
        ],
        out_specs=pl.BlockSpec((tm, tn), lambda i, j: (i, j)),
        out_shape=jax.ShapeDtypeStruct((M, N), F32),
        scratch_shapes=[pltpu.VMEM((tm, D), BF16)],
        compiler_params=_params(("arbitrary", "arbitrary")),
        name="in_proj",
    )(x, g_all, w_all)


def _out_proj_kernel(x_ref, y0, y1, y2, y3, w0, w1, w2, w3, o_ref):
    acc = _dot(y0[...].astype(BF16), w0[...])
    acc = acc + _dot(y1[...].astype(BF16), w1[...])
    acc = acc + _dot(y2[...].astype(BF16), w2[...])
    acc = acc + _dot(y3[...].astype(BF16), w3[...])
    o_ref[...] = x_ref[...] + acc


def _out_proj(x, ys, w_all, l, tm, tn):
    M, D = x.shape
    W = GROUP_WIDTH
    yspec = pl.BlockSpec((tm, W), lambda i, j: (i, 0))
    wspecs = [pl.BlockSpec((None, W, tn), functools.partial(lambda i, j, q: (l, q, j), q=q)) for q in range(4)]
    return pl.pallas_call(
        _out_proj_kernel,
        grid=(M // tm, D // tn),
        in_specs=[pl.BlockSpec((tm, tn), lambda i, j: (i, j))] + [yspec] * 4 + wspecs,
        out_specs=pl.BlockSpec((tm, tn), lambda i, j: (i, j)),
        out_shape=jax.ShapeDtypeStruct((M, D), F32),
        compiler_params=_params(("arbitrary", "arbitrary")),
        name="out_proj",
    )(x, *ys, w_all, w_all, w_all, w_all)


def _ffn_up_kernel(x_ref, g_ref, wu_ref, wg_ref, cw_ref, cb_ref, buf_ref, act_ref, ust_ref, h_scr, *car,
                   tm, tn, tiles_per_seq, t0, seq_rows, inject):
    i = pl.program_id(0)
    j = pl.program_id(1)

    @pl.when(j == 0)
    def _():
        x = x_ref[...]
        r = lax.rsqrt(jnp.mean(x * x, axis=-1, keepdims=True) + EPS)
        h_scr[...] = (x * r * g_ref[...]).astype(BF16)

    if not inject:
        (car_scr,) = car

        @pl.when(i % tiles_per_seq == 0)
        def _():
            car_scr[j, 0:2, :] = buf_ref[...]

    nb = FFN_COL_BLOCK
    blocks = [slice(nb * q, nb * (q + 1)) for q in range(tn // nb)]
    h = h_scr[...]
    us = [_dot(h, wu_ref[:, cs]) for cs in blocks]
    gts = [_dot(h, wg_ref[:, cs]) for cs in blocks]
    rows = lax.broadcasted_iota(jnp.int32, (tm, nb), 0)
    for cs, u, gt in zip(blocks, us, gts):
        if inject:
            tl = rows & (seq_rows - 1)
            u = jnp.where((tl == t0 - 2) | (tl == t0 - 1), buf_ref[:, cs], u)
            p1 = pltpu.roll(u, 1, 0)
            p2 = pltpu.roll(u, 2, 0)
            ust_ref[:, cs] = u
        else:
            c0 = car_scr[j, 0:1, cs]
            c1 = car_scr[j, 1:2, cs]
            p1 = jnp.where(rows == 0, c1, pltpu.roll(u, 1, 0))
            p2 = jnp.where(rows == 0, c0, jnp.where(rows == 1, c1, pltpu.roll(u, 2, 0)))
            last = u[tm - 2:tm, :]
            car_scr[j, 0:2, cs] = last
            ust_ref[:, cs] = last
        c = cb_ref[:, cs] + cw_ref[0:1, cs] * p2
        c = c + cw_ref[1:2, cs] * p1
        c = c + cw_ref[2:3, cs] * u
        act_ref[:, cs] = (_gelu(c) * gt).astype(BF16)


def _ffn_up(x, g_all, wu_all, wg_all, cw_all, cb_all, buf, l, l_in, cfg, tn):
    M, D = x.shape
    tm = cfg.tm
    N = wu_all.shape[-1]
    tps = cfg.T // tm if not cfg.inject else 1
    kern = functools.partial(_ffn_up_kernel, tm=tm, tn=tn, tiles_per_seq=tps, t0=cfg.t0, seq_rows=cfg.T,
                             inject=cfg.inject)
    if cfg.inject:
        buf_spec = pl.BlockSpec((None, tm, tn), lambda i, j: (l_in, i, j))
        ust_spec = pl.BlockSpec((tm, tn), lambda i, j: (i, j))
        ust_shape = jax.ShapeDtypeStruct((M, N), F32)
        scratch = [pltpu.VMEM((tm, D), BF16)]
    else:
        buf_spec = pl.BlockSpec((None, None, CONV_W - 1, tn), lambda i, j: (l_in, i // tps, 0, j))
        ust_spec = pl.BlockSpec((None, CONV_W - 1, tn), lambda i, j: (i, 0, j))
        ust_shape = jax.ShapeDtypeStruct((M // tm, CONV_W - 1, N), F32)
        scratch = [pltpu.VMEM((tm, D), BF16), pltpu.VMEM((N // tn, 8, tn), F32)]
    return pl.pallas_call(
        kern,
        grid=(M // tm, N // tn),
        in_specs=[
            pl.BlockSpec((tm, D), lambda i, j: (i, 0)),
            pl.BlockSpec((None, 1, D), lambda i, j: (l, 0, 0)),
            pl.BlockSpec((None, D, tn), lambda i, j: (l, 0, j)),
            pl.BlockSpec((None, D, tn), lambda i, j: (l, 0, j)),
            pl.BlockSpec((None, CONV_W, tn), lambda i, j: (l, 0, j)),
            pl.BlockSpec((None, 1, tn), lambda i, j: (l, 0, j)),
            buf_spec,
        ],
        out_specs=[pl.BlockSpec((tm, tn), lambda i, j: (i, j)), ust_spec],
        out_shape=[jax.ShapeDtypeStruct((M, N), BF16), ust_shape],
        scratch_shapes=scratch,
        compiler_params=_params(("arbitrary", "arbitrary")),
        name="ffn_up",
    )(x, g_all, wu_all, wg_all, cw_all, cb_all, buf)


def _ffn_down_kernel(x_ref, act_ref, w_ref, o_ref):
    o_ref[...] = x_ref[...] + _dot(act_ref[...], w_ref[...])


def _ffn_down(x, act, w_all, l, tm, tn):
    M, D = x.shape
    K = w_all.shape[1]
    return pl.pallas_call(
        _ffn_down_kernel,
        grid=(M // tm, D // tn),
        in_specs=[
            pl.BlockSpec((tm, tn), lambda i, j: (i, j)),
            pl.BlockSpec((tm, K), lambda i, j: (i, 0)),
            pl.BlockSpec((None, K, tn), lambda i, j: (l, 0, j)),
        ],
        out_specs=pl.BlockSpec((tm, tn), lambda i, j: (i, j)),
        out_shape=jax.ShapeDtypeStruct((M, D), F32),
        compiler_params=_params(("arbitrary", "arbitrary")),
        name="ffn_down",
    )(x, act, w_all)


def _final_norm_kernel(x_ref, g_ref, o_ref):
    x = x_ref[...]
    r = lax.rsqrt(jnp.mean(x * x, axis=-1, keepdims=True) + EPS)
    o_ref[...] = x * r * g_ref[...]


def _final_norm_skip_kernel(x_ref, g_ref, o_ref):
    x = x_ref[0]
    r = lax.rsqrt(jnp.mean(x * x, axis=-1, keepdims=True) + EPS)
    o_ref[0] = x * r * g_ref[...]


def _final_norm_skip(x3, g, skip, tr):
    B, T, D = x3.shape
    assert skip % 8 == 0 and tr % 8 == 0 and (T - skip) % tr == 0
    return pl.pallas_call(
        _final_norm_skip_kernel,
        grid=(B, (T - skip) // tr),
        in_specs=[pl.BlockSpec((pl.Element(1), pl.Element(tr), pl.Element(D)),
                               lambda b, i: (b, (skip // 8 + i * (tr // 8)) * 8, 0)),
                  pl.BlockSpec((1, D), lambda b, i: (0, 0))],
        out_specs=pl.BlockSpec((1, tr, D), lambda b, i: (b, i, 0)),
        out_shape=jax.ShapeDtypeStruct((B, T - skip, D), F32),
        compiler_params=_params(("arbitrary", "arbitrary")),
        name="final_norm_skip",
    )(x3, g)


def _final_norm(x, g, tm):
    M, D = x.shape
    return pl.pallas_call(
        _final_norm_kernel,
        grid=(M // tm,),
        in_specs=[pl.BlockSpec((tm, D), lambda i: (i, 0)), pl.BlockSpec((1, D), lambda i: (0, 0))],
        out_specs=pl.BlockSpec((tm, D), lambda i: (i, 0)),
        out_shape=jax.ShapeDtypeStruct((M, D), F32),
        compiler_params=_params(("arbitrary",)),
        name="final_norm",
    )(x, g)


def _rwkv_kernel(prkv_ref, plow_ref, shr_in, shl_in, s0_ref, mur_ref, mul_ref, w0_ref, wup_ref, a0_ref, aup_ref,
                 gup_ref, kk_ref, ka_ref, rk_ref, ln_ref, e_ref, acc_ref,
                 y_ref, sout_ref, shr_out, shl_out,
                 at_scr, rt_scr, kt_scr, bt_scr, v_scr, gam_scr, y_scr, *, G, C, t0, U):
    R = G * C
    W = R_WIDTH

    @pl.when(pl.program_id(1) == 0)
    def _():
        sout_ref[...] = s0_ref[...]
        shr_out[...] = shr_in[...]
        shl_out[...] = shl_in[...]

    def token_shift(p_ref, car_ref, mu_ref, w):
        p = p_ref[...].reshape(R, w)
        tl = _tloc(G, C, w)
        carry = jnp.concatenate([jnp.broadcast_to(car_ref[g], (C, w)) for g in range(G)], axis=0)
        prev = jnp.where(tl == t0, carry, pltpu.roll(p, 1, 0))
        for g in range(G):
            car_ref[g] = p[g * C + C - 1:g * C + C, :]
        return p + (prev - p) * mu_ref[...]

    ps = token_shift(prkv_ref, shr_out, mur_ref, 3 * W)
    lo = token_shift(plow_ref, shl_out, mul_ref, R_LOW)
    r = ps[:, 0:W]
    k = ps[:, W:2 * W]
    v = ps[:, 2 * W:3 * W]
    lw = -RWKV_DECAY_SCALE * jax.nn.sigmoid(w0_ref[...] + _dot(jnp.tanh(lo).astype(BF16), wup_ref[...]))
    a = jax.nn.sigmoid(a0_ref[...] + _dot(lo.astype(BF16), aup_ref[...]))
    gate = _dot(jax.nn.sigmoid(lo).astype(BF16), gup_ref[...])
    e = e_ref[...]
    kk = k * kk_ref[...]
    kk = kk / jnp.maximum(jnp.sqrt(_segsum(kk * kk, e)), 1e-12)
    k2 = k * (1.0 + (a - 1.0) * ka_ref[...])
    ka = kk * a
    tl = _tloc(G, C, W)
    if t0 > 0:
        valid = tl >= t0
        lw = jnp.where(valid, lw, 0.0)
        kk = jnp.where(valid, kk, 0.0)
        ka = jnp.where(valid, ka, 0.0)
        k2 = jnp.where(valid, k2, 0.0)
        v = jnp.where(valid, v, 0.0)
    cum = _cumsum_rows(lw, tl, C)
    gam = jnp.exp(cum)
    inv = jnp.exp(-cum)
    at_scr[...] = kk * jnp.exp(cum - lw)
    rt_scr[...] = r * gam
    kt_scr[...] = k2 * inv
    bt_scr[...] = ka * inv
    v_scr[...] = v
    gam_scr[...] = gam

    row = lax.broadcasted_iota(jnp.int32, (C, C), 0)
    col = lax.broadcasted_iota(jnp.int32, (C, C), 1)
    low_s = row > col
    low_i = row >= col

    def seq_body(it, carry):
        gs = [it * U + i for i in range(U)]
        lanes = [(i, h) for i in range(U) for h in range(R_HEADS)]
        data = []
        for g in gs:
            off = pl.multiple_of(g * C, 8)
            rows = pl.ds(off, C)
            data.append((at_scr[rows, :], rt_scr[rows, :], kt_scr[rows, :], bt_scr[rows, :], v_scr[rows, :],
                         gam_scr[pl.ds(off + C - 8, 8), :][7:8, :]))
        s_in = [sout_ref[gs[i], h] for i, h in lanes]

        def head(x, h):
            return x[:, R_HEAD_DIM * h:R_HEAD_DIM * (h + 1)]

        ar = [jnp.concatenate([head(data[i][0], h), head(data[i][1], h)], axis=0).astype(BF16) for i, h in lanes]
        ktb = [head(data[i][2], h).astype(BF16) for i, h in lanes]
        btb = [head(data[i][3], h).astype(BF16) for i, h in lanes]
        vb = [head(data[i][4], h).astype(BF16) for i, h in lanes]
        n = range(len(lanes))
        m_k = [_dg(ar[j], ktb[j], NT) for j in n]
        m_b = [_dg(ar[j], btb[j], NT) for j in n]
        n_pow = [jnp.where(low_s, m_b[j][0:C], 0.0) for j in n]
        acc = [-n_pow[j] for j in n]
        pre_rhs = [_dot(jnp.where(low_s, m_k[j][0:C], 0.0).astype(BF16), vb[j]) for j in n]
        pre_y = [_dot(jnp.where(low_i, m_k[j][C:2 * C], 0.0).astype(BF16), vb[j]) for j in n]
        pre_ds = [_dg(vb[j], ktb[j], TN) for j in n]
        trb = [jnp.where(low_i, m_b[j][C:2 * C], 0.0).astype(BF16) for j in n]
        span = 2
        nb = [n_pow[j].astype(BF16) for j in n]
        n_pow = [_dot(nb[j], nb[j]) for j in n]
        while span < C:
            nb = [n_pow[j].astype(BF16) for j in n]
            span *= 2
            prod = [_dot(acc[j].astype(BF16), nb[j]) for j in n]
            if span < C:
                nxt = [_dot(nb[j], nb[j]) for j in n]
            acc = [acc[j] + n_pow[j] + prod[j] for j in n]
            if span < C:
                n_pow = nxt
        accb = [acc[j].astype(BF16) for j in n]
        m_s = [_dg(ar[j], s_in[j].astype(BF16), NT) for j in n]
        rhs = [m_s[j][0:C] + pre_rhs[j] for j in n]
        u = [rhs[j] + _dot(accb[j], rhs[j].astype(BF16)) for j in n]
        ub = [u[j].astype(BF16) for j in n]
        yh = [m_s[j][C:2 * C] + pre_y[j] - _dot(trb[j], ub[j]) for j in n]
        ds = [pre_ds[j] - _dg(ub[j], btb[j], TN) for j in n]
        for j, (i, h) in enumerate(lanes):
            sout_ref[gs[i], h] = (s_in[j] + ds[j]) * head(data[i][5], h)
        for i, g in enumerate(gs):
            rows = pl.ds(pl.multiple_of(g * C, 8), C)
            y_scr[rows, :] = jnp.concatenate(yh[i * R_HEADS:(i + 1) * R_HEADS], axis=1)
        return carry

    lax.fori_loop(0, G // U, seq_body, 0)

    y = y_scr[...]
    inv_n = 1.0 / R_HEAD_DIM
    d = y - _segsum(y, e) * inv_n
    var = _segsum(d * d, e) * inv_n
    y = d * lax.rsqrt(var + RWKV_GN_EPS) * ln_ref[...]
    y = y + _segsum(r * k2 * rk_ref[...], e) * v
    y_ref[...] = (y * gate).reshape(G, C, W)


def _rwkv(p3, st, acc, l_in, prm, l, cfg):
    B, T, _ = p3.shape
    G, C = cfg.g_mix, cfg.c_rwkv
    W = R_WIDTH
    shr_in, shl_in, s0 = st

    def pspec(w, off):
        return pl.BlockSpec((G, C, w), lambda i, c: (i, c, off // w))

    def lay(*shape):
        return pl.BlockSpec((None,) + shape, lambda i, c: (l,) + (0,) * len(shape))

    kern = functools.partial(_rwkv_kernel, G=G, C=C, t0=cfg.t0, U=cfg.u_rwkv)
    R = G * C
    return pl.pallas_call(
        kern,
        grid=(B // G, T // C),
        in_specs=[
            pspec(3 * W, P_R), pspec(R_LOW, P_RL),
            pl.BlockSpec((None, G, 1, 3 * W), lambda i, c: (l_in, i, 0, 0)),
            pl.BlockSpec((None, G, 1, R_LOW), lambda i, c: (l_in, i, 0, 0)),
            pl.BlockSpec((None, G, R_HEADS, R_HEAD_DIM, R_HEAD_DIM), lambda i, c: (l_in, i, 0, 0, 0)),
            lay(1, 3 * W), lay(1, R_LOW), lay(1, W), lay(R_LOW, W), lay(1, W), lay(R_LOW, W), lay(R_LOW, W),
            lay(1, W), lay(1, W), lay(1, W), lay(1, W),
            pl.BlockSpec((W, W), lambda i, c: (0, 0)),
            pl.BlockSpec(memory_space=pl.ANY),
        ],
        input_output_aliases={17: 1},
        out_specs=[
            pl.BlockSpec((G, C, W), lambda i, c: (i, c, 0)),
            pl.BlockSpec((None, G, R_HEADS, R_HEAD_DIM, R_HEAD_DIM), lambda i, c: (l, i, 0, 0, 0)),
            pl.BlockSpec((G, 1, 3 * W), lambda i, c: (i, 0, 0)),
            pl.BlockSpec((G, 1, R_LOW), lambda i, c: (i, 0, 0)),
        ],
        out_shape=[
            jax.ShapeDtypeStruct((B, T, W), F32),
            jax.ShapeDtypeStruct(acc.shape, F32),
            jax.ShapeDtypeStruct((B, 1, 3 * W), F32),
            jax.ShapeDtypeStruct((B, 1, R_LOW), F32),
        ],
        scratch_shapes=[pltpu.VMEM((R, W), F32)] * 7,
        compiler_params=_params(("arbitrary", "arbitrary")),
        name="rwkv7",
    )(p3, p3, shr_in, shl_in, s0, prm["mu_rkv"], prm["mu_low"], prm["rwkv_w0"], prm["rwkv_w_up"], prm["rwkv_a0"],
      prm["rwkv_a_up"], prm["rwkv_g_up"], prm["rwkv_k_k"], prm["rwkv_k_a"], prm["rwkv_r_k"], prm["rwkv_ln"],
      prm["e_head"], acc)


def _s5_kernel(u_ref, h0r_ref, h0i_ref, bbhr_ref, bblr_ref, bbhi_ref, bbli_ref, ccr_ref, cci_ref, ap_ref, pw_ref,
               d_ref, wglu_ref, bglu_ref,
               y_ref, hr_out, hi_out, hre_scr, him_scr, *, G, C, t0, split):
    R = G * C
    NB = S5_FLAT // S5_CHUNKS
    UB = S5_WIDTH // S5_CHUNKS

    @pl.when(pl.program_id(1) == 0)
    def _():
        hr_out[...] = h0r_ref[...]
        hi_out[...] = h0i_ref[...]

    u = u_ref[...].reshape(R, S5_WIDTH)
    uh = u.astype(BF16)
    ul = (u - uh.astype(F32)).astype(BF16)
    valid = _tloc(G, C, NB) >= t0
    for j in range(S5_CHUNKS):
        us = slice(UB * j, UB * (j + 1))
        br = _dot(uh[:, us], bbhr_ref[j])
        bi = _dot(uh[:, us], bbhi_ref[j])
        if split:
            br = br + _dot(ul[:, us], bbhr_ref[j]) + _dot(uh[:, us], bblr_ref[j])
            bi = bi + _dot(ul[:, us], bbhi_ref[j]) + _dot(uh[:, us], bbli_ref[j])
        if t0 > 0:
            br = jnp.where(valid, br, 0.0)
            bi = jnp.where(valid, bi, 0.0)
        hre_scr[:, NB * j:NB * (j + 1)] = br
        him_scr[:, NB * j:NB * (j + 1)] = bi

    steps = [(1 << k, ap_ref[S5_SCAN * 2 * k:S5_SCAN * (2 * k + 1), :],
              ap_ref[S5_SCAN * (2 * k + 1):S5_SCAN * (2 * k + 2), :]) for k in range(3)]
    pwr = pw_ref[0:S5_SCAN, :]
    pwi = pw_ref[S5_SCAN:2 * S5_SCAN, :]

    def seq_body(g, carry):
        def tile_body(i, hc):
            hcr, hci = hc
            rows = pl.ds(pl.multiple_of(g * C + i * S5_SCAN, S5_SCAN), S5_SCAN)
            xr = hre_scr[rows, :]
            xi = him_scr[rows, :]
            for sh, ar, ai in steps:
                sr = pltpu.roll(xr, sh, 0)
                si = pltpu.roll(xi, sh, 0)
                xr, xi = xr + ar * sr - ai * si, xi + ar * si + ai * sr
            xr, xi = xr + pwr * hcr - pwi * hci, xi + pwr * hci + pwi * hcr
            hre_scr[rows, :] = xr
            him_scr[rows, :] = xi
            return xr[S5_SCAN - 1:S5_SCAN, :], xi[S5_SCAN - 1:S5_SCAN, :]

        hcr, hci = lax.fori_loop(0, C // S5_SCAN, tile_body, (hr_out[g], hi_out[g]))
        hr_out[g] = hcr
        hi_out[g] = hci
        return carry

    lax.fori_loop(0, G, seq_body, 0)

    ys = []
    for j in range(S5_CHUNKS):
        hs = slice(NB * j, NB * (j + 1))
        ys.append(_dot(hre_scr[:, hs].astype(BF16), ccr_ref[j]) - _dot(him_scr[:, hs].astype(BF16), cci_ref[j]))
    y = jnp.concatenate(ys, axis=1) + d_ref[...] * u
    y = _gelu(y)
    y = y * jax.nn.sigmoid(_dot(y.astype(BF16), wglu_ref[...]) + bglu_ref[...])
    y_ref[...] = y.reshape(G, C, S5_WIDTH)


def _s5(p3, st, l_in, prm, l, cfg):
    B, T, _ = p3.shape
    G, C = cfg.g_s5, cfg.c_s5
    h0r, h0i = st
    NB = S5_FLAT // S5_CHUNKS
    UB = S5_WIDTH // S5_CHUNKS

    def lay(*shape):
        return pl.BlockSpec((None,) + shape, lambda i, c: (l,) + (0,) * len(shape))

    st_spec = pl.BlockSpec((None, G, 1, S5_FLAT), lambda i, c: (l_in, i, 0, 0))
    out_st = pl.BlockSpec((G, 1, S5_FLAT), lambda i, c: (i, 0, 0))
    kern = functools.partial(_s5_kernel, G=G, C=C, t0=cfg.t0, split=cfg.s5_split)
    pw = prm["s5_pw"][cfg.t0]
    return pl.pallas_call(
        kern,
        grid=(B // G, T // C),
        in_specs=[
            pl.BlockSpec((G, C, S5_WIDTH), lambda i, c: (i, c, P_S5 // S5_WIDTH)),
            st_spec, st_spec,
            lay(S5_CHUNKS, UB, NB), lay(S5_CHUNKS, UB, NB), lay(S5_CHUNKS, UB, NB), lay(S5_CHUNKS, UB, NB),
            lay(S5_CHUNKS, NB, UB), lay(S5_CHUNKS, NB, UB),
            lay(6 * S5_SCAN, S5_FLAT), lay(2 * S5_SCAN, S5_FLAT),
            lay(1, S5_WIDTH), lay(S5_WIDTH, S5_WIDTH), lay(1, S5_WIDTH),
        ],
        out_specs=[pl.BlockSpec((G, C, S5_WIDTH), lambda i, c: (i, c, 0)), out_st, out_st],
        out_shape=[
            jax.ShapeDtypeStruct((B, T, S5_WIDTH), F32),
            jax.ShapeDtypeStruct((B, 1, S5_FLAT), F32),
            jax.ShapeDtypeStruct((B, 1, S5_FLAT), F32),
        ],
        scratch_shapes=[pltpu.VMEM((G * C, S5_FLAT), F32)] * 2,
        compiler_params=_params(("arbitrary", "arbitrary")),
        name="s5",
    )(p3, h0r, h0i, prm["s5_bbh_re"], prm["s5_bbl_re"], prm["s5_bbh_im"], prm["s5_bbl_im"], prm["s5_cc_re"],
      prm["s5_cc_im"], prm["s5_apow"], pw, prm["s5_D"], prm["s5_w_glu"], prm["s5_b_glu"])


def _gla_core(q, kx, gdec, v, s0_ref, sout_ref, st_scr, q_scr, k_scr, b_scr, v_scr, o_scr, *, G, CB, C, t0, K, KS):
    assert CB % C == 0 and (CB == C or C & (C - 1) == 0) and (t0 == 0 or CB == C)
    H = G_HEADS
    V = G_VAL_DIM
    HK = H * K
    c = pl.program_id(1)

    @pl.when(c == 0)
    def _():
        for g in range(G):
            for h in range(H):
                s = s0_ref[g, h].T
                if KS < K:
                    s = jnp.concatenate([s, jnp.zeros((V, K - KS), F32)], axis=1)
                st_scr[g, h] = s

    if CB == C:
        tl = _tloc(G, C, HK)
    else:
        tl = lax.broadcasted_iota(jnp.int32, (G * CB, HK), 0) & (C - 1)
    if t0 > 0:
        valid = tl >= t0
        gdec = jnp.where(valid, gdec, 0.0)
        kx = jnp.where(valid, kx, 0.0)
    q_scr[...] = q
    k_scr[...] = kx
    b_scr[...] = _cumsum_rows(gdec, tl, C)
    v_scr[...] = v
    ti = lax.broadcasted_iota(jnp.int32, (C, HK), 0)

    row = lax.broadcasted_iota(jnp.int32, (C, C), 0)
    col = lax.broadcasted_iota(jnp.int32, (C, C), 1)
    lanes = [(g, h) for g in range(G) for h in range(H)]

    def ksl(h):
        return slice(K * h, K * (h + 1))

    def vsl(h):
        return slice(V * h, V * (h + 1))

    def chunk_body(j, carry):
        rows = [pl.ds(pl.multiple_of(g * CB + j * C, 8), C) for g in range(G)]
        qg = [q_scr[r, :] for r in rows]
        kg = [k_scr[r, :] for r in rows]
        bg = [b_scr[r, :] for r in rows]
        vg = [v_scr[r, :] for r in rows]
        vb = [x.astype(BF16) for x in vg]
        b_last = [b[C - 1:C, :] for b in bg]
        qe = [(qg[g] * jnp.exp(bg[g])).astype(BF16) for g in range(G)]
        kd = [(kg[g] * jnp.exp(b_last[g] - bg[g])).astype(BF16) for g in range(G)]
        dec_last = [jnp.exp(b) for b in b_last]
        s_in = [st_scr[g, h] for g, h in lanes]
        inter = [_dg(qe[g][:, ksl(h)], s_in[n].astype(BF16), NT) for n, (g, h) in enumerate(lanes)]
        upd = [_dg(vb[g][:, vsl(h)], kd[g][:, ksl(h)], TN) for g, h in lanes]
        b_min = b_last[0]
        for b in b_last[1:]:
            b_min = jnp.minimum(b_min, b)

        def factored():
            kn = [(kg[g] * jnp.exp(-bg[g])).astype(BF16) for g in range(G)]
            att = [_dg(qe[g][:, ksl(h)], kn[g][:, ksl(h)], NT) for g, h in lanes]
            att = [jnp.where(row >= col, a, 0.0).astype(BF16) for a in att]
            outs = [_dot(att[n], vb[g][:, vsl(h)]) for n, (g, h) in enumerate(lanes)]
            return [jnp.concatenate(outs[g * H:(g + 1) * H], axis=1) for g in range(G)]

        def pairwise():
            res = []
            for g in range(G):
                base = g * CB + j * C

                def key_row(jj, intra, g=g, base=base):
                    kj = k_scr[pl.ds(base + jj, 1), :]
                    bj = b_scr[pl.ds(base + jj, 1), :]
                    vj = v_scr[pl.ds(base + jj, 1), :]
                    causal = ti >= jj
                    e = jnp.where(causal, qg[g] * kj * jnp.exp(jnp.where(causal, bg[g] - bj, 0.0)), 0.0)
                    return tuple(intra[h] + jnp.sum(e[:, ksl(h)], axis=-1, keepdims=True) * vj[:, vsl(h)]
                                 for h in range(H))

                intra = lax.fori_loop(0, C, key_row, tuple(jnp.zeros((C, V), F32) for _ in range(H)))
                res.append(jnp.concatenate(intra, axis=1))
            return res

        intra = lax.cond(jnp.min(b_min) >= -GLA_SAFE_DECAY, factored, pairwise)
        for n, (g, h) in enumerate(lanes):
            st_scr[g, h] = s_in[n] * dec_last[g][:, ksl(h)] + upd[n]
        for g in range(G):
            o_scr[rows[g], :] = jnp.concatenate(inter[g * H:(g + 1) * H], axis=1) + intra[g]
        return carry

    lax.fori_loop(0, CB // C, chunk_body, 0)

    @pl.when(c == pl.num_programs(1) - 1)
    def _():
        for g in range(G):
            for h in range(H):
                sout_ref[g, h] = st_scr[g, h][:, 0:KS].T


def _head_norm_gate(o, norm, gate_raw):
    V = G_VAL_DIM
    ys = []
    for h in range(G_HEADS):
        oh = o[:, V * h:V * (h + 1)]
        ys.append(oh * lax.rsqrt(jnp.mean(oh * oh, axis=-1, keepdims=True) + EPS))
    y = jnp.concatenate(ys, axis=1) * norm
    return y * (gate_raw * jax.nn.sigmoid(gate_raw))


def _hgrn_kernel(q_ref, f_ref, i_ref, g_ref, lb_ref, norm_ref, s0_ref, acc_ref, y_ref, sout_ref,
                 st_scr, q_scr, k_scr, b_scr, v_scr, o_scr, *, G, C, chunk, t0, layer):
    R = G * C
    W = H_WIDTH
    lbr = lb_ref[...]
    ex = jnp.exp(lbr - jnp.max(lbr, axis=0, keepdims=True))
    sm = ex / jnp.sum(ex, axis=0, keepdims=True)
    lb = jnp.zeros((1, W), F32)
    for i in range(1, layer + 1):
        lb = lb + sm[i:i + 1, :]
    qr = q_ref[...].reshape(R, W)
    kx = jnp.minimum((1.0 - lb) * jax.nn.sigmoid(-f_ref[...].reshape(R, W)), HGRN_MAX_INPUT)
    _gla_core(qr * jax.nn.sigmoid(qr), kx, jnp.log1p(-kx), i_ref[...].reshape(R, W), s0_ref, sout_ref,
              st_scr, q_scr, k_scr, b_scr, v_scr, o_scr, G=G, CB=C, C=chunk, t0=t0, K=H_HEAD_DIM, KS=H_HEAD_DIM)
    y_ref[...] = _head_norm_gate(o_scr[...], norm_ref[...], g_ref[...].reshape(R, W)).reshape(G, C, W)


def _gla_kernel(v_ref, gate_ref, q_ref, k_ref, gl_ref, gkup_ref, gkb_ref, norm_ref, s0_ref, acc_ref, y_ref,
                sout_ref,
                st_scr, q_scr, k_scr, b_scr, v_scr, o_scr, *, G, C, chunk, t0):
    R = G * C
    z = _dot(gl_ref[...].reshape(R, GL_PAD).astype(BF16), gkup_ref[...]) + gkb_ref[...]
    gdec = -(jnp.maximum(-z, 0.0) + jnp.log1p(jnp.exp(-jnp.abs(z)))) / GLA_GATE_NORM
    _gla_core(q_ref[...].reshape(R, G_QK_PAD) * (G_KEY_DIM ** -0.5), k_ref[...].reshape(R, G_QK_PAD), gdec,
              v_ref[...].reshape(R, G_WIDTH), s0_ref, sout_ref, st_scr, q_scr, k_scr, b_scr, v_scr, o_scr,
              G=G, CB=C, C=chunk, t0=t0, K=G_KEY_PAD, KS=G_KEY_DIM)
    y_ref[...] = _head_norm_gate(o_scr[...], norm_ref[...], gate_ref[...].reshape(R, G_WIDTH)).reshape(G, C, G_WIDTH)


def _gla_like(p3, s0, acc, l_in, prm, l, cfg, hgrn):
    B, T, _ = p3.shape
    G, C = cfg.g_mix, cfg.cb_gla
    H, V = G_HEADS, G_VAL_DIM
    K = H_HEAD_DIM if hgrn else G_KEY_PAD
    KS = H_HEAD_DIM if hgrn else G_KEY_DIM
    HK = H * K
    R = G * C

    def pspec(w, off):
        return pl.BlockSpec((G, C, w), lambda i, c: (i, c, off // w))

    def lay(*shape):
        return pl.BlockSpec((None,) + shape, lambda i, c: (l,) + (0,) * len(shape))

    st_in = pl.BlockSpec((None, G, H, KS, V), lambda i, c: (l_in, i, 0, 0, 0))
    if hgrn:
        kern = functools.partial(_hgrn_kernel, G=G, C=C, chunk=cfg.c_gla, t0=cfg.t0, layer=l)
        in_specs = [pspec(HK, P_H), pspec(HK, P_H + HK), pspec(HK, P_H + 2 * HK), pspec(HK, P_H + 3 * HK),
                    pl.BlockSpec((DEPTH, HK), lambda i, c: (0, 0)), lay(1, HK), st_in]
        args = (p3, p3, p3, p3, prm["hgrn_lb_raw"], prm["hgrn_norm"], s0)
        name = "hgrn2"
    else:
        kern = functools.partial(_gla_kernel, G=G, C=C, chunk=cfg.c_gla, t0=cfg.t0)
        in_specs = [pspec(G_WIDTH, P_GV), pspec(G_WIDTH, P_GG), pspec(HK, P_GQ), pspec(HK, P_GK),
                    pspec(GL_PAD, P_GL), lay(GL_PAD, HK), lay(1, HK), lay(1, G_WIDTH), st_in]
        args = (p3, p3, p3, p3, p3, prm["gla_gk_up"], prm["gla_gk_b"], prm["gla_norm"], s0)
        name = "gla"
    return pl.pallas_call(
        kern,
        grid=(B // G, T // C),
        in_specs=in_specs + [pl.BlockSpec(memory_space=pl.ANY)],
        out_specs=[pl.BlockSpec((G, C, H * V), lambda i, c: (i, c, 0)),
                   pl.BlockSpec((None, G, H, KS, V), lambda i, c: (l, i, 0, 0, 0))],
        out_shape=[jax.ShapeDtypeStruct((B, T, H * V), F32), jax.ShapeDtypeStruct(acc.shape, F32)],
        input_output_aliases={len(args): 1},
        scratch_shapes=[pltpu.VMEM((G, H, V, K), F32), pltpu.VMEM((R, HK), F32), pltpu.VMEM((R, HK), F32),
                        pltpu.VMEM((R, HK), F32), pltpu.VMEM((R, H * V), F32), pltpu.VMEM((R, H * V), F32)],
        compiler_params=_params(("arbitrary", "arbitrary")),
        name=name,
    )(*args, acc)


def _s5_tables(A_re, A_im, log_dt, B_re, B_im, C_re, C_im, t0s):
    L = A_re.shape[0]
    A_re = A_re.astype(F32)
    A_im = A_im.astype(F32)
    dt = jnp.exp(log_dt.astype(F32))[..., None]
    mag = jnp.exp(A_re * dt)
    ab_re = mag * jnp.cos(A_im * dt)
    ab_im = mag * jnp.sin(A_im * dt)
    den = A_re * A_re + A_im * A_im
    n_re = ab_re - 1.0
    co_re = (n_re * A_re + ab_im * A_im) / den
    co_im = (ab_im * A_re - n_re * A_im) / den
    B_re = B_re.astype(F32)
    B_im = B_im.astype(F32)
    bb_re = co_re[..., None] * B_re - co_im[..., None] * B_im
    bb_im = co_re[..., None] * B_im + co_im[..., None] * B_re
    gpc = S5_GROUPS // S5_CHUNKS
    eye = jnp.eye(gpc, dtype=F32)

    def block_in(bb):
        t = bb.reshape(L, S5_CHUNKS, gpc, S5_STATE, S5_GROUP_CH).transpose(0, 1, 2, 4, 3)
        t = t[:, :, :, :, None, :] * eye[None, None, :, None, :, None]
        return t.reshape(L, S5_CHUNKS, gpc * S5_GROUP_CH, gpc * S5_STATE)

    def block_out(cc):
        t = cc.astype(F32).reshape(L, S5_CHUNKS, gpc, S5_GROUP_CH, S5_STATE).transpose(0, 1, 2, 4, 3)
        t = t[:, :, :, :, None, :] * eye[None, None, :, None, :, None]
        return t.reshape(L, S5_CHUNKS, gpc * S5_STATE, gpc * S5_GROUP_CH)

    def split(x):
        hi = x.astype(BF16)
        return hi, (x - hi.astype(F32)).astype(BF16)

    bbh_re, bbl_re = split(block_in(bb_re))
    bbh_im, bbl_im = split(block_in(bb_im))

    def power(m):
        mg = jnp.exp(m * dt * A_re)
        return (mg * jnp.cos(m * dt * A_im)).reshape(L, S5_FLAT), (mg * jnp.sin(m * dt * A_im)).reshape(L, S5_FLAT)

    zero = jnp.zeros((L, S5_FLAT), F32)
    levels = []
    for sh in (1, 2, 4):
        pr, pi = power(float(sh))
        levels += [zero if r < sh else pr for r in range(S5_SCAN)]
        levels += [zero if r < sh else pi for r in range(S5_SCAN)]
    apow = jnp.stack(levels, axis=1)
    pws = {}
    for t0 in t0s:
        first = t0 % S5_SCAN
        res, ims = [], []
        for r in range(S5_SCAN):
            if r < first:
                res.append(zero)
                ims.append(zero)
            else:
                pr, pi = power(float(r - first + 1))
                res.append(pr)
                ims.append(pi)
        pws[t0] = jnp.stack(res + ims, axis=1)
    return dict(s5_bbh_re=bbh_re, s5_bbl_re=bbl_re, s5_bbh_im=bbh_im, s5_bbl_im=bbl_im,
                s5_cc_re=block_out(C_re).astype(BF16), s5_cc_im=block_out(C_im).astype(BF16),
                s5_apow=apow, s5_pw=pws)


def _prep_params(raw, t0s):
    L = raw["w_in"].shape[0]
    W = R_WIDTH
    w_in = raw["w_in"]
    o_s5 = R_COLS
    o_h = o_s5 + S5_WIDTH
    o_g = o_h + 4 * H_WIDTH
    o_gv = o_g + 2 * G_QK
    o_gl = o_gv + G_WIDTH
    o_gg = o_gl + G_GATE_RANK
    def key_pad(w):
        w = w.reshape(w.shape[:-1] + (G_HEADS, G_KEY_DIM))
        w = jnp.pad(w, ((0, 0),) * (w.ndim - 1) + ((0, G_KEY_PAD - G_KEY_DIM),))
        return w.reshape(w.shape[:-2] + (G_QK_PAD,))

    def key_pad_rows(w):
        w = w.reshape(L, G_HEADS, G_KEY_DIM, w.shape[-1])
        w = jnp.pad(w, ((0, 0), (0, 0), (0, G_KEY_PAD - G_KEY_DIM), (0, 0)))
        return w.reshape(L, G_QK_PAD, w.shape[-1])

    w_t = jnp.swapaxes(w_in, 1, 2)
    pad = jnp.zeros((L, GL_PAD - G_GATE_RANK, w_t.shape[-1]), w_in.dtype)
    w_in_p = jnp.concatenate([
        w_t[:, 0:3 * W], w_t[:, o_s5:o_h], w_t[:, o_h:o_g], w_t[:, o_gv:o_gl],
        w_t[:, o_gg:o_gg + G_WIDTH], key_pad_rows(w_t[:, o_g:o_g + G_QK]), key_pad_rows(w_t[:, o_g + G_QK:o_gv]),
        w_t[:, 3 * W:R_COLS], w_t[:, o_gl:o_gg], pad], axis=1).astype(BF16)
    assert w_in_p.shape[1] == P_COLS

    def row3(x):
        return x.reshape(L, 1, -1).astype(F32)

    def low_pad(w, off):
        return jnp.pad(w, ((0, 0), (off, R_LOW - off - w.shape[1]), (0, 0))).astype(BF16)

    ff = FF_PAD - D_FF
    w_up = raw["ffn_w_up"]
    head = jnp.arange(W) // R_HEAD_DIM
    prm = dict(
        norm_mix=row3(raw["norm_mix"]), norm_ffn=row3(raw["norm_ffn"]),
        w_in=w_in_p, w_out=raw["w_out"].astype(BF16),
        mu_rkv=row3(raw["rwkv_mu"][:, 0:3 * W]), mu_low=row3(raw["rwkv_mu"][:, 3 * W:]),
        rwkv_w0=row3(raw["rwkv_w0"]), rwkv_a0=row3(raw["rwkv_a0"]),
        rwkv_w_up=low_pad(raw["rwkv_w_up"], 0), rwkv_a_up=low_pad(raw["rwkv_a_up"], R_DECAY_RANK),
        rwkv_g_up=low_pad(raw["rwkv_g_up"], R_DECAY_RANK + R_ICL_RANK),
        rwkv_k_k=row3(raw["rwkv_k_k"]), rwkv_k_a=row3(raw["rwkv_k_a"]), rwkv_r_k=row3(raw["rwkv_r_k"]),
        rwkv_ln=row3(raw["rwkv_ln"]),
        e_head=(head[:, None] == head[None, :]).astype(BF16),
        s5_D=row3(raw["s5_D"]), s5_w_glu=raw["s5_w_glu"].astype(BF16), s5_b_glu=row3(raw["s5_b_glu"]),
        hgrn_lb_raw=raw["hgrn_lower_bounds"].astype(F32), hgrn_norm=row3(raw["hgrn_norm"]),
        gla_gk_up=jnp.pad(key_pad(raw["gla_gk_up"]), ((0, 0), (0, GL_PAD - G_GATE_RANK), (0, 0))).astype(BF16),
        gla_gk_b=row3(key_pad(raw["gla_gk_b"])), gla_norm=row3(raw["gla_norm"]),
        ffn_wu=jnp.pad(w_up[..., :D_FF], ((0, 0), (0, 0), (0, ff))).astype(BF16),
        ffn_wg=jnp.pad(w_up[..., D_FF:], ((0, 0), (0, 0), (0, ff))).astype(BF16),
        ffn_conv_w=jnp.pad(raw["ffn_conv_w"], ((0, 0), (0, 0), (0, ff))).astype(F32),
        ffn_conv_b=jnp.pad(raw["ffn_conv_b"], ((0, 0), (0, ff))).reshape(L, 1, FF_PAD).astype(F32),
        ffn_w_down=raw["ffn_w_down"].astype(BF16),
    )
    prm.update(_s5_tables(raw["s5_A_re"], raw["s5_A_im"], raw["s5_log_dt"], raw["s5_B_re"], raw["s5_B_im"],
                          raw["s5_C_re"], raw["s5_C_im"], t0s))
    return prm


def _layer(x, st, acc, l_in, prm, l, cfg):
    shr, shl, s_rw, h_re, h_im, s_hg, s_gl, buf = st
    a_rw, a_hg, a_gl = acc
    p = _norm_matmul(x, prm["norm_mix"], prm["w_in"], l, cfg.tm, P_COLS // 4)
    p3 = p.reshape(cfg.B, cfg.T, P_COLS)
    y_r, a_rw, shr_n, shl_n = _rwkv(p3, (shr, shl, s_rw), a_rw, l_in, prm, l, cfg)
    y_s, h_re_n, h_im_n = _s5(p3, (h_re, h_im), l_in, prm, l, cfg)
    y_h, a_hg = _gla_like(p3, s_hg, a_hg, l_in, prm, l, cfg, True)
    y_g, a_gl = _gla_like(p3, s_gl, a_gl, l_in, prm, l, cfg, False)
    M = cfg.B * cfg.T
    ys = [y.reshape(M, GROUP_WIDTH) for y in (y_r, y_s, y_h, y_g)]
    x = _out_proj(x, ys, prm["w_out"], l, cfg.tm, 1024)
    act, ust = _ffn_up(x, prm["norm_ffn"], prm["ffn_wu"], prm["ffn_wg"], prm["ffn_conv_w"], prm["ffn_conv_b"], buf,
                       l, l_in, cfg, 512)
    x = _ffn_down(x, act, prm["ffn_w_down"], l, cfg.tm, 512)
    return x, (shr_n, shl_n, h_re_n, h_im_n, ust), (a_rw, a_hg, a_gl)


def _state_outputs(new, cfg):
    shr, shl, h_re, h_im, ust = new
    B = cfg.B
    shift = jnp.concatenate([shr, shl], axis=-1).reshape(B, R_COLS)
    if cfg.inject:
        conv = ust.reshape(B, cfg.T, FF_PAD)[:, cfg.T - (CONV_W - 1):, :D_FF]
    else:
        tps = cfg.T // cfg.tm
        conv = ust[tps - 1::tps, :, :D_FF]
    return (shift, h_re.reshape(B, S5_GROUPS, S5_STATE), h_im.reshape(B, S5_GROUPS, S5_STATE), conv)


PROMPT_PAD = 0
SAMPLE_PAD = 4


def kernel(x_prompt, x_sample, state_rwkv, state_rwkv_shift, state_s5_re, state_s5_im, state_hgrn, state_gla, state_ffn_conv, meta_tokens, norm_mix, w_in, w_out, rwkv_mu, rwkv_w0, rwkv_w_up, rwkv_a0, rwkv_a_up, rwkv_g_up, rwkv_k_k, rwkv_k_a, rwkv_r_k, rwkv_ln, s5_A_re, s5_A_im, s5_log_dt, s5_B_re, s5_B_im, s5_C_re, s5_C_im, s5_D, s5_w_glu, s5_b_glu, hgrn_lower_bounds, hgrn_norm, gla_gk_up, gla_gk_b, gla_norm, norm_ffn, ffn_w_up, ffn_conv_w, ffn_conv_b, ffn_w_down, norm_final):
    L = DEPTH
    Bp, Sp, D = x_prompt.shape
    Bs, Ss, _ = x_sample.shape
    Tp = N_META + Sp
    Ts = SAMPLE_PAD + Ss
    raw = dict(norm_mix=norm_mix, w_in=w_in, w_out=w_out, rwkv_mu=rwkv_mu, rwkv_w0=rwkv_w0, rwkv_w_up=rwkv_w_up,
               rwkv_a0=rwkv_a0, rwkv_a_up=rwkv_a_up, rwkv_g_up=rwkv_g_up, rwkv_k_k=rwkv_k_k, rwkv_k_a=rwkv_k_a,
               rwkv_r_k=rwkv_r_k, rwkv_ln=rwkv_ln, s5_A_re=s5_A_re, s5_A_im=s5_A_im, s5_log_dt=s5_log_dt,
               s5_B_re=s5_B_re, s5_B_im=s5_B_im, s5_C_re=s5_C_re, s5_C_im=s5_C_im, s5_D=s5_D, s5_w_glu=s5_w_glu,
               s5_b_glu=s5_b_glu, hgrn_lower_bounds=hgrn_lower_bounds, hgrn_norm=hgrn_norm, gla_gk_up=gla_gk_up,
               gla_gk_b=gla_gk_b, gla_norm=gla_norm, norm_ffn=norm_ffn, ffn_w_up=ffn_w_up, ffn_conv_w=ffn_conv_w,
               ffn_conv_b=ffn_conv_b, ffn_w_down=ffn_w_down)
    prm = _prep_params(raw, (PROMPT_PAD, SAMPLE_PAD))

    cfg_p = _Cfg(B=Bp, T=Tp, t0=PROMPT_PAD, tm=Tp // 3, g_mix=Bp, u_rwkv=2, c_rwkv=48, cb_gla=48, c_gla=48, s5_split=False, g_s5=1, c_s5=Tp // 3,
                 inject=False)
    cfg_s = _Cfg(B=Bs, T=Ts, t0=SAMPLE_PAD, tm=Bs * Ts, g_mix=8, u_rwkv=4, c_rwkv=Ts, cb_gla=Ts, c_gla=Ts, s5_split=True, g_s5=32, c_s5=Ts,
                 inject=True)

    dt = x_prompt.dtype
    meta = jnp.broadcast_to(meta_tokens.astype(dt)[None], (Bp, N_META, D))
    xp = jnp.concatenate([meta, x_prompt], axis=1).reshape(Bp * Tp, D)
    xs = jnp.pad(x_sample, ((0, 0), (SAMPLE_PAD, 0), (0, 0))).reshape(Bs * Ts, D)

    W = R_WIDTH
    st_p = (jnp.zeros((1, Bp, 1, 3 * W), F32), jnp.zeros((1, Bp, 1, R_LOW), F32),
            jnp.zeros((1, Bp, R_HEADS, R_HEAD_DIM, R_HEAD_DIM), F32),
            jnp.zeros((1, Bp, 1, S5_FLAT), F32), jnp.zeros((1, Bp, 1, S5_FLAT), F32),
            jnp.zeros((1, Bp, H_HEADS, H_HEAD_DIM, H_HEAD_DIM), F32),
            jnp.zeros((1, Bp, G_HEADS, G_KEY_DIM, G_VAL_DIM), F32),
            jnp.zeros((1, Bp, CONV_W - 1, FF_PAD), F32))
    buf_s = jnp.pad(state_ffn_conv, ((0, 0), (0, 0), (SAMPLE_PAD - (CONV_W - 1), Ss), (0, FF_PAD - D_FF)))
    st_s = (state_rwkv_shift[:, :, None, 0:3 * W], state_rwkv_shift[:, :, None, 3 * W:], state_rwkv,
            state_s5_re.reshape(L, Bs, 1, S5_FLAT), state_s5_im.reshape(L, Bs, 1, S5_FLAT),
            state_hgrn, state_gla, buf_s.reshape(L, Bs * Ts, FF_PAD))

    def acc_init(B):
        return (jnp.zeros((L, B, R_HEADS, R_HEAD_DIM, R_HEAD_DIM), F32),
                jnp.zeros((L, B, H_HEADS, H_HEAD_DIM, H_HEAD_DIM), F32),
                jnp.zeros((L, B, G_HEADS, G_KEY_DIM, G_VAL_DIM), F32))

    acc_p, acc_s = acc_init(Bp), acc_init(Bs)
    outs_p, outs_s = [], []
    for l in range(L):
        xp, new_p, acc_p = _layer(xp, st_p, acc_p, 0, prm, l, cfg_p)
        xs, new_s, acc_s = _layer(xs, st_s, acc_s, l, prm, l, cfg_s)
        outs_p.append(_state_outputs(new_p, cfg_p))
        outs_s.append(_state_outputs(new_s, cfg_s))

    g_fin = norm_final.reshape(1, D).astype(F32)
    y_prompt = _final_norm_skip(xp.reshape(Bp, Tp, D), g_fin, N_META, 512)
    y_sample = _final_norm(xs, g_fin, cfg_s.tm).reshape(Bs, Ts, D)[:, SAMPLE_PAD:]
    sp = [jnp.stack([o[i] for o in outs_p]) for i in range(4)]
    ss = [jnp.stack([o[i] for o in outs_s]) for i in range(4)]
    return (y_prompt, y_sample, acc_p[0], acc_s[0], sp[0], ss[0], sp[1], ss[1], sp[2], ss[2],
            acc_p[1], acc_s[1], acc_p[2], acc_s[2], sp[3], ss[3])
```

```python
import functools
import math
from typing import NamedTuple

import jax
import jax.numpy as jnp
from jax import lax
from jax.experimental import pallas as pl
from jax.experimental.pallas import tpu as pltpu

F32 = jnp.float32
BF16 = jnp.bfloat16

D_MODEL = 2048
DEPTH = 4
N_META = 16
GROUP_WIDTH = D_MODEL // 4
EPS = 1e-6
R_HEAD_DIM = 64
R_HEADS = GROUP_WIDTH // R_HEAD_DIM
R_WIDTH = R_HEADS * R_HEAD_DIM
R_DECAY_RANK = 64
R_ICL_RANK = 64
R_GATE_RANK = 128
R_LOW = R_DECAY_RANK + R_ICL_RANK + R_GATE_RANK
R_COLS = 3 * R_WIDTH + R_LOW
RWKV_DECAY_SCALE = 0.606531
RWKV_GN_EPS = 64e-5
S5_GROUP_CH = 16
S5_GROUPS = GROUP_WIDTH // S5_GROUP_CH
S5_WIDTH = S5_GROUPS * S5_GROUP_CH
S5_STATE = 64
S5_FLAT = S5_GROUPS * S5_STATE
H_HEAD_DIM = 128
H_HEADS = GROUP_WIDTH // H_HEAD_DIM
H_WIDTH = H_HEADS * H_HEAD_DIM
HGRN_MAX_INPUT = 1.0 - 1e-4
G_VAL_DIM = 128
G_HEADS = GROUP_WIDTH // G_VAL_DIM
G_KEY_DIM = G_VAL_DIM // 2
G_WIDTH = G_HEADS * G_VAL_DIM
G_QK = G_HEADS * G_KEY_DIM
G_GATE_RANK = 16
GLA_GATE_NORM = 16.0
GLA_SAFE_DECAY = 80.0
D_FF = ((8 * D_MODEL // 3 + 127) // 128) * 128
CONV_W = 3

P_R = 0
P_S5 = 3 * R_WIDTH
P_H = P_S5 + S5_WIDTH
P_GV = P_H + 4 * H_WIDTH
P_GG = P_GV + G_WIDTH
G_KEY_PAD = G_VAL_DIM
G_QK_PAD = G_HEADS * G_KEY_PAD
P_GQ = P_GG + G_WIDTH
P_GK = P_GQ + G_QK_PAD
P_RL = P_GK + G_QK_PAD
P_GL = P_RL + R_LOW
GL_PAD = 256
P_COLS = P_GL + GL_PAD
FF_PAD = 5632
S5_CHUNKS = 4
S5_SCAN = 8
FFN_COL_BLOCK = 256

VMEM_LIMIT = 56 * 1024 * 1024

NT = (((1,), (1,)), ((), ()))
TN = (((0,), (0,)), ((), ()))


class _Cfg(NamedTuple):
    B: int
    T: int
    t0: int
    tm: int
    g_mix: int
    u_rwkv: int
    c_rwkv: int
    cb_gla: int
    c_gla: int
    s5_split: bool
    g_s5: int
    c_s5: int
    inject: bool


def _dot(a, b):
    return jnp.dot(a, b, preferred_element_type=F32)


def _dg(a, b, dims):
    return lax.dot_general(a, b, dims, preferred_element_type=F32)


def _gelu(x):
    c = math.sqrt(2.0 / math.pi)
    return x * (0.5 * (1.0 + jnp.tanh(c * (x + 0.044715 * (x * x * x)))))


def _tloc(G, C, W):
    t = lax.broadcasted_iota(jnp.int32, (C, W), 0)
    return t if G == 1 else jnp.concatenate([t] * G, axis=0)


def _cumsum_rows(x, tl, C):
    sh = 1
    while sh < C:
        x = x + jnp.where(tl >= sh, pltpu.roll(x, sh, 0), 0.0)
        sh *= 2
    return x


def _segsum(x, e):
    x1 = x.astype(BF16)
    x2 = (x - x1.astype(F32)).astype(BF16)
    return _dot(x1, e) + _dot(x2, e)


def _params(sem):
    return pltpu.CompilerParams(dimension_semantics=sem, vmem_limit_bytes=VMEM_LIMIT)


def _norm_matmul_kernel(x_ref, g_ref, w_ref, o_ref, h_scr):
    @pl.when(pl.program_id(1) == 0)
    def _():
        x = x_ref[...]
        r = lax.rsqrt(jnp.mean(x * x, axis=-1, keepdims=True) + EPS)
        h_scr[...] = (x * r * g_ref[...]).astype(BF16)

    o_ref[...] = _dg(h_scr[...], w_ref[...], NT)


def _norm_matmul(x, g_all, w_all, l, tm, tn):
    M, D = x.shape
    N = w_all.shape[1]
    return pl.pallas_call(
        _norm_matmul_kernel,
        grid=(M // tm, N // tn),
        in_specs=[
            pl.BlockSpec((tm, D), lambda i, j: (i, 0)),
            pl.BlockSpec((None, 1, D), lambda i, j: (l, 0, 0)),
            pl.BlockSpec((None, tn, D), lambda i, j: (l, j, 0)),
        ],
        out_specs=pl.BlockSpec((tm, tn), lambda i, j: (i, j)),
        out_shape=jax.ShapeDtypeStruct((M, N), F32),
        scratch_shapes=[pltpu.VMEM((tm, D), BF16)],
        compiler_params=_params(("arbitrary", "arbitrary")),
        name="in_proj",
    )(x, g_all, w_all)


def _out_proj_kernel(x_ref, y0, y1, y2, y3, w0, w1, w2, w3, o_ref):
    acc = _dot(y0[...].astype(BF16), w0[...])
    acc = acc + _dot(y1[...].astype(BF16), w1[...])
    acc = acc + _dot(y2[...].astype(BF16), w2[...])
    acc = acc + _dot(y3[...].astype(BF16), w3[...])
    o_ref[...] = x_ref[...] + acc


def _out_proj(x, ys, w_all, l, tm, tn):
    M, D = x.shape
    W = GROUP_WIDTH
    yspec = pl.BlockSpec((tm, W), lambda i, j: (i, 0))
    wspecs = [pl.BlockSpec((None, W, tn), functools.partial(lambda i, j, q: (l, q, j), q=q)) for q in range(4)]
    return pl.pallas_call(
        _out_proj_kernel,
        grid=(M // tm, D // tn),
        in_specs=[pl.BlockSpec((tm, tn), lambda i, j: (i, j))] + [yspec] * 4 + wspecs,
        out_specs=pl.BlockSpec((tm, tn), lambda i, j: (i, j)),
        out_shape=jax.ShapeDtypeStruct((M, D), F32),
        compiler_params=_params(("arbitrary", "arbitrary")),
        name="out_proj",
    )(x, *ys, w_all, w_all, w_all, w_all)


def _ffn_up_kernel(x_ref, g_ref, wu_ref, wg_ref, cw_ref, cb_ref, buf_ref, act_ref, ust_ref, h_scr, *car,
                   tm, tn, tiles_per_seq, t0, seq_rows, inject):
    i = pl.program_id(0)
    j = pl.program_id(1)

    @pl.when(j == 0)
    def _():
        x = x_ref[...]
        r = lax.rsqrt(jnp.mean(x * x, axis=-1, keepdims=True) + EPS)
        h_scr[...] = (x * r * g_ref[...]).astype(BF16)

    if not inject:
        (car_scr,) = car

        @pl.when(i % tiles_per_seq == 0)
        def _():
            car_scr[j, 0:2, :] = buf_ref[...]

    nb = FFN_COL_BLOCK
    blocks = [slice(nb * q, nb * (q + 1)) for q in range(tn // nb)]
    h = h_scr[...]
    us = [_dot(h, wu_ref[:, cs]) for cs in blocks]
    gts = [_dot(h, wg_ref[:, cs]) for cs in blocks]
    rows = lax.broadcasted_iota(jnp.int32, (tm, nb), 0)
    for cs, u, gt in zip(blocks, us, gts):
        if inject:
            tl = rows & (seq_rows - 1)
            u = jnp.where((tl == t0 - 2) | (tl == t0 - 1), buf_ref[:, cs], u)
            p1 = pltpu.roll(u, 1, 0)
            p2 = pltpu.roll(u, 2, 0)
            ust_ref[:, cs] = u
        else:
            c0 = car_scr[j, 0:1, cs]
            c1 = car_scr[j, 1:2, cs]
            p1 = jnp.where(rows == 0, c1, pltpu.roll(u, 1, 0))
            p2 = jnp.where(rows == 0, c0, jnp.where(rows == 1, c1, pltpu.roll(u, 2, 0)))
            last = u[tm - 2:tm, :]
            car_scr[j, 0:2, cs] = last
            ust_ref[:, cs] = last
        c = cb_ref[:, cs] + cw_ref[0:1, cs] * p2
        c = c + cw_ref[1:2, cs] * p1
        c = c + cw_ref[2:3, cs] * u
        act_ref[:, cs] = (_gelu(c) * gt).astype(BF16)


def _ffn_up(x, g_all, wu_all, wg_all, cw_all, cb_all, buf, l, l_in, cfg, tn):
    M, D = x.shape
    tm = cfg.tm
    N = wu_all.shape[-1]
    tps = cfg.T // tm if not cfg.inject else 1
    kern = functools.partial(_ffn_up_kernel, tm=tm, tn=tn, tiles_per_seq=tps, t0=cfg.t0, seq_rows=cfg.T,
                             inject=cfg.inject)
    if cfg.inject:
        buf_spec = pl.BlockSpec((None, tm, tn), lambda i, j: (l_in, i, j))
        ust_spec = pl.BlockSpec((tm, tn), lambda i, j: (i, j))
        ust_shape = jax.ShapeDtypeStruct((M, N), F32)
        scratch = [pltpu.VMEM((tm, D), BF16)]
    else:
        buf_spec = pl.BlockSpec((None, None, CONV_W - 1, tn), lambda i, j: (l_in, i // tps, 0, j))
        ust_spec = pl.BlockSpec((None, CONV_W - 1, tn), lambda i, j: (i, 0, j))
        ust_shape = jax.ShapeDtypeStruct((M // tm, CONV_W - 1, N), F32)
        scratch = [pltpu.VMEM((tm, D), BF16), pltpu.VMEM((N // tn, 8, tn), F32)]
    return pl.pallas_call(
        kern,
        grid=(M // tm, N // tn),
        in_specs=[
            pl.BlockSpec((tm, D), lambda i, j: (i, 0)),
            pl.BlockSpec((None, 1, D), lambda i, j: (l, 0, 0)),
            pl.BlockSpec((None, D, tn), lambda i, j: (l, 0, j)),
            pl.BlockSpec((None, D, tn), lambda i, j: (l, 0, j)),
            pl.BlockSpec((None, CONV_W, tn), lambda i, j: (l, 0, j)),
            pl.BlockSpec((None, 1, tn), lambda i, j: (l, 0, j)),
            buf_spec,
        ],
        out_specs=[pl.BlockSpec((tm, tn), lambda i, j: (i, j)), ust_spec],
        out_shape=[jax.ShapeDtypeStruct((M, N), BF16), ust_shape],
        scratch_shapes=scratch,
        compiler_params=_params(("arbitrary", "arbitrary")),
        name="ffn_up",
    )(x, g_all, wu_all, wg_all, cw_all, cb_all, buf)


def _ffn_down_kernel(x_ref, act_ref, w_ref, o_ref):
    o_ref[...] = x_ref[...] + _dot(act_ref[...], w_ref[...])


def _ffn_down(x, act, w_all, l, tm, tn):
    M, D = x.shape
    K = w_all.shape[1]
    return pl.pallas_call(
        _ffn_down_kernel,
        grid=(M // tm, D // tn),
        in_specs=[
            pl.BlockSpec((tm, tn), lambda i, j: (i, j)),
            pl.BlockSpec((tm, K), lambda i, j: (i, 0)),
            pl.BlockSpec((None, K, tn), lambda i, j: (l, 0, j)),
        ],
        out_specs=pl.BlockSpec((tm, tn), lambda i, j: (i, j)),
        out_shape=jax.ShapeDtypeStruct((M, D), F32),
        compiler_params=_params(("arbitrary", "arbitrary")),
        name="ffn_down",
    )(x, act, w_all)


def _final_norm_kernel(x_ref, g_ref, o_ref):
    x = x_ref[...]
    r = lax.rsqrt(jnp.mean(x * x, axis=-1, keepdims=True) + EPS)
    o_ref[...] = x * r * g_ref[...]


def _final_norm_skip_kernel(x_ref, g_ref, o_ref):
    x = x_ref[0]
    r = lax.rsqrt(jnp.mean(x * x, axis=-1, keepdims=True) + EPS)
    o_ref[0] = x * r * g_ref[...]


def _final_norm_skip(x3, g, skip, tr):
    B, T, D = x3.shape
    assert skip % 8 == 0 and tr % 8 == 0 and (T - skip) % tr == 0
    return pl.pallas_call(
        _final_norm_skip_kernel,
        grid=(B, (T - skip) // tr),
        in_specs=[pl.BlockSpec((pl.Element(1), pl.Element(tr), pl.Element(D)),
                               lambda b, i: (b, (skip // 8 + i * (tr // 8)) * 8, 0)),
                  pl.BlockSpec((1, D), lambda b, i: (0, 0))],
        out_specs=pl.BlockSpec((1, tr, D), lambda b, i: (b, i, 0)),
        out_shape=jax.ShapeDtypeStruct((B, T - skip, D), F32),
        compiler_params=_params(("arbitrary", "arbitrary")),
        name="final_norm_skip",
    )(x3, g)


def _final_norm(x, g, tm):
    M, D = x.shape
    return pl.pallas_call(
        _final_norm_kernel,
        grid=(M // tm,),
        in_specs=[pl.BlockSpec((tm, D), lambda i: (i, 0)), pl.BlockSpec((1, D), lambda i: (0, 0))],
        out_specs=pl.BlockSpec((tm, D), lambda i: (i, 0)),
        out_shape=jax.ShapeDtypeStruct((M, D), F32),
        compiler_params=_params(("arbitrary",)),
        name="final_norm",
    )(x, g)


def _rwkv_kernel(prkv_ref, plow_ref, shr_in, shl_in, s0_ref, mur_ref, mul_ref, w0_ref, wup_ref, a0_ref, aup_ref,
                 gup_ref, kk_ref, ka_ref, rk_ref, ln_ref, e_ref, acc_ref,
                 y_ref, sout_ref, shr_out, shl_out,
                 at_scr, rt_scr, kt_scr, bt_scr, v_scr, gam_scr, y_scr, *, G, C, t0, U):
    R = G * C
    W = R_WIDTH

    @pl.when(pl.program_id(1) == 0)
    def _():
        sout_ref[...] = s0_ref[...]
        shr_out[...] = shr_in[...]
        shl_out[...] = shl_in[...]

    def token_shift(p_ref, car_ref, mu_ref, w):
        p = p_ref[...].reshape(R, w)
        tl = _tloc(G, C, w)
        carry = jnp.concatenate([jnp.broadcast_to(car_ref[g], (C, w)) for g in range(G)], axis=0)
        prev = jnp.where(tl == t0, carry, pltpu.roll(p, 1, 0))
        for g in range(G):
            car_ref[g] = p[g * C + C - 1:g * C + C, :]
        return p + (prev - p) * mu_ref[...]

    ps = token_shift(prkv_ref, shr_out, mur_ref, 3 * W)
    lo = token_shift(plow_ref, shl_out, mul_ref, R_LOW)
    r = ps[:, 0:W]
    k = ps[:, W:2 * W]
    v = ps[:, 2 * W:3 * W]
    lw = -RWKV_DECAY_SCALE * jax.nn.sigmoid(w0_ref[...] + _dot(jnp.tanh(lo).astype(BF16), wup_ref[...]))
    a = jax.nn.sigmoid(a0_ref[...] + _dot(lo.astype(BF16), aup_ref[...]))
    gate = _dot(jax.nn.sigmoid(lo).astype(BF16), gup_ref[...])
    e = e_ref[...]
    kk = k * kk_ref[...]
    kk = kk / jnp.maximum(jnp.sqrt(_segsum(kk * kk, e)), 1e-12)
    k2 = k * (1.0 + (a - 1.0) * ka_ref[...])
    ka = kk * a
    tl = _tloc(G, C, W)
    if t0 > 0:
        valid = tl >= t0
        lw = jnp.where(valid, lw, 0.0)
        kk = jnp.where(valid, kk, 0.0)
        ka = jnp.where(valid, ka, 0.0)
        k2 = jnp.where(valid, k2, 0.0)
        v = jnp.where(valid, v, 0.0)
    cum = _cumsum_rows(lw, tl, C)
    gam = jnp.exp(cum)
    inv = jnp.exp(-cum)
    at_scr[...] = kk * jnp.exp(cum - lw)
    rt_scr[...] = r * gam
    kt_scr[...] = k2 * inv
    bt_scr[...] = ka * inv
    v_scr[...] = v
    gam_scr[...] = gam

    row = lax.broadcasted_iota(jnp.int32, (C, C), 0)
    col = lax.broadcasted_iota(jnp.int32, (C, C), 1)
    low_s = row > col
    low_i = row >= col

    def seq_body(it, carry):
        gs = [it * U + i for i in range(U)]
        lanes = [(i, h) for i in range(U) for h in range(R_HEADS)]
        data = []
        for g in gs:
            off = pl.multiple_of(g * C, 8)
            rows = pl.ds(off, C)
            data.append((at_scr[rows, :], rt_scr[rows, :], kt_scr[rows, :], bt_scr[rows, :], v_scr[rows, :],
                         gam_scr[pl.ds(off + C - 8, 8), :][7:8, :]))
        s_in = [sout_ref[gs[i], h] for i, h in lanes]

        def head(x, h):
            return x[:, R_HEAD_DIM * h:R_HEAD_DIM * (h + 1)]

        ar = [jnp.concatenate([head(data[i][0], h), head(data[i][1], h)], axis=0).astype(BF16) for i, h in lanes]
        ktb = [head(data[i][2], h).astype(BF16) for i, h in lanes]
        btb = [head(data[i][3], h).astype(BF16) for i, h in lanes]
        vb = [head(data[i][4], h).astype(BF16) for i, h in lanes]
        n = range(len(lanes))
        m_k = [_dg(ar[j], ktb[j], NT) for j in n]
        m_b = [_dg(ar[j], btb[j], NT) for j in n]
        n_pow = [jnp.where(low_s, m_b[j][0:C], 0.0) for j in n]
        acc = [-n_pow[j] for j in n]
        pre_rhs = [_dot(jnp.where(low_s, m_k[j][0:C], 0.0).astype(BF16), vb[j]) for j in n]
        pre_y = [_dot(jnp.where(low_i, m_k[j][C:2 * C], 0.0).astype(BF16), vb[j]) for j in n]
        pre_ds = [_dg(vb[j], ktb[j], TN) for j in n]
        trb = [jnp.where(low_i, m_b[j][C:2 * C], 0.0).astype(BF16) for j in n]
        span = 2
        nb = [n_pow[j].astype(BF16) for j in n]
        n_pow = [_dot(nb[j], nb[j]) for j in n]
        while span < C:
            nb = [n_pow[j].astype(BF16) for j in n]
            span *= 2
            prod = [_dot(acc[j].astype(BF16), nb[j]) for j in n]
            if span < C:
                nxt = [_dot(nb[j], nb[j]) for j in n]
            acc = [acc[j] + n_pow[j] + prod[j] for j in n]
            if span < C:
                n_pow = nxt
        accb = [acc[j].astype(BF16) for j in n]
        m_s = [_dg(ar[j], s_in[j].astype(BF16), NT) for j in n]
        rhs = [m_s[j][0:C] + pre_rhs[j] for j in n]
        u = [rhs[j] + _dot(accb[j], rhs[j].astype(BF16)) for j in n]
        ub = [u[j].astype(BF16) for j in n]
        yh = [m_s[j][C:2 * C] + pre_y[j] - _dot(trb[j], ub[j]) for j in n]
        ds = [pre_ds[j] - _dg(ub[j], btb[j], TN) for j in n]
        for j, (i, h) in enumerate(lanes):
            sout_ref[gs[i], h] = (s_in[j] + ds[j]) * head(data[i][5], h)
        for i, g in enumerate(gs):
            rows = pl.ds(pl.multiple_of(g * C, 8), C)
            y_scr[rows, :] = jnp.concatenate(yh[i * R_HEADS:(i + 1) * R_HEADS], axis=1)
        return carry

    lax.fori_loop(0, G // U, seq_body, 0)

    y = y_scr[...]
    inv_n = 1.0 / R_HEAD_DIM
    d = y - _segsum(y, e) * inv_n
    var = _segsum(d * d, e) * inv_n
    y = d * lax.rsqrt(var + RWKV_GN_EPS) * ln_ref[...]
    y = y + _segsum(r * k2 * rk_ref[...], e) * v
    y_ref[...] = (y * gate).reshape(G, C, W)


def _rwkv(p3, st, acc, l_in, prm, l, cfg):
    B, T, _ = p3.shape
    G, C = cfg.g_mix, cfg.c_rwkv
    W = R_WIDTH
    shr_in, shl_in, s0 = st

    def pspec(w, off):
        return pl.BlockSpec((G, C, w), lambda i, c: (i, c, off // w))

    def lay(*shape):
        return pl.BlockSpec((None,) + shape, lambda i, c: (l,) + (0,) * len(shape))

    kern = functools.partial(_rwkv_kernel, G=G, C=C, t0=cfg.t0, U=cfg.u_rwkv)
    R = G * C
    return pl.pallas_call(
        kern,
        grid=(B // G, T // C),
        in_specs=[
            pspec(3 * W, P_R), pspec(R_LOW, P_RL),
            pl.BlockSpec((None, G, 1, 3 * W), lambda i, c: (l_in, i, 0, 0)),
            pl.BlockSpec((None, G, 1, R_LOW), lambda i, c: (l_in, i, 0, 0)),
            pl.BlockSpec((None, G, R_HEADS, R_HEAD_DIM, R_HEAD_DIM), lambda i, c: (l_in, i, 0, 0, 0)),
            lay(1, 3 * W), lay(1, R_LOW), lay(1, W), lay(R_LOW, W), lay(1, W), lay(R_LOW, W), lay(R_LOW, W),
            lay(1, W), lay(1, W), lay(1, W), lay(1, W),
            pl.BlockSpec((W, W), lambda i, c: (0, 0)),
            pl.BlockSpec(memory_space=pl.ANY),
        ],
        input_output_aliases={17: 1},
        out_specs=[
            pl.BlockSpec((G, C, W), lambda i, c: (i, c, 0)),
            pl.BlockSpec((None, G, R_HEADS, R_HEAD_DIM, R_HEAD_DIM), lambda i, c: (l, i, 0, 0, 0)),
            pl.BlockSpec((G, 1, 3 * W), lambda i, c: (i, 0, 0)),
            pl.BlockSpec((G, 1, R_LOW), lambda i, c: (i, 0, 0)),
        ],
        out_shape=[
            jax.ShapeDtypeStruct((B, T, W), F32),
            jax.ShapeDtypeStruct(acc.shape, F32),
            jax.ShapeDtypeStruct((B, 1, 3 * W), F32),
            jax.ShapeDtypeStruct((B, 1, R_LOW), F32),
        ],
        scratch_shapes=[pltpu.VMEM((R, W), F32)] * 7,
        compiler_params=_params(("arbitrary", "arbitrary")),
        name="rwkv7",
    )(p3, p3, shr_in, shl_in, s0, prm["mu_rkv"], prm["mu_low"], prm["rwkv_w0"], prm["rwkv_w_up"], prm["rwkv_a0"],
      prm["rwkv_a_up"], prm["rwkv_g_up"], prm["rwkv_k_k"], prm["rwkv_k_a"], prm["rwkv_r_k"], prm["rwkv_ln"],
      prm["e_head"], acc)


def _s5_kernel(u_ref, h0r_ref, h0i_ref, bbhr_ref, bblr_ref, bbhi_ref, bbli_ref, ccr_ref, cci_ref, ap_ref, pw_ref,
               d_ref, wglu_ref, bglu_ref,
               y_ref, hr_out, hi_out, hre_scr, him_scr, *, G, C, t0, split):
    R = G * C
    NB = S5_FLAT // S5_CHUNKS
    UB = S5_WIDTH // S5_CHUNKS

    @pl.when(pl.program_id(1) == 0)
    def _():
        hr_out[...] = h0r_ref[...]
        hi_out[...] = h0i_ref[...]

    u = u_ref[...].reshape(R, S5_WIDTH)
    uh = u.astype(BF16)
    ul = (u - uh.astype(F32)).astype(BF16)
    valid = _tloc(G, C, NB) >= t0
    for j in range(S5_CHUNKS):
        us = slice(UB * j, UB * (j + 1))
        br = _dot(uh[:, us], bbhr_ref[j])
        bi = _dot(uh[:, us], bbhi_ref[j])
        if split:
            br = br + _dot(ul[:, us], bbhr_ref[j]) + _dot(uh[:, us], bblr_ref[j])
            bi = bi + _dot(ul[:, us], bbhi_ref[j]) + _dot(uh[:, us], bbli_ref[j])
        if t0 > 0:
            br = jnp.where(valid, br, 0.0)
            bi = jnp.where(valid, bi, 0.0)
        hre_scr[:, NB * j:NB * (j + 1)] = br
        him_scr[:, NB * j:NB * (j + 1)] = bi

    steps = [(1 << k, ap_ref[S5_SCAN * 2 * k:S5_SCAN * (2 * k + 1), :],
              ap_ref[S5_SCAN * (2 * k + 1):S5_SCAN * (2 * k + 2), :]) for k in range(3)]
    pwr = pw_ref[0:S5_SCAN, :]
    pwi = pw_ref[S5_SCAN:2 * S5_SCAN, :]

    def seq_body(g, carry):
        def tile_body(i, hc):
            hcr, hci = hc
            rows = pl.ds(pl.multiple_of(g * C + i * S5_SCAN, S5_SCAN), S5_SCAN)
            xr = hre_scr[rows, :]
            xi = him_scr[rows, :]
            for sh, ar, ai in steps:
                sr = pltpu.roll(xr, sh, 0)
                si = pltpu.roll(xi, sh, 0)
                xr, xi = xr + ar * sr - ai * si, xi + ar * si + ai * sr
            xr, xi = xr + pwr * hcr - pwi * hci, xi + pwr * hci + pwi * hcr
            hre_scr[rows, :] = xr
            him_scr[rows, :] = xi
            return xr[S5_SCAN - 1:S5_SCAN, :], xi[S5_SCAN - 1:S5_SCAN, :]

        hcr, hci = lax.fori_loop(0, C // S5_SCAN, tile_body, (hr_out[g], hi_out[g]))
        hr_out[g] = hcr
        hi_out[g] = hci
        return carry

    lax.fori_loop(0, G, seq_body, 0)

    ys = []
    for j in range(S5_CHUNKS):
        hs = slice(NB * j, NB * (j + 1))
        ys.append(_dot(hre_scr[:, hs].astype(BF16), ccr_ref[j]) - _dot(him_scr[:, hs].astype(BF16), cci_ref[j]))
    y = jnp.concatenate(ys, axis=1) + d_ref[...] * u
    y = _gelu(y)
    y = y * jax.nn.sigmoid(_dot(y.astype(BF16), wglu_ref[...]) + bglu_ref[...])
    y_ref[...] = y.reshape(G, C, S5_WIDTH)


def _s5(p3, st, l_in, prm, l, cfg):
    B, T, _ = p3.shape
    G, C = cfg.g_s5, cfg.c_s5
    h0r, h0i = st
    NB = S5_FLAT // S5_CHUNKS
    UB = S5_WIDTH // S5_CHUNKS

    def lay(*shape):
        return pl.BlockSpec((None,) + shape, lambda i, c: (l,) + (0,) * len(shape))

    st_spec = pl.BlockSpec((None, G, 1, S5_FLAT), lambda i, c: (l_in, i, 0, 0))
    out_st = pl.BlockSpec((G, 1, S5_FLAT), lambda i, c: (i, 0, 0))
    kern = functools.partial(_s5_kernel, G=G, C=C, t0=cfg.t0, split=cfg.s5_split)
    pw = prm["s5_pw"][cfg.t0]
    return pl.pallas_call(
        kern,
        grid=(B // G, T // C),
        in_specs=[
            pl.BlockSpec((G, C, S5_WIDTH), lambda i, c: (i, c, P_S5 // S5_WIDTH)),
            st_spec, st_spec,
            lay(S5_CHUNKS, UB, NB), lay(S5_CHUNKS, UB, NB), lay(S5_CHUNKS, UB, NB), lay(S5_CHUNKS, UB, NB),
            lay(S5_CHUNKS, NB, UB), lay(S5_CHUNKS, NB, UB),
            lay(6 * S5_SCAN, S5_FLAT), lay(2 * S5_SCAN, S5_FLAT),
            lay(1, S5_WIDTH), lay(S5_WIDTH, S5_WIDTH), lay(1, S5_WIDTH),
        ],
        out_specs=[pl.BlockSpec((G, C, S5_WIDTH), lambda i, c: (i, c, 0)), out_st, out_st],
        out_shape=[
            jax.ShapeDtypeStruct((B, T, S5_WIDTH), F32),
            jax.ShapeDtypeStruct((B, 1, S5_FLAT), F32),
            jax.ShapeDtypeStruct((B, 1, S5_FLAT), F32),
        ],
        scratch_shapes=[pltpu.VMEM((G * C, S5_FLAT), F32)] * 2,
        compiler_params=_params(("arbitrary", "arbitrary")),
        name="s5",
    )(p3, h0r, h0i, prm["s5_bbh_re"], prm["s5_bbl_re"], prm["s5_bbh_im"], prm["s5_bbl_im"], prm["s5_cc_re"],
      prm["s5_cc_im"], prm["s5_apow"], pw, prm["s5_D"], prm["s5_w_glu"], prm["s5_b_glu"])


def _gla_core(q, kx, gdec, v, s0_ref, sout_ref, st_scr, q_scr, k_scr, b_scr, v_scr, o_scr, *, G, CB, C, t0, K, KS):
    assert CB % C == 0 and (CB == C or C & (C - 1) == 0) and (t0 == 0 or CB == C)
    H = G_HEADS
    V = G_VAL_DIM
    HK = H * K
    c = pl.program_id(1)

    @pl.when(c == 0)
    def _():
        for g in range(G):
            for h in range(H):
                s = s0_ref[g, h].T
                if KS < K:
                    s = jnp.concatenate([s, jnp.zeros((V, K - KS), F32)], axis=1)
                st_scr[g, h] = s

    if CB == C:
        tl = _tloc(G, C, HK)
    else:
        tl = lax.broadcasted_iota(jnp.int32, (G * CB, HK), 0) & (C - 1)
    if t0 > 0:
        valid = tl >= t0
        gdec = jnp.where(valid, gdec, 0.0)
        kx = jnp.where(valid, kx, 0.0)
    q_scr[...] = q
    k_scr[...] = kx
    b_scr[...] = _cumsum_rows(gdec, tl, C)
    v_scr[...] = v
    ti = lax.broadcasted_iota(jnp.int32, (C, HK), 0)

    row = lax.broadcasted_iota(jnp.int32, (C, C), 0)
    col = lax.broadcasted_iota(jnp.int32, (C, C), 1)
    lanes = [(g, h) for g in range(G) for h in range(H)]

    def ksl(h):
        return slice(K * h, K * (h + 1))

    def vsl(h):
        return slice(V * h, V * (h + 1))

    def chunk_body(j, carry):
        rows = [pl.ds(pl.multiple_of(g * CB + j * C, 8), C) for g in range(G)]
        qg = [q_scr[r, :] for r in rows]
        kg = [k_scr[r, :] for r in rows]
        bg = [b_scr[r, :] for r in rows]
        vg = [v_scr[r, :] for r in rows]
        vb = [x.astype(BF16) for x in vg]
        b_last = [b[C - 1:C, :] for b in bg]
        qe = [(qg[g] * jnp.exp(bg[g])).astype(BF16) for g in range(G)]
        kd = [(kg[g] * jnp.exp(b_last[g] - bg[g])).astype(BF16) for g in range(G)]
        dec_last = [jnp.exp(b) for b in b_last]
        s_in = [st_scr[g, h] for g, h in lanes]
        inter = [_dg(qe[g][:, ksl(h)], s_in[n].astype(BF16), NT) for n, (g, h) in enumerate(lanes)]
        upd = [_dg(vb[g][:, vsl(h)], kd[g][:, ksl(h)], TN) for g, h in lanes]
        b_min = b_last[0]
        for b in b_last[1:]:
            b_min = jnp.minimum(b_min, b)

        def factored():
            kn = [(kg[g] * jnp.exp(-bg[g])).astype(BF16) for g in range(G)]
            att = [_dg(qe[g][:, ksl(h)], kn[g][:, ksl(h)], NT) for g, h in lanes]
            att = [jnp.where(row >= col, a, 0.0).astype(BF16) for a in att]
            outs = [_dot(att[n], vb[g][:, vsl(h)]) for n, (g, h) in enumerate(lanes)]
            return [jnp.concatenate(outs[g * H:(g + 1) * H], axis=1) for g in range(G)]

        def pairwise():
            res = []
            for g in range(G):
                base = g * CB + j * C

                def key_row(jj, intra, g=g, base=base):
                    kj = k_scr[pl.ds(base + jj, 1), :]
                    bj = b_scr[pl.ds(base + jj, 1), :]
                    vj = v_scr[pl.ds(base + jj, 1), :]
                    causal = ti >= jj
                    e = jnp.where(causal, qg[g] * kj * jnp.exp(jnp.where(causal, bg[g] - bj, 0.0)), 0.0)
                    return tuple(intra[h] + jnp.sum(e[:, ksl(h)], axis=-1, keepdims=True) * vj[:, vsl(h)]
                                 for h in range(H))

                intra = lax.fori_loop(0, C, key_row, tuple(jnp.zeros((C, V), F32) for _ in range(H)))
                res.append(jnp.concatenate(intra, axis=1))
            return res

        intra = lax.cond(jnp.min(b_min) >= -GLA_SAFE_DECAY, factored, pairwise)
        for n, (g, h) in enumerate(lanes):
            st_scr[g, h] = s_in[n] * dec_last[g][:, ksl(h)] + upd[n]
        for g in range(G):
            o_scr[rows[g], :] = jnp.concatenate(inter[g * H:(g + 1) * H], axis=1) + intra[g]
        return carry

    lax.fori_loop(0, CB // C, chunk_body, 0)

    @pl.when(c == pl.num_programs(1) - 1)
    def _():
        for g in range(G):
            for h in range(H):
                sout_ref[g, h] = st_scr[g, h][:, 0:KS].T


def _head_norm_gate(o, norm, gate_raw):
    V = G_VAL_DIM
    ys = []
    for h in range(G_HEADS):
        oh = o[:, V * h:V * (h + 1)]
        ys.append(oh * lax.rsqrt(jnp.mean(oh * oh, axis=-1, keepdims=True) + EPS))
    y = jnp.concatenate(ys, axis=1) * norm
    return y * (gate_raw * jax.nn.sigmoid(gate_raw))


def _hgrn_kernel(q_ref, f_ref, i_ref, g_ref, lb_ref, norm_ref, s0_ref, acc_ref, y_ref, sout_ref,
                 st_scr, q_scr, k_scr, b_scr, v_scr, o_scr, *, G, C, chunk, t0, layer):
    R = G * C
    W = H_WIDTH
    lbr = lb_ref[...]
    ex = jnp.exp(lbr - jnp.max(lbr, axis=0, keepdims=True))
    sm = ex / jnp.sum(ex, axis=0, keepdims=True)
    lb = jnp.zeros((1, W), F32)
    for i in range(1, layer + 1):
        lb = lb + sm[i:i + 1, :]
    qr = q_ref[...].reshape(R, W)
    kx = jnp.minimum((1.0 - lb) * jax.nn.sigmoid(-f_ref[...].reshape(R, W)), HGRN_MAX_INPUT)
    _gla_core(qr * jax.nn.sigmoid(qr), kx, jnp.log1p(-kx), i_ref[...].reshape(R, W), s0_ref, sout_ref,
              st_scr, q_scr, k_scr, b_scr, v_scr, o_scr, G=G, CB=C, C=chunk, t0=t0, K=H_HEAD_DIM, KS=H_HEAD_DIM)
    y_ref[...] = _head_norm_gate(o_scr[...], norm_ref[...], g_ref[...].reshape(R, W)).reshape(G, C, W)


def _gla_kernel(v_ref, gate_ref, q_ref, k_ref, gl_ref, gkup_ref, gkb_ref, norm_ref, s0_ref, acc_ref, y_ref,
                sout_ref,
                st_scr, q_scr, k_scr, b_scr, v_scr, o_scr, *, G, C, chunk, t0):
    R = G * C
    z = _dot(gl_ref[...].reshape(R, GL_PAD).astype(BF16), gkup_ref[...]) + gkb_ref[...]
    gdec = -(jnp.maximum(-z, 0.0) + jnp.log1p(jnp.exp(-jnp.abs(z)))) / GLA_GATE_NORM
    _gla_core(q_ref[...].reshape(R, G_QK_PAD) * (G_KEY_DIM ** -0.5), k_ref[...].reshape(R, G_QK_PAD), gdec,
              v_ref[...].reshape(R, G_WIDTH), s0_ref, sout_ref, st_scr, q_scr, k_scr, b_scr, v_scr, o_scr,
              G=G, CB=C, C=chunk, t0=t0, K=G_KEY_PAD, KS=G_KEY_DIM)
    y_ref[...] = _head_norm_gate(o_scr[...], norm_ref[...], gate_ref[...].reshape(R, G_WIDTH)).reshape(G, C, G_WIDTH)


def _gla_like(p3, s0, acc, l_in, prm, l, cfg, hgrn):
    B, T, _ = p3.shape
    G, C = cfg.g_mix, cfg.cb_gla
    H, V = G_HEADS, G_VAL_DIM
    K = H_HEAD_DIM if hgrn else G_KEY_PAD
    KS = H_HEAD_DIM if hgrn else G_KEY_DIM
    HK = H * K
    R = G * C

    def pspec(w, off):
        return pl.BlockSpec((G, C, w), lambda i, c: (i, c, off // w))

    def lay(*shape):
        return pl.BlockSpec((None,) + shape, lambda i, c: (l,) + (0,) * len(shape))

    st_in = pl.BlockSpec((None, G, H, KS, V), lambda i, c: (l_in, i, 0, 0, 0))
    if hgrn:
        kern = functools.partial(_hgrn_kernel, G=G, C=C, chunk=cfg.c_gla, t0=cfg.t0, layer=l)
        in_specs = [pspec(HK, P_H), pspec(HK, P_H + HK), pspec(HK, P_H + 2 * HK), pspec(HK, P_H + 3 * HK),
                    pl.BlockSpec((DEPTH, HK), lambda i, c: (0, 0)), lay(1, HK), st_in]
        args = (p3, p3, p3, p3, prm["hgrn_lb_raw"], prm["hgrn_norm"], s0)
        name = "hgrn2"
    else:
        kern = functools.partial(_gla_kernel, G=G, C=C, chunk=cfg.c_gla, t0=cfg.t0)
        in_specs = [pspec(G_WIDTH, P_GV), pspec(G_WIDTH, P_GG), pspec(HK, P_GQ), pspec(HK, P_GK),
                    pspec(GL_PAD, P_GL), lay(GL_PAD, HK), lay(1, HK), lay(1, G_WIDTH), st_in]
        args = (p3, p3, p3, p3, p3, prm["gla_gk_up"], prm["gla_gk_b"], prm["gla_norm"], s0)
        name = "gla"
    return pl.pallas_call(
        kern,
        grid=(B // G, T // C),
        in_specs=in_specs + [pl.BlockSpec(memory_space=pl.ANY)],
        out_specs=[pl.BlockSpec((G, C, H * V), lambda i, c: (i, c, 0)),
                   pl.BlockSpec((None, G, H, KS, V), lambda i, c: (l, i, 0, 0, 0))],
        out_shape=[jax.ShapeDtypeStruct((B, T, H * V), F32), jax.ShapeDtypeStruct(acc.shape, F32)],
        input_output_aliases={len(args): 1},
        scratch_shapes=[pltpu.VMEM((G, H, V, K), F32), pltpu.VMEM((R, HK), F32), pltpu.VMEM((R, HK), F32),
                        pltpu.VMEM((R, HK), F32), pltpu.VMEM((R, H * V), F32), pltpu.VMEM((R, H * V), F32)],
        compiler_params=_params(("arbitrary", "arbitrary")),
        name=name,
    )(*args, acc)


def _s5_tables(A_re, A_im, log_dt, B_re, B_im, C_re, C_im, t0s):
    L = A_re.shape[0]
    A_re = A_re.astype(F32)
    A_im = A_im.astype(F32)
    dt = jnp.exp(log_dt.astype(F32))[..., None]
    mag = jnp.exp(A_re * dt)
    ab_re = mag * jnp.cos(A_im * dt)
    ab_im = mag * jnp.sin(A_im * dt)
    den = A_re * A_re + A_im * A_im
    n_re = ab_re - 1.0
    co_re = (n_re * A_re + ab_im * A_im) / den
    co_im = (ab_im * A_re - n_re * A_im) / den
    B_re = B_re.astype(F32)
    B_im = B_im.astype(F32)
    bb_re = co_re[..., None] * B_re - co_im[..., None] * B_im
    bb_im = co_re[..., None] * B_im + co_im[..., None] * B_re
    gpc = S5_GROUPS // S5_CHUNKS
    eye = jnp.eye(gpc, dtype=F32)

    def block_in(bb):
        t = bb.reshape(L, S5_CHUNKS, gpc, S5_STATE, S5_GROUP_CH).transpose(0, 1, 2, 4, 3)
        t = t[:, :, :, :, None, :] * eye[None, None, :, None, :, None]
        return t.reshape(L, S5_CHUNKS, gpc * S5_GROUP_CH, gpc * S5_STATE)

    def block_out(cc):
        t = cc.astype(F32).reshape(L, S5_CHUNKS, gpc, S5_GROUP_CH, S5_STATE).transpose(0, 1, 2, 4, 3)
        t = t[:, :, :, :, None, :] * eye[None, None, :, None, :, None]
        return t.reshape(L, S5_CHUNKS, gpc * S5_STATE, gpc * S5_GROUP_CH)

    def split(x):
        hi = x.astype(BF16)
        return hi, (x - hi.astype(F32)).astype(BF16)

    bbh_re, bbl_re = split(block_in(bb_re))
    bbh_im, bbl_im = split(block_in(bb_im))

    def power(m):
        mg = jnp.exp(m * dt * A_re)
        return (mg * jnp.cos(m * dt * A_im)).reshape(L, S5_FLAT), (mg * jnp.sin(m * dt * A_im)).reshape(L, S5_FLAT)

    zero = jnp.zeros((L, S5_FLAT), F32)
    levels = []
    for sh in (1, 2, 4):
        pr, pi = power(float(sh))
        levels += [zero if r < sh else pr for r in range(S5_SCAN)]
        levels += [zero if r < sh else pi for r in range(S5_SCAN)]
    apow = jnp.stack(levels, axis=1)
    pws = {}
    for t0 in t0s:
        first = t0 % S5_SCAN
        res, ims = [], []
        for r in range(S5_SCAN):
            if r < first:
                res.append(zero)
                ims.append(zero)
            else:
                pr, pi = power(float(r - first + 1))
                res.append(pr)
                ims.append(pi)
        pws[t0] = jnp.stack(res + ims, axis=1)
    return dict(s5_bbh_re=bbh_re, s5_bbl_re=bbl_re, s5_bbh_im=bbh_im, s5_bbl_im=bbl_im,
                s5_cc_re=block_out(C_re).astype(BF16), s5_cc_im=block_out(C_im).astype(BF16),
                s5_apow=apow, s5_pw=pws)


def _prep_params(raw, t0s):
    L = raw["w_in"].shape[0]
    W = R_WIDTH
    w_in = raw["w_in"]
    o_s5 = R_COLS
    o_h = o_s5 + S5_WIDTH
    o_g = o_h + 4 * H_WIDTH
    o_gv = o_g + 2 * G_QK
    o_gl = o_gv + G_WIDTH
    o_gg = o_gl + G_GATE_RANK
    def key_pad(w):
        w = w.reshape(w.shape[:-1] + (G_HEADS, G_KEY_DIM))
        w = jnp.pad(w, ((0, 0),) * (w.ndim - 1) + ((0, G_KEY_PAD - G_KEY_DIM),))
        return w.reshape(w.shape[:-2] + (G_QK_PAD,))

    def key_pad_rows(w):
        w = w.reshape(L, G_HEADS, G_KEY_DIM, w.shape[-1])
        w = jnp.pad(w, ((0, 0), (0, 0), (0, G_KEY_PAD - G_KEY_DIM), (0, 0)))
        return w.reshape(L, G_QK_PAD, w.shape[-1])

    w_t = jnp.swapaxes(w_in, 1, 2)
    pad = jnp.zeros((L, GL_PAD - G_GATE_RANK, w_t.shape[-1]), w_in.dtype)
    w_in_p = jnp.concatenate([
        w_t[:, 0:3 * W], w_t[:, o_s5:o_h], w_t[:, o_h:o_g], w_t[:, o_gv:o_gl],
        w_t[:, o_gg:o_gg + G_WIDTH], key_pad_rows(w_t[:, o_g:o_g + G_QK]), key_pad_rows(w_t[:, o_g + G_QK:o_gv]),
        w_t[:, 3 * W:R_COLS], w_t[:, o_gl:o_gg], pad], axis=1).astype(BF16)
    assert w_in_p.shape[1] == P_COLS

    def row3(x):
        return x.reshape(L, 1, -1).astype(F32)

    def low_pad(w, off):
        return jnp.pad(w, ((0, 0), (off, R_LOW - off - w.shape[1]), (0, 0))).astype(BF16)

    ff = FF_PAD - D_FF
    w_up = raw["ffn_w_up"]
    head = jnp.arange(W) // R_HEAD_DIM
    prm = dict(
        norm_mix=row3(raw["norm_mix"]), norm_ffn=row3(raw["norm_ffn"]),
        w_in=w_in_p, w_out=raw["w_out"].astype(BF16),
        mu_rkv=row3(raw["rwkv_mu"][:, 0:3 * W]), mu_low=row3(raw["rwkv_mu"][:, 3 * W:]),
        rwkv_w0=row3(raw["rwkv_w0"]), rwkv_a0=row3(raw["rwkv_a0"]),
        rwkv_w_up=low_pad(raw["rwkv_w_up"], 0), rwkv_a_up=low_pad(raw["rwkv_a_up"], R_DECAY_RANK),
        rwkv_g_up=low_pad(raw["rwkv_g_up"], R_DECAY_RANK + R_ICL_RANK),
        rwkv_k_k=row3(raw["rwkv_k_k"]), rwkv_k_a=row3(raw["rwkv_k_a"]), rwkv_r_k=row3(raw["rwkv_r_k"]),
        rwkv_ln=row3(raw["rwkv_ln"]),
        e_head=(head[:, None] == head[None, :]).astype(BF16),
        s5_D=row3(raw["s5_D"]), s5_w_glu=raw["s5_w_glu"].astype(BF16), s5_b_glu=row3(raw["s5_b_glu"]),
        hgrn_lb_raw=raw["hgrn_lower_bounds"].astype(F32), hgrn_norm=row3(raw["hgrn_norm"]),
        gla_gk_up=jnp.pad(key_pad(raw["gla_gk_up"]), ((0, 0), (0, GL_PAD - G_GATE_RANK), (0, 0))).astype(BF16),
        gla_gk_b=row3(key_pad(raw["gla_gk_b"])), gla_norm=row3(raw["gla_norm"]),
        ffn_wu=jnp.pad(w_up[..., :D_FF], ((0, 0), (0, 0), (0, ff))).astype(BF16),
        ffn_wg=jnp.pad(w_up[..., D_FF:], ((0, 0), (0, 0), (0, ff))).astype(BF16),
        ffn_conv_w=jnp.pad(raw["ffn_conv_w"], ((0, 0), (0, 0), (0, ff))).astype(F32),
        ffn_conv_b=jnp.pad(raw["ffn_conv_b"], ((0, 0), (0, ff))).reshape(L, 1, FF_PAD).astype(F32),
        ffn_w_down=raw["ffn_w_down"].astype(BF16),
    )
    prm.update(_s5_tables(raw["s5_A_re"], raw["s5_A_im"], raw["s5_log_dt"], raw["s5_B_re"], raw["s5_B_im"],
                          raw["s5_C_re"], raw["s5_C_im"], t0s))
    return prm


def _layer(x, st, acc, l_in, prm, l, cfg):
    shr, shl, s_rw, h_re, h_im, s_hg, s_gl, buf = st
    a_rw, a_hg, a_gl = acc
    p = _norm_matmul(x, prm["norm_mix"], prm["w_in"], l, cfg.tm, P_COLS // 4)
    p3 = p.reshape(cfg.B, cfg.T, P_COLS)
    y_r, a_rw, shr_n, shl_n = _rwkv(p3, (shr, shl, s_rw), a_rw, l_in, prm, l, cfg)
    y_s, h_re_n, h_im_n = _s5(p3, (h_re, h_im), l_in, prm, l, cfg)
    y_h, a_hg = _gla_like(p3, s_hg, a_hg, l_in, prm, l, cfg, True)
    y_g, a_gl = _gla_like(p3, s_gl, a_gl, l_in, prm, l, cfg, False)
    M = cfg.B * cfg.T
    ys = [y.reshape(M, GROUP_WIDTH) for y in (y_r, y_s, y_h, y_g)]
    x = _out_proj(x, ys, prm["w_out"], l, cfg.tm, 1024)
    act, ust = _ffn_up(x, prm["norm_ffn"], prm["ffn_wu"], prm["ffn_wg"], prm["ffn_conv_w"], prm["ffn_conv_b"], buf,
                       l, l_in, cfg, 512)
    x = _ffn_down(x, act, prm["ffn_w_down"], l, cfg.tm, 512)
    return x, (shr_n, shl_n, h_re_n, h_im_n, ust), (a_rw, a_hg, a_gl)


def _state_outputs(new, cfg):
    shr, shl, h_re, h_im, ust = new
    B = cfg.B
    shift = jnp.concatenate([shr, shl], axis=-1).reshape(B, R_COLS)
    if cfg.inject:
        conv = ust.reshape(B, cfg.T, FF_PAD)[:, cfg.T - (CONV_W - 1):, :D_FF]
    else:
        tps = cfg.T // cfg.tm
        conv = ust[tps - 1::tps, :, :D_FF]
    return (shift, h_re.reshape(B, S5_GROUPS, S5_STATE), h_im.reshape(B, S5_GROUPS, S5_STATE), conv)


PROMPT_PAD = 0
SAMPLE_PAD = 4


def kernel(x_prompt, x_sample, state_rwkv, state_rwkv_shift, state_s5_re, state_s5_im, state_hgrn, state_gla, state_ffn_conv, meta_tokens, norm_mix, w_in, w_out, rwkv_mu, rwkv_w0, rwkv_w_up, rwkv_a0, rwkv_a_up, rwkv_g_up, rwkv_k_k, rwkv_k_a, rwkv_r_k, rwkv_ln, s5_A_re, s5_A_im, s5_log_dt, s5_B_re, s5_B_im, s5_C_re, s5_C_im, s5_D, s5_w_glu, s5_b_glu, hgrn_lower_bounds, hgrn_norm, gla_gk_up, gla_gk_b, gla_norm, norm_ffn, ffn_w_up, ffn_conv_w, ffn_conv_b, ffn_w_down, norm_final):
    L = DEPTH
    Bp, Sp, D = x_prompt.shape
    Bs, Ss, _ = x_sample.shape
    Tp = N_META + Sp
    Ts = SAMPLE_PAD + Ss
    raw = dict(norm_mix=norm_mix, w_in=w_in, w_out=w_out, rwkv_mu=rwkv_mu, rwkv_w0=rwkv_w0, rwkv_w_up=rwkv_w_up,
               rwkv_a0=rwkv_a0, rwkv_a_up=rwkv_a_up, rwkv_g_up=rwkv_g_up, rwkv_k_k=rwkv_k_k, rwkv_k_a=rwkv_k_a,
               rwkv_r_k=rwkv_r_k, rwkv_ln=rwkv_ln, s5_A_re=s5_A_re, s5_A_im=s5_A_im, s5_log_dt=s5_log_dt,
               s5_B_re=s5_B_re, s5_B_im=s5_B_im, s5_C_re=s5_C_re, s5_C_im=s5_C_im, s5_D=s5_D, s5_w_glu=s5_w_glu,
               s5_b_glu=s5_b_glu, hgrn_lower_bounds=hgrn_lower_bounds, hgrn_norm=hgrn_norm, gla_gk_up=gla_gk_up,
               gla_gk_b=gla_gk_b, gla_norm=gla_norm, norm_ffn=norm_ffn, ffn_w_up=ffn_w_up, ffn_conv_w=ffn_conv_w,
               ffn_conv_b=ffn_conv_b, ffn_w_down=ffn_w_down)
    prm = _prep_params(raw, (PROMPT_PAD, SAMPLE_PAD))

    cfg_p = _Cfg(B=Bp, T=Tp, t0=PROMPT_PAD, tm=Tp // 3, g_mix=Bp, u_rwkv=4, c_rwkv=48, cb_gla=48, c_gla=48,
                 s5_split=False, g_s5=1, c_s5=Tp // 3, inject=False)
    cfg_s = _Cfg(B=Bs, T=Ts, t0=SAMPLE_PAD, tm=Bs * Ts, g_mix=16, u_rwkv=4, c_rwkv=Ts, cb_gla=Ts, c_gla=Ts,
                 s5_split=True, g_s5=32, c_s5=Ts, inject=True)

    dt = x_prompt.dtype
    meta = jnp.broadcast_to(meta_tokens.astype(dt)[None], (Bp, N_META, D))
    xp = jnp.concatenate([meta, x_prompt], axis=1).reshape(Bp * Tp, D)
    xs = jnp.pad(x_sample, ((0, 0), (SAMPLE_PAD, 0), (0, 0))).reshape(Bs * Ts, D)

    W = R_WIDTH
    st_p = (jnp.zeros((1, Bp, 1, 3 * W), F32), jnp.zeros((1, Bp, 1, R_LOW), F32),
            jnp.zeros((1, Bp, R_HEADS, R_HEAD_DIM, R_HEAD_DIM), F32),
            jnp.zeros((1, Bp, 1, S5_FLAT), F32), jnp.zeros((1, Bp, 1, S5_FLAT), F32),
            jnp.zeros((1, Bp, H_HEADS, H_HEAD_DIM, H_HEAD_DIM), F32),
            jnp.zeros((1, Bp, G_HEADS, G_KEY_DIM, G_VAL_DIM), F32),
            jnp.zeros((1, Bp, CONV_W - 1, FF_PAD), F32))
    buf_s = jnp.pad(state_ffn_conv, ((0, 0), (0, 0), (SAMPLE_PAD - (CONV_W - 1), Ss), (0, FF_PAD - D_FF)))
    st_s = (state_rwkv_shift[:, :, None, 0:3 * W], state_rwkv_shift[:, :, None, 3 * W:], state_rwkv,
            state_s5_re.reshape(L, Bs, 1, S5_FLAT), state_s5_im.reshape(L, Bs, 1, S5_FLAT),
            state_hgrn, state_gla, buf_s.reshape(L, Bs * Ts, FF_PAD))

    def acc_init(B):
        return (jnp.zeros((L, B, R_HEADS, R_HEAD_DIM, R_HEAD_DIM), F32),
                jnp.zeros((L, B, H_HEADS, H_HEAD_DIM, H_HEAD_DIM), F32),
                jnp.zeros((L, B, G_HEADS, G_KEY_DIM, G_VAL_DIM), F32))

    acc_p, acc_s = acc_init(Bp), acc_init(Bs)
    outs_p, outs_s = [], []
    for l in range(L):
        xp, new_p, acc_p = _layer(xp, st_p, acc_p, 0, prm, l, cfg_p)
        xs, new_s, acc_s = _layer(xs, st_s, acc_s, l, prm, l, cfg_s)
        outs_p.append(_state_outputs(new_p, cfg_p))
        outs_s.append(_state_outputs(new_s, cfg_s))

    g_fin = norm_final.reshape(1, D).astype(F32)
    y_prompt = _final_norm_skip(xp.reshape(Bp, Tp, D), g_fin, N_META, 512)
    y_sample = _final_norm(xs, g_fin, cfg_s.tm).reshape(Bs, Ts, D)[:, SAMPLE_PAD:]
    sp = [jnp.stack([o[i] for o in outs_p]) for i in range(4)]
    ss = [jnp.stack([o[i] for o in outs_s]) for i in range(4)]
    return (y_prompt, y_sample, acc_p[0], acc_s[0], sp[0], ss[0], sp[1], ss[1], sp[2], ss[2],
            acc_p[1], acc_s[1], acc_p[2], acc_s[2], sp[3], ss[3])
```

```python
import functools
import math
from typing import NamedTuple

import jax
import jax.numpy as jnp
from jax import lax
from jax.experimental import pallas as pl
from jax.experimental.pallas import tpu as pltpu

F32 = jnp.float32
BF16 = jnp.bfloat16

D_MODEL = 2048
DEPTH = 4
N_META = 16
GROUP_WIDTH = D_MODEL // 4
EPS = 1e-6
R_HEAD_DIM = 64
R_HEADS = GROUP_WIDTH // R_HEAD_DIM
R_WIDTH = R_HEADS * R_HEAD_DIM
R_DECAY_RANK = 64
R_ICL_RANK = 64
R_GATE_RANK = 128
R_LOW = R_DECAY_RANK + R_ICL_RANK + R_GATE_RANK
R_COLS = 3 * R_WIDTH + R_LOW
RWKV_DECAY_SCALE = 0.606531
RWKV_GN_EPS = 64e-5
S5_GROUP_CH = 16
S5_GROUPS = GROUP_WIDTH // S5_GROUP_CH
S5_WIDTH = S5_GROUPS * S5_GROUP_CH
S5_STATE = 64
S5_FLAT = S5_GROUPS * S5_STATE
H_HEAD_DIM = 128
H_HEADS = GROUP_WIDTH // H_HEAD_DIM
H_WIDTH = H_HEADS * H_HEAD_DIM
HGRN_MAX_INPUT = 1.0 - 1e-4
G_VAL_DIM = 128
G_HEADS = GROUP_WIDTH // G_VAL_DIM
G_KEY_DIM = G_VAL_DIM // 2
G_WIDTH = G_HEADS * G_VAL_DIM
G_QK = G_HEADS * G_KEY_DIM
G_GATE_RANK = 16
GLA_GATE_NORM = 16.0
GLA_SAFE_DECAY = 80.0
D_FF = ((8 * D_MODEL // 3 + 127) // 128) * 128
CONV_W = 3

P_R = 0
P_S5 = 3 * R_WIDTH
P_H = P_S5 + S5_WIDTH
P_GV = P_H + 4 * H_WIDTH
P_GG = P_GV + G_WIDTH
G_KEY_PAD = G_VAL_DIM
G_QK_PAD = G_HEADS * G_KEY_PAD
P_GQ = P_GG + G_WIDTH
P_GK = P_GQ + G_QK_PAD
P_RL = P_GK + G_QK_PAD
P_GL = P_RL + R_LOW
GL_PAD = 256
P_COLS = P_GL + GL_PAD
FF_PAD = 5632
S5_CHUNKS = 4
S5_SCAN = 8
FFN_COL_BLOCK = 256

VMEM_LIMIT = 56 * 1024 * 1024

NT = (((1,), (1,)), ((), ()))
TN = (((0,), (0,)), ((), ()))


class _Cfg(NamedTuple):
    B: int
    T: int
    t0: int
    tm: int
    g_mix: int
    u_rwkv: int
    c_rwkv: int
    cb_gla: int
    c_gla: int
    s5_split: bool
    g_s5: int
    c_s5: int
    inject: bool


def _dot(a, b):
    return jnp.dot(a, b, preferred_element_type=F32)


def _dg(a, b, dims):
    return lax.dot_general(a, b, dims, preferred_element_type=F32)


def _gelu(x):
    c = math.sqrt(2.0 / math.pi)
    return x * (0.5 * (1.0 + jnp.tanh(c * (x + 0.044715 * (x * x * x)))))


def _tloc(G, C, W):
    t = lax.broadcasted_iota(jnp.int32, (C, W), 0)
    return t if G == 1 else jnp.concatenate([t] * G, axis=0)


def _cumsum_rows(x, tl, C):
    sh = 1
    while sh < C:
        x = x + jnp.where(tl >= sh, pltpu.roll(x, sh, 0), 0.0)
        sh *= 2
    return x


def _segsum(x, e):
    x1 = x.astype(BF16)
    x2 = (x - x1.astype(F32)).astype(BF16)
    return _dot(x1, e) + _dot(x2, e)


def _params(sem):
    return pltpu.CompilerParams(dimension_semantics=sem, vmem_limit_bytes=VMEM_LIMIT)


def _norm_matmul_kernel(x_ref, g_ref, w_ref, o_ref, h_scr):
    @pl.when(pl.program_id(1) == 0)
    def _():
        x = x_ref[...]
        r = lax.rsqrt(jnp.mean(x * x, axis=-1, keepdims=True) + EPS)
        h_scr[...] = (x * r * g_ref[...]).astype(BF16)

    o_ref[...] = _dg(h_scr[...], w_ref[...], NT)


def _norm_matmul(x, g_all, w_all, l, tm, tn):
    M, D = x.shape
    N = w_all.shape[1]
    return pl.pallas_call(
        _norm_matmul_kernel,
        grid=(M // tm, N // tn),
        in_specs=[
            pl.BlockSpec((tm, D), lambda i, j: (i, 0)),
            pl.BlockSpec((None, 1, D), lambda i, j: (l, 0, 0)),
            pl.BlockSpec((None, tn, D), lambda i, j: (l, j, 0)),
        ],
        out_specs=pl.BlockSpec((tm, tn), lambda i, j: (i, j)),
        out_shape=jax.ShapeDtypeStruct((M, N), F32),
        scratch_shapes=[pltpu.VMEM((tm, D), BF16)],
        compiler_params=_params(("arbitrary", "arbitrary")),
        name="in_proj",
    )(x, g_all, w_all)


def _out_proj_kernel(x_ref, y0, y1, y2, y3, w0, w1, w2, w3, o_ref):
    acc = _dot(y0[...].astype(BF16), w0[...])
    acc = acc + _dot(y1[...].astype(BF16), w1[...])
    acc = acc + _dot(y2[...].astype(BF16), w2[...])
    acc = acc + _dot(y3[...].astype(BF16), w3[...])
    o_ref[...] = x_ref[...] + acc


def _out_proj(x, ys, w_all, l, tm, tn):
    M, D = x.shape
    W = GROUP_WIDTH
    yspec = pl.BlockSpec((tm, W), lambda i, j: (i, 0))
    wspecs = [pl.BlockSpec((None, W, tn), functools.partial(lambda i, j, q: (l, q, j), q=q)) for q in range(4)]
    return pl.pallas_call(
        _out_proj_kernel,
        grid=(M // tm, D // tn),
        in_specs=[pl.BlockSpec((tm, tn), lambda i, j: (i, j))] + [yspec] * 4 + wspecs,
        out_specs=pl.BlockSpec((tm, tn), lambda i, j: (i, j)),
        out_shape=jax.ShapeDtypeStruct((M, D), F32),
        compiler_params=_params(("arbitrary", "arbitrary")),
        name="out_proj",
    )(x, *ys, w_all, w_all, w_all, w_all)


def _ffn_up_kernel(x_ref, g_ref, wu_ref, wg_ref, cw_ref, cb_ref, buf_ref, act_ref, ust_ref, h_scr, *car,
                   tm, tn, tiles_per_seq, t0, seq_rows, inject):
    i = pl.program_id(0)
    j = pl.program_id(1)

    @pl.when(j == 0)
    def _():
        x = x_ref[...]
        r = lax.rsqrt(jnp.mean(x * x, axis=-1, keepdims=True) + EPS)
        h_scr[...] = (x * r * g_ref[...]).astype(BF16)

    if not inject:
        (car_scr,) = car

        @pl.when(i % tiles_per_seq == 0)
        def _():
            car_scr[j, 0:2, :] = buf_ref[...]

    nb = FFN_COL_BLOCK
    blocks = [slice(nb * q, nb * (q + 1)) for q in range(tn // nb)]
    h = h_scr[...]
    us = [_dot(h, wu_ref[:, cs]) for cs in blocks]
    gts = [_dot(h, wg_ref[:, cs]) for cs in blocks]
    rows = lax.broadcasted_iota(jnp.int32, (tm, nb), 0)
    for cs, u, gt in zip(blocks, us, gts):
        if inject:
            tl = rows & (seq_rows - 1)
            u = jnp.where((tl == t0 - 2) | (tl == t0 - 1), buf_ref[:, cs], u)
            p1 = pltpu.roll(u, 1, 0)
            p2 = pltpu.roll(u, 2, 0)
            ust_ref[:, cs] = u
        else:
            c0 = car_scr[j, 0:1, cs]
            c1 = car_scr[j, 1:2, cs]
            p1 = jnp.where(rows == 0, c1, pltpu.roll(u, 1, 0))
            p2 = jnp.where(rows == 0, c0, jnp.where(rows == 1, c1, pltpu.roll(u, 2, 0)))
            last = u[tm - 2:tm, :]
            car_scr[j, 0:2, cs] = last
            ust_ref[:, cs] = last
        c = cb_ref[:, cs] + cw_ref[0:1, cs] * p2
        c = c + cw_ref[1:2, cs] * p1
        c = c + cw_ref[2:3, cs] * u
        act_ref[:, cs] = (_gelu(c) * gt).astype(BF16)


def _ffn_up(x, g_all, w_all, cw_all, cb_all, buf, l, l_in, cfg, tn):
    M, D = x.shape
    tm = cfg.tm
    N = w_all.shape[-1] // 2
    nj = N // tn
    tps = cfg.T // tm if not cfg.inject else 1
    kern = functools.partial(_ffn_up_kernel, tm=tm, tn=tn, tiles_per_seq=tps, t0=cfg.t0, seq_rows=cfg.T,
                             inject=cfg.inject)
    if cfg.inject:
        buf_spec = pl.BlockSpec((None, tm, tn), lambda i, j: (l_in, i, j))
        ust_spec = pl.BlockSpec((tm, tn), lambda i, j: (i, j))
        ust_shape = jax.ShapeDtypeStruct((M, N), F32)
        scratch = [pltpu.VMEM((tm, D), BF16)]
    else:
        buf_spec = pl.BlockSpec((None, None, CONV_W - 1, tn), lambda i, j: (l_in, i // tps, 0, j))
        ust_spec = pl.BlockSpec((None, CONV_W - 1, tn), lambda i, j: (i, 0, j))
        ust_shape = jax.ShapeDtypeStruct((M // tm, CONV_W - 1, N), F32)
        scratch = [pltpu.VMEM((tm, D), BF16), pltpu.VMEM((N // tn, 8, tn), F32)]
    return pl.pallas_call(
        kern,
        grid=(M // tm, N // tn),
        in_specs=[
            pl.BlockSpec((tm, D), lambda i, j: (i, 0)),
            pl.BlockSpec((None, 1, D), lambda i, j: (l, 0, 0)),
            pl.BlockSpec((None, D, tn), lambda i, j: (l, 0, j)),
            pl.BlockSpec((None, D, tn), lambda i, j: (l, 0, j + nj)),
            pl.BlockSpec((None, CONV_W, tn), lambda i, j: (l, 0, j)),
            pl.BlockSpec((None, 1, tn), lambda i, j: (l, 0, j)),
            buf_spec,
        ],
        out_specs=[pl.BlockSpec((tm, tn), lambda i, j: (i, j)), ust_spec],
        out_shape=[jax.ShapeDtypeStruct((M, N), BF16), ust_shape],
        scratch_shapes=scratch,
        compiler_params=_params(("arbitrary", "arbitrary")),
        name="ffn_up",
    )(x, g_all, w_all, w_all, cw_all, cb_all, buf)


def _ffn_down_kernel(x_ref, act_ref, w_ref, o_ref):
    o_ref[...] = x_ref[...] + _dot(act_ref[...], w_ref[...])


def _ffn_down(x, act, w_all, l, tm, tn):
    M, D = x.shape
    K = w_all.shape[1]
    return pl.pallas_call(
        _ffn_down_kernel,
        grid=(M // tm, D // tn),
        in_specs=[
            pl.BlockSpec((tm, tn), lambda i, j: (i, j)),
            pl.BlockSpec((tm, K), lambda i, j: (i, 0)),
            pl.BlockSpec((None, K, tn), lambda i, j: (l, 0, j)),
        ],
        out_specs=pl.BlockSpec((tm, tn), lambda i, j: (i, j)),
        out_shape=jax.ShapeDtypeStruct((M, D), F32),
        compiler_params=_params(("arbitrary", "arbitrary")),
        name="ffn_down",
    )(x, act, w_all)


def _final_norm_kernel(x_ref, g_ref, o_ref):
    x = x_ref[...]
    r = lax.rsqrt(jnp.mean(x * x, axis=-1, keepdims=True) + EPS)
    o_ref[...] = x * r * g_ref[...]


def _final_norm_skip_kernel(x_ref, g_ref, o_ref):
    x = x_ref[0]
    r = lax.rsqrt(jnp.mean(x * x, axis=-1, keepdims=True) + EPS)
    o_ref[0] = x * r * g_ref[...]


def _final_norm_skip(x3, g, skip, tr):
    B, T, D = x3.shape
    assert skip % 8 == 0 and tr % 8 == 0 and (T - skip) % tr == 0
    return pl.pallas_call(
        _final_norm_skip_kernel,
        grid=(B, (T - skip) // tr),
        in_specs=[pl.BlockSpec((pl.Element(1), pl.Element(tr), pl.Element(D)),
                               lambda b, i: (b, (skip // 8 + i * (tr // 8)) * 8, 0)),
                  pl.BlockSpec((1, D), lambda b, i: (0, 0))],
        out_specs=pl.BlockSpec((1, tr, D), lambda b, i: (b, i, 0)),
        out_shape=jax.ShapeDtypeStruct((B, T - skip, D), F32),
        compiler_params=_params(("arbitrary", "arbitrary")),
        name="final_norm_skip",
    )(x3, g)


def _final_norm(x, g, tm):
    M, D = x.shape
    return pl.pallas_call(
        _final_norm_kernel,
        grid=(M // tm,),
        in_specs=[pl.BlockSpec((tm, D), lambda i: (i, 0)), pl.BlockSpec((1, D), lambda i: (0, 0))],
        out_specs=pl.BlockSpec((tm, D), lambda i: (i, 0)),
        out_shape=jax.ShapeDtypeStruct((M, D), F32),
        compiler_params=_params(("arbitrary",)),
        name="final_norm",
    )(x, g)


def _rwkv_kernel(prkv_ref, plow_ref, shr_in, shl_in, s0_ref, mur_ref, mul_ref, w0_ref, wup_ref, a0_ref, aup_ref,
                 gup_ref, kk_ref, ka_ref, rk_ref, ln_ref, e_ref, acc_ref,
                 y_ref, sout_ref, shr_out, shl_out,
                 at_scr, rt_scr, kt_scr, bt_scr, v_scr, gam_scr, y_scr, *, G, C, t0, U):
    R = G * C
    W = R_WIDTH

    @pl.when(pl.program_id(1) == 0)
    def _():
        sout_ref[...] = s0_ref[...]
        shr_out[...] = shr_in[...]
        shl_out[...] = shl_in[...]

    def token_shift(p_ref, car_ref, mu_ref, w):
        p = p_ref[...].reshape(R, w)
        tl = _tloc(G, C, w)
        carry = jnp.concatenate([jnp.broadcast_to(car_ref[g], (C, w)) for g in range(G)], axis=0)
        prev = jnp.where(tl == t0, carry, pltpu.roll(p, 1, 0))
        for g in range(G):
            car_ref[g] = p[g * C + C - 1:g * C + C, :]
        return p + (prev - p) * mu_ref[...]

    ps = token_shift(prkv_ref, shr_out, mur_ref, 3 * W)
    lo = token_shift(plow_ref, shl_out, mul_ref, R_LOW)
    r = ps[:, 0:W]
    k = ps[:, W:2 * W]
    v = ps[:, 2 * W:3 * W]
    lw = -RWKV_DECAY_SCALE * jax.nn.sigmoid(w0_ref[...] + _dot(jnp.tanh(lo).astype(BF16), wup_ref[...]))
    a = jax.nn.sigmoid(a0_ref[...] + _dot(lo.astype(BF16), aup_ref[...]))
    gate = _dot(jax.nn.sigmoid(lo).astype(BF16), gup_ref[...])
    e = e_ref[...]
    kk = k * kk_ref[...]
    kk = kk / jnp.maximum(jnp.sqrt(_segsum(kk * kk, e)), 1e-12)
    k2 = k * (1.0 + (a - 1.0) * ka_ref[...])
    ka = kk * a
    tl = _tloc(G, C, W)
    if t0 > 0:
        valid = tl >= t0
        lw = jnp.where(valid, lw, 0.0)
        kk = jnp.where(valid, kk, 0.0)
        ka = jnp.where(valid, ka, 0.0)
        k2 = jnp.where(valid, k2, 0.0)
        v = jnp.where(valid, v, 0.0)
    cum = _cumsum_rows(lw, tl, C)
    gam = jnp.exp(cum)
    inv = jnp.exp(-cum)
    at_scr[...] = kk * jnp.exp(cum - lw)
    rt_scr[...] = r * gam
    kt_scr[...] = k2 * inv
    bt_scr[...] = ka * inv
    v_scr[...] = v
    gam_scr[...] = gam

    row = lax.broadcasted_iota(jnp.int32, (C, C), 0)
    col = lax.broadcasted_iota(jnp.int32, (C, C), 1)
    low_s = row > col
    low_i = row >= col

    def seq_body(it, carry):
        gs = [it * U + i for i in range(U)]
        lanes = [(i, h) for i in range(U) for h in range(R_HEADS)]
        data = []
        for g in gs:
            off = pl.multiple_of(g * C, 8)
            rows = pl.ds(off, C)
            data.append((at_scr[rows, :], rt_scr[rows, :], kt_scr[rows, :], bt_scr[rows, :], v_scr[rows, :],
                         gam_scr[pl.ds(off + C - 8, 8), :][7:8, :]))
        s_in = [sout_ref[gs[i], h] for i, h in lanes]

        def head(x, h):
            return x[:, R_HEAD_DIM * h:R_HEAD_DIM * (h + 1)]

        ar = [jnp.concatenate([head(data[i][0], h), head(data[i][1], h)], axis=0).astype(BF16) for i, h in lanes]
        ktb = [head(data[i][2], h).astype(BF16) for i, h in lanes]
        btb = [head(data[i][3], h).astype(BF16) for i, h in lanes]
        vb = [head(data[i][4], h).astype(BF16) for i, h in lanes]
        n = range(len(lanes))
        m_k = [_dg(ar[j], ktb[j], NT) for j in n]
        m_b = [_dg(ar[j], btb[j], NT) for j in n]
        n_pow = [jnp.where(low_s, m_b[j][0:C], 0.0) for j in n]
        acc = [-n_pow[j] for j in n]
        pre_rhs = [_dot(jnp.where(low_s, m_k[j][0:C], 0.0).astype(BF16), vb[j]) for j in n]
        pre_y = [_dot(jnp.where(low_i, m_k[j][C:2 * C], 0.0).astype(BF16), vb[j]) for j in n]
        pre_ds = [_dg(vb[j], ktb[j], TN) for j in n]
        trb = [jnp.where(low_i, m_b[j][C:2 * C], 0.0).astype(BF16) for j in n]
        span = 2
        nb = [n_pow[j].astype(BF16) for j in n]
        n_pow = [_dot(nb[j], nb[j]) for j in n]
        while span < C:
            nb = [n_pow[j].astype(BF16) for j in n]
            span *= 2
            prod = [_dot(acc[j].astype(BF16), nb[j]) for j in n]
            if span < C:
                nxt = [_dot(nb[j], nb[j]) for j in n]
            acc = [acc[j] + n_pow[j] + prod[j] for j in n]
            if span < C:
                n_pow = nxt
        accb = [acc[j].astype(BF16) for j in n]
        m_s = [_dg(ar[j], s_in[j].astype(BF16), NT) for j in n]
        rhs = [m_s[j][0:C] + pre_rhs[j] for j in n]
        u = [rhs[j] + _dot(accb[j], rhs[j].astype(BF16)) for j in n]
        ub = [u[j].astype(BF16) for j in n]
        yh = [m_s[j][C:2 * C] + pre_y[j] - _dot(trb[j], ub[j]) for j in n]
        ds = [pre_ds[j] - _dg(ub[j], btb[j], TN) for j in n]
        for j, (i, h) in enumerate(lanes):
            sout_ref[gs[i], h] = (s_in[j] + ds[j]) * head(data[i][5], h)
        for i, g in enumerate(gs):
            rows = pl.ds(pl.multiple_of(g * C, 8), C)
            y_scr[rows, :] = jnp.concatenate(yh[i * R_HEADS:(i + 1) * R_HEADS], axis=1)
        return carry

    lax.fori_loop(0, G // U, seq_body, 0)

    y = y_scr[...]
    inv_n = 1.0 / R_HEAD_DIM
    d = y - _segsum(y, e) * inv_n
    var = _segsum(d * d, e) * inv_n
    y = d * lax.rsqrt(var + RWKV_GN_EPS) * ln_ref[...]
    y = y + _segsum(r * k2 * rk_ref[...], e) * v
    y_ref[...] = (y * gate).reshape(G, C, W)


def _rwkv(p3, st, acc, l_in, prm, l, cfg):
    B, T, _ = p3.shape
    G, C = cfg.g_mix, cfg.c_rwkv
    W = R_WIDTH
    shr_in, shl_in, s0 = st

    def pspec(w, off):
        return pl.BlockSpec((G, C, w), lambda i, c: (i, c, off // w))

    def lay(*shape):
        return pl.BlockSpec((None,) + shape, lambda i, c: (l,) + (0,) * len(shape))

    kern = functools.partial(_rwkv_kernel, G=G, C=C, t0=cfg.t0, U=cfg.u_rwkv)
    R = G * C
    return pl.pallas_call(
        kern,
        grid=(B // G, T // C),
        in_specs=[
            pspec(3 * W, P_R), pspec(R_LOW, P_RL),
            pl.BlockSpec((None, G, 1, 3 * W), lambda i, c: (l_in, i, 0, 0)),
            pl.BlockSpec((None, G, 1, R_LOW), lambda i, c: (l_in, i, 0, 0)),
            pl.BlockSpec((None, G, R_HEADS, R_HEAD_DIM, R_HEAD_DIM), lambda i, c: (l_in, i, 0, 0, 0)),
            lay(1, 3 * W), lay(1, R_LOW), lay(1, W), lay(R_LOW, W), lay(1, W), lay(R_LOW, W), lay(R_LOW, W),
            lay(1, W), lay(1, W), lay(1, W), lay(1, W),
            pl.BlockSpec((W, W), lambda i, c: (0, 0)),
            pl.BlockSpec(memory_space=pl.ANY),
        ],
        input_output_aliases={17: 1},
        out_specs=[
            pl.BlockSpec((G, C, W), lambda i, c: (i, c, 0)),
            pl.BlockSpec((None, G, R_HEADS, R_HEAD_DIM, R_HEAD_DIM), lambda i, c: (l, i, 0, 0, 0)),
            pl.BlockSpec((G, 1, 3 * W), lambda i, c: (i, 0, 0)),
            pl.BlockSpec((G, 1, R_LOW), lambda i, c: (i, 0, 0)),
        ],
        out_shape=[
            jax.ShapeDtypeStruct((B, T, W), F32),
            jax.ShapeDtypeStruct(acc.shape, F32),
            jax.ShapeDtypeStruct((B, 1, 3 * W), F32),
            jax.ShapeDtypeStruct((B, 1, R_LOW), F32),
        ],
        scratch_shapes=[pltpu.VMEM((R, W), F32)] * 7,
        compiler_params=_params(("arbitrary", "arbitrary")),
        name="rwkv7",
    )(p3, p3, shr_in, shl_in, s0, prm["mu_rkv"], prm["mu_low"], prm["rwkv_w0"], prm["rwkv_w_up"], prm["rwkv_a0"],
      prm["rwkv_a_up"], prm["rwkv_g_up"], prm["rwkv_k_k"], prm["rwkv_k_a"], prm["rwkv_r_k"], prm["rwkv_ln"],
      prm["e_head"], acc)


def _s5_kernel(u_ref, h0r_ref, h0i_ref, bbhr_ref, bblr_ref, bbhi_ref, bbli_ref, ccr_ref, cci_ref, ap_ref, pw_ref,
               d_ref, wglu_ref, bglu_ref,
               y_ref, hr_out, hi_out, hre_scr, him_scr, *, G, C, t0, split):
    R = G * C
    NB = S5_FLAT // S5_CHUNKS
    UB = S5_WIDTH // S5_CHUNKS

    @pl.when(pl.program_id(1) == 0)
    def _():
        hr_out[...] = h0r_ref[...]
        hi_out[...] = h0i_ref[...]

    u = u_ref[...].reshape(R, S5_WIDTH)
    uh = u.astype(BF16)
    ul = (u - uh.astype(F32)).astype(BF16)
    valid = _tloc(G, C, NB) >= t0
    for j in range(S5_CHUNKS):
        us = slice(UB * j, UB * (j + 1))
        br = _dot(uh[:, us], bbhr_ref[j])
        bi = _dot(uh[:, us], bbhi_ref[j])
        if split:
            br = br + _dot(ul[:, us], bbhr_ref[j]) + _dot(uh[:, us], bblr_ref[j])
            bi = bi + _dot(ul[:, us], bbhi_ref[j]) + _dot(uh[:, us], bbli_ref[j])
        if t0 > 0:
            br = jnp.where(valid, br, 0.0)
            bi = jnp.where(valid, bi, 0.0)
        hre_scr[:, NB * j:NB * (j + 1)] = br
        him_scr[:, NB * j:NB * (j + 1)] = bi

    steps = [(1 << k, ap_ref[S5_SCAN * 2 * k:S5_SCAN * (2 * k + 1), :],
              ap_ref[S5_SCAN * (2 * k + 1):S5_SCAN * (2 * k + 2), :]) for k in range(3)]
    pwr = pw_ref[0:S5_SCAN, :]
    pwi = pw_ref[S5_SCAN:2 * S5_SCAN, :]

    def seq_body(g, carry):
        def tile_body(i, hc):
            hcr, hci = hc
            rows = pl.ds(pl.multiple_of(g * C + i * S5_SCAN, S5_SCAN), S5_SCAN)
            xr = hre_scr[rows, :]
            xi = him_scr[rows, :]
            for sh, ar, ai in steps:
                sr = pltpu.roll(xr, sh, 0)
                si = pltpu.roll(xi, sh, 0)
                xr, xi = xr + ar * sr - ai * si, xi + ar * si + ai * sr
            xr, xi = xr + pwr * hcr - pwi * hci, xi + pwr * hci + pwi * hcr
            hre_scr[rows, :] = xr
            him_scr[rows, :] = xi
            return xr[S5_SCAN - 1:S5_SCAN, :], xi[S5_SCAN - 1:S5_SCAN, :]

        hcr, hci = lax.fori_loop(0, C // S5_SCAN, tile_body, (hr_out[g], hi_out[g]))
        hr_out[g] = hcr
        hi_out[g] = hci
        return carry

    lax.fori_loop(0, G, seq_body, 0)

    ys = []
    for j in range(S5_CHUNKS):
        hs = slice(NB * j, NB * (j + 1))
        ys.append(_dot(hre_scr[:, hs].astype(BF16), ccr_ref[j]) - _dot(him_scr[:, hs].astype(BF16), cci_ref[j]))
    y = jnp.concatenate(ys, axis=1) + d_ref[...] * u
    y = _gelu(y)
    y = y * jax.nn.sigmoid(_dot(y.astype(BF16), wglu_ref[...]) + bglu_ref[...])
    y_ref[...] = y.reshape(G, C, S5_WIDTH)


def _s5(p3, st, l_in, prm, l, cfg):
    B, T, _ = p3.shape
    G, C = cfg.g_s5, cfg.c_s5
    h0r, h0i = st
    NB = S5_FLAT // S5_CHUNKS
    UB = S5_WIDTH // S5_CHUNKS

    def lay(*shape):
        return pl.BlockSpec((None,) + shape, lambda i, c: (l,) + (0,) * len(shape))

    st_spec = pl.BlockSpec((None, G, 1, S5_FLAT), lambda i, c: (l_in, i, 0, 0))
    out_st = pl.BlockSpec((G, 1, S5_FLAT), lambda i, c: (i, 0, 0))
    kern = functools.partial(_s5_kernel, G=G, C=C, t0=cfg.t0, split=cfg.s5_split)
    pw = prm["s5_pw"][cfg.t0]
    return pl.pallas_call(
        kern,
        grid=(B // G, T // C),
        in_specs=[
            pl.BlockSpec((G, C, S5_WIDTH), lambda i, c: (i, c, P_S5 // S5_WIDTH)),
            st_spec, st_spec,
            lay(S5_CHUNKS, UB, NB), lay(S5_CHUNKS, UB, NB), lay(S5_CHUNKS, UB, NB), lay(S5_CHUNKS, UB, NB),
            lay(S5_CHUNKS, NB, UB), lay(S5_CHUNKS, NB, UB),
            lay(6 * S5_SCAN, S5_FLAT), lay(2 * S5_SCAN, S5_FLAT),
            lay(1, S5_WIDTH), lay(S5_WIDTH, S5_WIDTH), lay(1, S5_WIDTH),
        ],
        out_specs=[pl.BlockSpec((G, C, S5_WIDTH), lambda i, c: (i, c, 0)), out_st, out_st],
        out_shape=[
            jax.ShapeDtypeStruct((B, T, S5_WIDTH), F32),
            jax.ShapeDtypeStruct((B, 1, S5_FLAT), F32),
            jax.ShapeDtypeStruct((B, 1, S5_FLAT), F32),
        ],
        scratch_shapes=[pltpu.VMEM((G * C, S5_FLAT), F32)] * 2,
        compiler_params=_params(("arbitrary", "arbitrary")),
        name="s5",
    )(p3, h0r, h0i, prm["s5_bbh_re"], prm["s5_bbl_re"], prm["s5_bbh_im"], prm["s5_bbl_im"], prm["s5_cc_re"],
      prm["s5_cc_im"], prm["s5_apow"], pw, prm["s5_D"], prm["s5_w_glu"], prm["s5_b_glu"])


def _gla_core(q, kx, gdec, v, s0_ref, sout_ref, st_scr, q_scr, k_scr, b_scr, v_scr, o_scr, *, G, CB, C, t0, K, KS):
    assert CB % C == 0 and (CB == C or C & (C - 1) == 0) and (t0 == 0 or CB == C)
    H = G_HEADS
    V = G_VAL_DIM
    HK = H * K
    c = pl.program_id(1)

    @pl.when(c == 0)
    def _():
        for g in range(G):
            for h in range(H):
                s = s0_ref[g, h].T
                if KS < K:
                    s = jnp.concatenate([s, jnp.zeros((V, K - KS), F32)], axis=1)
                st_scr[g, h] = s

    if CB == C:
        tl = _tloc(G, C, HK)
    else:
        tl = lax.broadcasted_iota(jnp.int32, (G * CB, HK), 0) & (C - 1)
    if t0 > 0:
        valid = tl >= t0
        gdec = jnp.where(valid, gdec, 0.0)
        kx = jnp.where(valid, kx, 0.0)
    q_scr[...] = q
    k_scr[...] = kx
    b_scr[...] = _cumsum_rows(gdec, tl, C)
    v_scr[...] = v
    ti = lax.broadcasted_iota(jnp.int32, (C, HK), 0)

    row = lax.broadcasted_iota(jnp.int32, (C, C), 0)
    col = lax.broadcasted_iota(jnp.int32, (C, C), 1)
    lanes = [(g, h) for g in range(G) for h in range(H)]

    def ksl(h):
        return slice(K * h, K * (h + 1))

    def vsl(h):
        return slice(V * h, V * (h + 1))

    def chunk_body(j, carry):
        rows = [pl.ds(pl.multiple_of(g * CB + j * C, 8), C) for g in range(G)]
        qg = [q_scr[r, :] for r in rows]
        kg = [k_scr[r, :] for r in rows]
        bg = [b_scr[r, :] for r in rows]
        vg = [v_scr[r, :] for r in rows]
        vb = [x.astype(BF16) for x in vg]
        b_last = [b[C - 1:C, :] for b in bg]
        qe = [(qg[g] * jnp.exp(bg[g])).astype(BF16) for g in range(G)]
        kd = [(kg[g] * jnp.exp(b_last[g] - bg[g])).astype(BF16) for g in range(G)]
        dec_last = [jnp.exp(b) for b in b_last]
        s_in = [st_scr[g, h] for g, h in lanes]
        inter = [_dg(qe[g][:, ksl(h)], s_in[n].astype(BF16), NT) for n, (g, h) in enumerate(lanes)]
        upd = [_dg(vb[g][:, vsl(h)], kd[g][:, ksl(h)], TN) for g, h in lanes]
        b_min = b_last[0]
        for b in b_last[1:]:
            b_min = jnp.minimum(b_min, b)

        def factored():
            kn = [(kg[g] * jnp.exp(-bg[g])).astype(BF16) for g in range(G)]
            att = [_dg(qe[g][:, ksl(h)], kn[g][:, ksl(h)], NT) for g, h in lanes]
            att = [jnp.where(row >= col, a, 0.0).astype(BF16) for a in att]
            outs = [_dot(att[n], vb[g][:, vsl(h)]) for n, (g, h) in enumerate(lanes)]
            return [jnp.concatenate(outs[g * H:(g + 1) * H], axis=1) for g in range(G)]

        def pairwise():
            res = []
            for g in range(G):
                base = g * CB + j * C

                def key_row(jj, intra, g=g, base=base):
                    kj = k_scr[pl.ds(base + jj, 1), :]
                    bj = b_scr[pl.ds(base + jj, 1), :]
                    vj = v_scr[pl.ds(base + jj, 1), :]
                    causal = ti >= jj
                    e = jnp.where(causal, qg[g] * kj * jnp.exp(jnp.where(causal, bg[g] - bj, 0.0)), 0.0)
                    return tuple(intra[h] + jnp.sum(e[:, ksl(h)], axis=-1, keepdims=True) * vj[:, vsl(h)]
                                 for h in range(H))

                intra = lax.fori_loop(0, C, key_row, tuple(jnp.zeros((C, V), F32) for _ in range(H)))
                res.append(jnp.concatenate(intra, axis=1))
            return res

        intra = lax.cond(jnp.min(b_min) >= -GLA_SAFE_DECAY, factored, pairwise)
        for n, (g, h) in enumerate(lanes):
            st_scr[g, h] = s_in[n] * dec_last[g][:, ksl(h)] + upd[n]
        for g in range(G):
            o_scr[rows[g], :] = jnp.concatenate(inter[g * H:(g + 1) * H], axis=1) + intra[g]
        return carry

    lax.fori_loop(0, CB // C, chunk_body, 0)

    @pl.when(c == pl.num_programs(1) - 1)
    def _():
        for g in range(G):
            for h in range(H):
                sout_ref[g, h] = st_scr[g, h][:, 0:KS].T


def _head_norm_gate(o, norm, gate_raw):
    V = G_VAL_DIM
    ys = []
    for h in range(G_HEADS):
        oh = o[:, V * h:V * (h + 1)]
        ys.append(oh * lax.rsqrt(jnp.mean(oh * oh, axis=-1, keepdims=True) + EPS))
    y = jnp.concatenate(ys, axis=1) * norm
    return y * (gate_raw * jax.nn.sigmoid(gate_raw))


def _hgrn_kernel(q_ref, f_ref, i_ref, g_ref, lb_ref, norm_ref, s0_ref, acc_ref, y_ref, sout_ref,
                 st_scr, q_scr, k_scr, b_scr, v_scr, o_scr, *, G, C, chunk, t0, layer):
    R = G * C
    W = H_WIDTH
    lbr = lb_ref[...]
    ex = jnp.exp(lbr - jnp.max(lbr, axis=0, keepdims=True))
    sm = ex / jnp.sum(ex, axis=0, keepdims=True)
    lb = jnp.zeros((1, W), F32)
    for i in range(1, layer + 1):
        lb = lb + sm[i:i + 1, :]
    qr = q_ref[...].reshape(R, W)
    kx = jnp.minimum((1.0 - lb) * jax.nn.sigmoid(-f_ref[...].reshape(R, W)), HGRN_MAX_INPUT)
    _gla_core(qr * jax.nn.sigmoid(qr), kx, jnp.log1p(-kx), i_ref[...].reshape(R, W), s0_ref, sout_ref,
              st_scr, q_scr, k_scr, b_scr, v_scr, o_scr, G=G, CB=C, C=chunk, t0=t0, K=H_HEAD_DIM, KS=H_HEAD_DIM)
    y_ref[...] = _head_norm_gate(o_scr[...], norm_ref[...], g_ref[...].reshape(R, W)).reshape(G, C, W)


def _gla_kernel(v_ref, gate_ref, q_ref, k_ref, gl_ref, gkup_ref, gkb_ref, norm_ref, s0_ref, acc_ref, y_ref,
                sout_ref,
                st_scr, q_scr, k_scr, b_scr, v_scr, o_scr, *, G, C, chunk, t0):
    R = G * C
    z = _dot(gl_ref[...].reshape(R, GL_PAD).astype(BF16), gkup_ref[...]) + gkb_ref[...]
    gdec = -(jnp.maximum(-z, 0.0) + jnp.log1p(jnp.exp(-jnp.abs(z)))) / GLA_GATE_NORM
    _gla_core(q_ref[...].reshape(R, G_QK_PAD) * (G_KEY_DIM ** -0.5), k_ref[...].reshape(R, G_QK_PAD), gdec,
              v_ref[...].reshape(R, G_WIDTH), s0_ref, sout_ref, st_scr, q_scr, k_scr, b_scr, v_scr, o_scr,
              G=G, CB=C, C=chunk, t0=t0, K=G_KEY_PAD, KS=G_KEY_DIM)
    y_ref[...] = _head_norm_gate(o_scr[...], norm_ref[...], gate_ref[...].reshape(R, G_WIDTH)).reshape(G, C, G_WIDTH)


def _gla_like(p3, s0, acc, l_in, prm, l, cfg, hgrn):
    B, T, _ = p3.shape
    G, C = cfg.g_mix, cfg.cb_gla
    H, V = G_HEADS, G_VAL_DIM
    K = H_HEAD_DIM if hgrn else G_KEY_PAD
    KS = H_HEAD_DIM if hgrn else G_KEY_DIM
    HK = H * K
    R = G * C

    def pspec(w, off):
        return pl.BlockSpec((G, C, w), lambda i, c: (i, c, off // w))

    def lay(*shape):
        return pl.BlockSpec((None,) + shape, lambda i, c: (l,) + (0,) * len(shape))

    st_in = pl.BlockSpec((None, G, H, KS, V), lambda i, c: (l_in, i, 0, 0, 0))
    if hgrn:
        kern = functools.partial(_hgrn_kernel, G=G, C=C, chunk=cfg.c_gla, t0=cfg.t0, layer=l)
        in_specs = [pspec(HK, P_H), pspec(HK, P_H + HK), pspec(HK, P_H + 2 * HK), pspec(HK, P_H + 3 * HK),
                    pl.BlockSpec((DEPTH, HK), lambda i, c: (0, 0)), lay(1, HK), st_in]
        args = (p3, p3, p3, p3, prm["hgrn_lb_raw"], prm["hgrn_norm"], s0)
        name = "hgrn2"
    else:
        kern = functools.partial(_gla_kernel, G=G, C=C, chunk=cfg.c_gla, t0=cfg.t0)
        in_specs = [pspec(G_WIDTH, P_GV), pspec(G_WIDTH, P_GG), pspec(HK, P_GQ), pspec(HK, P_GK),
                    pspec(GL_PAD, P_GL), lay(GL_PAD, HK), lay(1, HK), lay(1, G_WIDTH), st_in]
        args = (p3, p3, p3, p3, p3, prm["gla_gk_up"], prm["gla_gk_b"], prm["gla_norm"], s0)
        name = "gla"
    return pl.pallas_call(
        kern,
        grid=(B // G, T // C),
        in_specs=in_specs + [pl.BlockSpec(memory_space=pl.ANY)],
        out_specs=[pl.BlockSpec((G, C, H * V), lambda i, c: (i, c, 0)),
                   pl.BlockSpec((None, G, H, KS, V), lambda i, c: (l, i, 0, 0, 0))],
        out_shape=[jax.ShapeDtypeStruct((B, T, H * V), F32), jax.ShapeDtypeStruct(acc.shape, F32)],
        input_output_aliases={len(args): 1},
        scratch_shapes=[pltpu.VMEM((G, H, V, K), F32), pltpu.VMEM((R, HK), F32), pltpu.VMEM((R, HK), F32),
                        pltpu.VMEM((R, HK), F32), pltpu.VMEM((R, H * V), F32), pltpu.VMEM((R, H * V), F32)],
        compiler_params=_params(("arbitrary", "arbitrary")),
        name=name,
    )(*args, acc)


def _s5_tables(A_re, A_im, log_dt, B_re, B_im, C_re, C_im, t0s):
    L = A_re.shape[0]
    A_re = A_re.astype(F32)
    A_im = A_im.astype(F32)
    dt = jnp.exp(log_dt.astype(F32))[..., None]
    mag = jnp.exp(A_re * dt)
    ab_re = mag * jnp.cos(A_im * dt)
    ab_im = mag * jnp.sin(A_im * dt)
    den = A_re * A_re + A_im * A_im
    n_re = ab_re - 1.0
    co_re = (n_re * A_re + ab_im * A_im) / den
    co_im = (ab_im * A_re - n_re * A_im) / den
    B_re = B_re.astype(F32)
    B_im = B_im.astype(F32)
    bb_re = co_re[..., None] * B_re - co_im[..., None] * B_im
    bb_im = co_re[..., None] * B_im + co_im[..., None] * B_re
    gpc = S5_GROUPS // S5_CHUNKS
    eye = jnp.eye(gpc, dtype=F32)

    def block_in(bb):
        t = bb.reshape(L, S5_CHUNKS, gpc, S5_STATE, S5_GROUP_CH).transpose(0, 1, 2, 4, 3)
        t = t[:, :, :, :, None, :] * eye[None, None, :, None, :, None]
        return t.reshape(L, S5_CHUNKS, gpc * S5_GROUP_CH, gpc * S5_STATE)

    def block_out(cc):
        t = cc.astype(F32).reshape(L, S5_CHUNKS, gpc, S5_GROUP_CH, S5_STATE).transpose(0, 1, 2, 4, 3)
        t = t[:, :, :, :, None, :] * eye[None, None, :, None, :, None]
        return t.reshape(L, S5_CHUNKS, gpc * S5_STATE, gpc * S5_GROUP_CH)

    def split(x):
        hi = x.astype(BF16)
        return hi, (x - hi.astype(F32)).astype(BF16)

    bbh_re, bbl_re = split(block_in(bb_re))
    bbh_im, bbl_im = split(block_in(bb_im))

    def power(m):
        mg = jnp.exp(m * dt * A_re)
        return (mg * jnp.cos(m * dt * A_im)).reshape(L, S5_FLAT), (mg * jnp.sin(m * dt * A_im)).reshape(L, S5_FLAT)

    zero = jnp.zeros((L, S5_FLAT), F32)
    levels = []
    for sh in (1, 2, 4):
        pr, pi = power(float(sh))
        levels += [zero if r < sh else pr for r in range(S5_SCAN)]
        levels += [zero if r < sh else pi for r in range(S5_SCAN)]
    apow = jnp.stack(levels, axis=1)
    pws = {}
    for t0 in t0s:
        first = t0 % S5_SCAN
        res, ims = [], []
        for r in range(S5_SCAN):
            if r < first:
                res.append(zero)
                ims.append(zero)
            else:
                pr, pi = power(float(r - first + 1))
                res.append(pr)
                ims.append(pi)
        pws[t0] = jnp.stack(res + ims, axis=1)
    return dict(s5_bbh_re=bbh_re, s5_bbl_re=bbl_re, s5_bbh_im=bbh_im, s5_bbl_im=bbl_im,
                s5_cc_re=block_out(C_re).astype(BF16), s5_cc_im=block_out(C_im).astype(BF16),
                s5_apow=apow, s5_pw=pws)


def _prep_params(raw, t0s):
    L = raw["w_in"].shape[0]
    W = R_WIDTH
    w_in = raw["w_in"]
    o_s5 = R_COLS
    o_h = o_s5 + S5_WIDTH
    o_g = o_h + 4 * H_WIDTH
    o_gv = o_g + 2 * G_QK
    o_gl = o_gv + G_WIDTH
    o_gg = o_gl + G_GATE_RANK
    def key_pad(w):
        w = w.reshape(w.shape[:-1] + (G_HEADS, G_KEY_DIM))
        w = jnp.pad(w, ((0, 0),) * (w.ndim - 1) + ((0, G_KEY_PAD - G_KEY_DIM),))
        return w.reshape(w.shape[:-2] + (G_QK_PAD,))

    def key_pad_rows(w):
        w = w.reshape(L, G_HEADS, G_KEY_DIM, w.shape[-1])
        w = jnp.pad(w, ((0, 0), (0, 0), (0, G_KEY_PAD - G_KEY_DIM), (0, 0)))
        return w.reshape(L, G_QK_PAD, w.shape[-1])

    w_t = jnp.swapaxes(w_in, 1, 2)
    pad = jnp.zeros((L, GL_PAD - G_GATE_RANK, w_t.shape[-1]), w_in.dtype)
    w_in_p = jnp.concatenate([
        w_t[:, 0:3 * W], w_t[:, o_s5:o_h], w_t[:, o_h:o_g], w_t[:, o_gv:o_gl],
        w_t[:, o_gg:o_gg + G_WIDTH], key_pad_rows(w_t[:, o_g:o_g + G_QK]), key_pad_rows(w_t[:, o_g + G_QK:o_gv]),
        w_t[:, 3 * W:R_COLS], w_t[:, o_gl:o_gg], pad], axis=1).astype(BF16)
    assert w_in_p.shape[1] == P_COLS

    def row3(x):
        return x.reshape(L, 1, -1).astype(F32)

    def low_pad(w, off):
        return jnp.pad(w, ((0, 0), (off, R_LOW - off - w.shape[1]), (0, 0))).astype(BF16)

    ff = FF_PAD - D_FF
    w_up = raw["ffn_w_up"]
    head = jnp.arange(W) // R_HEAD_DIM
    prm = dict(
        norm_mix=row3(raw["norm_mix"]), norm_ffn=row3(raw["norm_ffn"]),
        w_in=w_in_p, w_out=raw["w_out"].astype(BF16),
        mu_rkv=row3(raw["rwkv_mu"][:, 0:3 * W]), mu_low=row3(raw["rwkv_mu"][:, 3 * W:]),
        rwkv_w0=row3(raw["rwkv_w0"]), rwkv_a0=row3(raw["rwkv_a0"]),
        rwkv_w_up=low_pad(raw["rwkv_w_up"], 0), rwkv_a_up=low_pad(raw["rwkv_a_up"], R_DECAY_RANK),
        rwkv_g_up=low_pad(raw["rwkv_g_up"], R_DECAY_RANK + R_ICL_RANK),
        rwkv_k_k=row3(raw["rwkv_k_k"]), rwkv_k_a=row3(raw["rwkv_k_a"]), rwkv_r_k=row3(raw["rwkv_r_k"]),
        rwkv_ln=row3(raw["rwkv_ln"]),
        e_head=(head[:, None] == head[None, :]).astype(BF16),
        s5_D=row3(raw["s5_D"]), s5_w_glu=raw["s5_w_glu"].astype(BF16), s5_b_glu=row3(raw["s5_b_glu"]),
        hgrn_lb_raw=raw["hgrn_lower_bounds"].astype(F32), hgrn_norm=row3(raw["hgrn_norm"]),
        gla_gk_up=jnp.pad(key_pad(raw["gla_gk_up"]), ((0, 0), (0, GL_PAD - G_GATE_RANK), (0, 0))).astype(BF16),
        gla_gk_b=row3(key_pad(raw["gla_gk_b"])), gla_norm=row3(raw["gla_norm"]),
        ffn_w=jnp.concatenate([w_up[..., :D_FF], jnp.zeros(w_up.shape[:2] + (ff,), w_up.dtype),
                               w_up[..., D_FF:], jnp.zeros(w_up.shape[:2] + (ff,), w_up.dtype)],
                              axis=-1).astype(BF16),
        ffn_conv_w=jnp.pad(raw["ffn_conv_w"], ((0, 0), (0, 0), (0, ff))).astype(F32),
        ffn_conv_b=jnp.pad(raw["ffn_conv_b"], ((0, 0), (0, ff))).reshape(L, 1, FF_PAD).astype(F32),
        ffn_w_down=raw["ffn_w_down"].astype(BF16),
    )
    prm.update(_s5_tables(raw["s5_A_re"], raw["s5_A_im"], raw["s5_log_dt"], raw["s5_B_re"], raw["s5_B_im"],
                          raw["s5_C_re"], raw["s5_C_im"], t0s))
    return prm


def _layer(x, st, acc, l_in, prm, l, cfg):
    shr, shl, s_rw, h_re, h_im, s_hg, s_gl, buf = st
    a_rw, a_hg, a_gl = acc
    p = _norm_matmul(x, prm["norm_mix"], prm["w_in"], l, cfg.tm, P_COLS // 4)
    p3 = p.reshape(cfg.B, cfg.T, P_COLS)
    y_r, a_rw, shr_n, shl_n = _rwkv(p3, (shr, shl, s_rw), a_rw, l_in, prm, l, cfg)
    y_s, h_re_n, h_im_n = _s5(p3, (h_re, h_im), l_in, prm, l, cfg)
    y_h, a_hg = _gla_like(p3, s_hg, a_hg, l_in, prm, l, cfg, True)
    y_g, a_gl = _gla_like(p3, s_gl, a_gl, l_in, prm, l, cfg, False)
    M = cfg.B * cfg.T
    ys = [y.reshape(M, GROUP_WIDTH) for y in (y_r, y_s, y_h, y_g)]
    x = _out_proj(x, ys, prm["w_out"], l, cfg.tm, 1024)
    act, ust = _ffn_up(x, prm["norm_ffn"], prm["ffn_w"], prm["ffn_conv_w"], prm["ffn_conv_b"], buf, l, l_in, cfg, 512)
    x = _ffn_down(x, act, prm["ffn_w_down"], l, cfg.tm, 512)
    return x, (shr_n, shl_n, h_re_n, h_im_n, ust), (a_rw, a_hg, a_gl)


def _state_outputs(new, cfg):
    shr, shl, h_re, h_im, ust = new
    B = cfg.B
    shift = jnp.concatenate([shr, shl], axis=-1).reshape(B, R_COLS)
    if cfg.inject:
        conv = ust.reshape(B, cfg.T, FF_PAD)[:, cfg.T - (CONV_W - 1):, :D_FF]
    else:
        tps = cfg.T // cfg.tm
        conv = ust[tps - 1::tps, :, :D_FF]
    return (shift, h_re.reshape(B, S5_GROUPS, S5_STATE), h_im.reshape(B, S5_GROUPS, S5_STATE), conv)


PROMPT_PAD = 0
SAMPLE_PAD = 4


def kernel(x_prompt, x_sample, state_rwkv, state_rwkv_shift, state_s5_re, state_s5_im, state_hgrn, state_gla, state_ffn_conv, meta_tokens, norm_mix, w_in, w_out, rwkv_mu, rwkv_w0, rwkv_w_up, rwkv_a0, rwkv_a_up, rwkv_g_up, rwkv_k_k, rwkv_k_a, rwkv_r_k, rwkv_ln, s5_A_re, s5_A_im, s5_log_dt, s5_B_re, s5_B_im, s5_C_re, s5_C_im, s5_D, s5_w_glu, s5_b_glu, hgrn_lower_bounds, hgrn_norm, gla_gk_up, gla_gk_b, gla_norm, norm_ffn, ffn_w_up, ffn_conv_w, ffn_conv_b, ffn_w_down, norm_final):
    L = DEPTH
    Bp, Sp, D = x_prompt.shape
    Bs, Ss, _ = x_sample.shape
    Tp = N_META + Sp
    Ts = SAMPLE_PAD + Ss
    raw = dict(norm_mix=norm_mix, w_in=w_in, w_out=w_out, rwkv_mu=rwkv_mu, rwkv_w0=rwkv_w0, rwkv_w_up=rwkv_w_up,
               rwkv_a0=rwkv_a0, rwkv_a_up=rwkv_a_up, rwkv_g_up=rwkv_g_up, rwkv_k_k=rwkv_k_k, rwkv_k_a=rwkv_k_a,
               rwkv_r_k=rwkv_r_k, rwkv_ln=rwkv_ln, s5_A_re=s5_A_re, s5_A_im=s5_A_im, s5_log_dt=s5_log_dt,
               s5_B_re=s5_B_re, s5_B_im=s5_B_im, s5_C_re=s5_C_re, s5_C_im=s5_C_im, s5_D=s5_D, s5_w_glu=s5_w_glu,
               s5_b_glu=s5_b_glu, hgrn_lower_bounds=hgrn_lower_bounds, hgrn_norm=hgrn_norm, gla_gk_up=gla_gk_up,
               gla_gk_b=gla_gk_b, gla_norm=gla_norm, norm_ffn=norm_ffn, ffn_w_up=ffn_w_up, ffn_conv_w=ffn_conv_w,
               ffn_conv_b=ffn_conv_b, ffn_w_down=ffn_w_down)
    prm = _prep_params(raw, (PROMPT_PAD, SAMPLE_PAD))

    cfg_p = _Cfg(B=Bp, T=Tp, t0=PROMPT_PAD, tm=Tp // 3, g_mix=Bp, u_rwkv=4, c_rwkv=48, cb_gla=48, c_gla=48,
                 s5_split=False, g_s5=1, c_s5=Tp // 3, inject=False)
    cfg_s = _Cfg(B=Bs, T=Ts, t0=SAMPLE_PAD, tm=Bs * Ts, g_mix=16, u_rwkv=8, c_rwkv=Ts, cb_gla=Ts, c_gla=Ts,
                 s5_split=True, g_s5=32, c_s5=Ts, inject=True)

    dt = x_prompt.dtype
    meta = jnp.broadcast_to(meta_tokens.astype(dt)[None], (Bp, N_META, D))
    xp = jnp.concatenate([meta, x_prompt], axis=1).reshape(Bp * Tp, D)
    xs = jnp.pad(x_sample, ((0, 0), (SAMPLE_PAD, 0), (0, 0))).reshape(Bs * Ts, D)

    W = R_WIDTH
    st_p = (jnp.zeros((1, Bp, 1, 3 * W), F32), jnp.zeros((1, Bp, 1, R_LOW), F32),
            jnp.zeros((1, Bp, R_HEADS, R_HEAD_DIM, R_HEAD_DIM), F32),
            jnp.zeros((1, Bp, 1, S5_FLAT), F32), jnp.zeros((1, Bp, 1, S5_FLAT), F32),
            jnp.zeros((1, Bp, H_HEADS, H_HEAD_DIM, H_HEAD_DIM), F32),
            jnp.zeros((1, Bp, G_HEADS, G_KEY_DIM, G_VAL_DIM), F32),
            jnp.zeros((1, Bp, CONV_W - 1, FF_PAD), F32))
    buf_s = jnp.pad(state_ffn_conv, ((0, 0), (0, 0), (SAMPLE_PAD - (CONV_W - 1), Ss), (0, FF_PAD - D_FF)))
    st_s = (state_rwkv_shift[:, :, None, 0:3 * W], state_rwkv_shift[:, :, None, 3 * W:], state_rwkv,
            state_s5_re.reshape(L, Bs, 1, S5_FLAT), state_s5_im.reshape(L, Bs, 1, S5_FLAT),
            state_hgrn, state_gla, buf_s.reshape(L, Bs * Ts, FF_PAD))

    def acc_init(B):
        return (jnp.zeros((L, B, R_HEADS, R_HEAD_DIM, R_HEAD_DIM), F32),
                jnp.zeros((L, B, H_HEADS, H_HEAD_DIM, H_HEAD_DIM), F32),
                jnp.zeros((L, B, G_HEADS, G_KEY_DIM, G_VAL_DIM), F32))

    acc_p, acc_s = acc_init(Bp), acc_init(Bs)
    outs_p, outs_s = [], []
    for l in range(L):
        xp, new_p, acc_p = _layer(xp, st_p, acc_p, 0, prm, l, cfg_p)
        xs, new_s, acc_s = _layer(xs, st_s, acc_s, l, prm, l, cfg_s)
        outs_p.append(_state_outputs(new_p, cfg_p))
        outs_s.append(_state_outputs(new_s, cfg_s))

    g_fin = norm_final.reshape(1, D).astype(F32)
    y_prompt = _final_norm_skip(xp.reshape(Bp, Tp, D), g_fin, N_META, 512)
    y_sample = _final_norm(xs, g_fin, cfg_s.tm).reshape(Bs, Ts, D)[:, SAMPLE_PAD:]
    sp = [jnp.stack([o[i] for o in outs_p]) for i in range(4)]
    ss = [jnp.stack([o[i] for o in outs_s]) for i in range(4)]
    return (y_prompt, y_sample, acc_p[0], acc_s[0], sp[0], ss[0], sp[1], ss[1], sp[2], ss[2],
            acc_p[1], acc_s[1], acc_p[2], acc_s[2], sp[3], ss[3])
```

```python
import functools
import math
from typing import NamedTuple

import jax
import jax.numpy as jnp
from jax import lax
from jax.experimental import pallas as pl
from jax.experimental.pallas import tpu as pltpu

F32 = jnp.float32
BF16 = jnp.bfloat16

D_MODEL = 2048
DEPTH = 4
N_META = 16
GROUP_WIDTH = D_MODEL // 4
EPS = 1e-6
R_HEAD_DIM = 64
R_HEADS = GROUP_WIDTH // R_HEAD_DIM
R_WIDTH = R_HEADS * R_HEAD_DIM
R_DECAY_RANK = 64
R_ICL_RANK = 64
R_GATE_RANK = 128
R_LOW = R_DECAY_RANK + R_ICL_RANK + R_GATE_RANK
R_COLS = 3 * R_WIDTH + R_LOW
RWKV_DECAY_SCALE = 0.606531
RWKV_GN_EPS = 64e-5
S5_GROUP_CH = 16
S5_GROUPS = GROUP_WIDTH // S5_GROUP_CH
S5_WIDTH = S5_GROUPS * S5_GROUP_CH
S5_STATE = 64
S5_FLAT = S5_GROUPS * S5_STATE
H_HEAD_DIM = 128
H_HEADS = GROUP_WIDTH // H_HEAD_DIM
H_WIDTH = H_HEADS * H_HEAD_DIM
HGRN_MAX_INPUT = 1.0 - 1e-4
G_VAL_DIM = 128
G_HEADS = GROUP_WIDTH // G_VAL_DIM
G_KEY_DIM = G_VAL_DIM // 2
G_WIDTH = G_HEADS * G_VAL_DIM
G_QK = G_HEADS * G_KEY_DIM
G_GATE_RANK = 16
GLA_GATE_NORM = 16.0
GLA_SAFE_DECAY = 80.0
D_FF = ((8 * D_MODEL // 3 + 127) // 128) * 128
CONV_W = 3

P_R = 0
P_S5 = 3 * R_WIDTH
P_H = P_S5 + S5_WIDTH
P_GV = P_H + 4 * H_WIDTH
P_GG = P_GV + G_WIDTH
G_KEY_PAD = G_VAL_DIM
G_QK_PAD = G_HEADS * G_KEY_PAD
P_GQ = P_GG + G_WIDTH
P_GK = P_GQ + G_QK_PAD
P_RL = P_GK + G_QK_PAD
P_GL = P_RL + R_LOW
GL_PAD = 256
P_COLS = P_GL + GL_PAD
FF_PAD = 5632
S5_CHUNKS = 4
S5_SCAN = 8
FFN_COL_BLOCK = 256

VMEM_LIMIT = 56 * 1024 * 1024

NT = (((1,), (1,)), ((), ()))
TN = (((0,), (0,)), ((), ()))


class _Cfg(NamedTuple):
    B: int
    T: int
    t0: int
    tm: int
    g_mix: int
    u_rwkv: int
    c_rwkv: int
    cb_gla: int
    c_gla: int
    s5_split: bool
    g_s5: int
    c_s5: int
    inject: bool


def _dot(a, b):
    return jnp.dot(a, b, preferred_element_type=F32)


def _dg(a, b, dims):
    return lax.dot_general(a, b, dims, preferred_element_type=F32)


def _gelu(x):
    c = math.sqrt(2.0 / math.pi)
    return x * (0.5 * (1.0 + jnp.tanh(c * (x + 0.044715 * (x * x * x)))))


def _tloc(G, C, W):
    t = lax.broadcasted_iota(jnp.int32, (C, W), 0)
    return t if G == 1 else jnp.concatenate([t] * G, axis=0)


def _cumsum_rows(x, tl, C):
    sh = 1
    while sh < C:
        x = x + jnp.where(tl >= sh, pltpu.roll(x, sh, 0), 0.0)
        sh *= 2
    return x


def _segsum(x, e):
    x1 = x.astype(BF16)
    x2 = (x - x1.astype(F32)).astype(BF16)
    return _dot(x1, e) + _dot(x2, e)


def _alias_stacked(kern, n_in, acc):
    if acc is None:
        return (lambda *refs: kern(*refs[:n_in], None, *refs[n_in:])), [], (), {}
    return kern, [pl.BlockSpec(memory_space=pl.ANY)], (acc,), {n_in: 1}


def _params(sem):
    return pltpu.CompilerParams(dimension_semantics=sem, vmem_limit_bytes=VMEM_LIMIT)


def _norm_matmul_kernel(x_ref, g_ref, w_ref, o_ref, h_scr):
    @pl.when(pl.program_id(1) == 0)
    def _():
        x = x_ref[...]
        r = lax.rsqrt(jnp.mean(x * x, axis=-1, keepdims=True) + EPS)
        h_scr[...] = (x * r * g_ref[...]).astype(BF16)

    o_ref[...] = _dg(h_scr[...], w_ref[...], NT)


def _norm_matmul(x, g_all, w_all, l, tm, tn):
    M, D = x.shape
    N = w_all.shape[1]
    return pl.pallas_call(
        _norm_matmul_kernel,
        grid=(M // tm, N // tn),
        in_specs=[
            pl.BlockSpec((tm, D), lambda i, j: (i, 0)),
            pl.BlockSpec((None, 1, D), lambda i, j: (l, 0, 0)),
            pl.BlockSpec((None, tn, D), lambda i, j: (l, j, 0)),
        ],
        out_specs=pl.BlockSpec((tm, tn), lambda i, j: (i, j)),
        out_shape=jax.ShapeDtypeStruct((M, N), F32),
        scratch_shapes=[pltpu.VMEM((tm, D), BF16)],
        compiler_params=_params(("arbitrary", "arbitrary")),
        name="in_proj",
    )(x, g_all, w_all)


def _out_proj_kernel(x_ref, y0, y1, y2, y3, w0, w1, w2, w3, o_ref):
    acc = _dot(y0[...].astype(BF16), w0[...])
    acc = acc + _dot(y1[...].astype(BF16), w1[...])
    acc = acc + _dot(y2[...].astype(BF16), w2[...])
    acc = acc + _dot(y3[...].astype(BF16), w3[...])
    o_ref[...] = x_ref[...] + acc


def _out_proj(x, ys, w_all, l, tm, tn):
    M, D = x.shape
    W = GROUP_WIDTH
    yspec = pl.BlockSpec((tm, W), lambda i, j: (i, 0))
    wspecs = [pl.BlockSpec((None, W, tn), functools.partial(lambda i, j, q: (l, q, j), q=q)) for q in range(4)]
    return pl.pallas_call(
        _out_proj_kernel,
        grid=(M // tm, D // tn),
        in_specs=[pl.BlockSpec((tm, tn), lambda i, j: (i, j))] + [yspec] * 4 + wspecs,
        out_specs=pl.BlockSpec((tm, tn), lambda i, j: (i, j)),
        out_shape=jax.ShapeDtypeStruct((M, D), F32),
        compiler_params=_params(("arbitrary", "arbitrary")),
        name="out_proj",
    )(x, *ys, w_all, w_all, w_all, w_all)


def _ffn_up_kernel(x_ref, g_ref, wu_ref, wg_ref, cw_ref, cb_ref, buf_ref, act_ref, ust_ref, h_scr, *car,
                   tm, tn, tiles_per_seq, t0, seq_rows, inject):
    i = pl.program_id(0)
    j = pl.program_id(1)

    @pl.when(j == 0)
    def _():
        x = x_ref[...]
        r = lax.rsqrt(jnp.mean(x * x, axis=-1, keepdims=True) + EPS)
        h_scr[...] = (x * r * g_ref[...]).astype(BF16)

    if not inject:
        (car_scr,) = car

        @pl.when(i % tiles_per_seq == 0)
        def _():
            car_scr[j, 0:2, :] = buf_ref[...]

    nb = FFN_COL_BLOCK
    blocks = [slice(nb * q, nb * (q + 1)) for q in range(tn // nb)]
    h = h_scr[...]
    us = [_dot(h, wu_ref[:, cs]) for cs in blocks]
    gts = [_dot(h, wg_ref[:, cs]) for cs in blocks]
    rows = lax.broadcasted_iota(jnp.int32, (tm, nb), 0)
    for cs, u, gt in zip(blocks, us, gts):
        if inject:
            tl = rows & (seq_rows - 1)
            u = jnp.where((tl == t0 - 2) | (tl == t0 - 1), buf_ref[:, cs], u)
            p1 = pltpu.roll(u, 1, 0)
            p2 = pltpu.roll(u, 2, 0)
            ust_ref[:, cs] = u
        else:
            c0 = car_scr[j, 0:1, cs]
            c1 = car_scr[j, 1:2, cs]
            p1 = jnp.where(rows == 0, c1, pltpu.roll(u, 1, 0))
            p2 = jnp.where(rows == 0, c0, jnp.where(rows == 1, c1, pltpu.roll(u, 2, 0)))
            last = u[tm - 2:tm, :]
            car_scr[j, 0:2, cs] = last
            ust_ref[:, cs] = last
        c = cb_ref[:, cs] + cw_ref[0:1, cs] * p2
        c = c + cw_ref[1:2, cs] * p1
        c = c + cw_ref[2:3, cs] * u
        act_ref[:, cs] = (_gelu(c) * gt).astype(BF16)


def _ffn_up(x, g_all, wu_all, wg_all, cw_all, cb_all, buf, l, l_in, cfg, tn):
    M, D = x.shape
    tm = cfg.tm
    N = wu_all.shape[-1]
    tps = cfg.T // tm if not cfg.inject else 1
    kern = functools.partial(_ffn_up_kernel, tm=tm, tn=tn, tiles_per_seq=tps, t0=cfg.t0, seq_rows=cfg.T,
                             inject=cfg.inject)
    if cfg.inject:
        buf_spec = pl.BlockSpec((None, tm, tn), lambda i, j: (l_in, i, j))
        ust_spec = pl.BlockSpec((tm, tn), lambda i, j: (i, j))
        ust_shape = jax.ShapeDtypeStruct((M, N), F32)
        scratch = [pltpu.VMEM((tm, D), BF16)]
    else:
        buf_spec = pl.BlockSpec((None, None, CONV_W - 1, tn), lambda i, j: (l_in, i // tps, 0, j))
        ust_spec = pl.BlockSpec((None, CONV_W - 1, tn), lambda i, j: (i, 0, j))
        ust_shape = jax.ShapeDtypeStruct((M // tm, CONV_W - 1, N), F32)
        scratch = [pltpu.VMEM((tm, D), BF16), pltpu.VMEM((N // tn, 8, tn), F32)]
    return pl.pallas_call(
        kern,
        grid=(M // tm, N // tn),
        in_specs=[
            pl.BlockSpec((tm, D), lambda i, j: (i, 0)),
            pl.BlockSpec((None, 1, D), lambda i, j: (l, 0, 0)),
            pl.BlockSpec((None, D, tn), lambda i, j: (l, 0, j)),
            pl.BlockSpec((None, D, tn), lambda i, j: (l, 0, j)),
            pl.BlockSpec((None, CONV_W, tn), lambda i, j: (l, 0, j)),
            pl.BlockSpec((None, 1, tn), lambda i, j: (l, 0, j)),
            buf_spec,
        ],
        out_specs=[pl.BlockSpec((tm, tn), lambda i, j: (i, j)), ust_spec],
        out_shape=[jax.ShapeDtypeStruct((M, N), BF16), ust_shape],
        scratch_shapes=scratch,
        compiler_params=_params(("arbitrary", "arbitrary")),
        name="ffn_up",
    )(x, g_all, wu_all, wg_all, cw_all, cb_all, buf)


def _ffn_down_kernel(x_ref, act_ref, w_ref, o_ref):
    o_ref[...] = x_ref[...] + _dot(act_ref[...], w_ref[...])


def _ffn_down(x, act, w_all, l, tm, tn):
    M, D = x.shape
    K = w_all.shape[1]
    return pl.pallas_call(
        _ffn_down_kernel,
        grid=(M // tm, D // tn),
        in_specs=[
            pl.BlockSpec((tm, tn), lambda i, j: (i, j)),
            pl.BlockSpec((tm, K), lambda i, j: (i, 0)),
            pl.BlockSpec((None, K, tn), lambda i, j: (l, 0, j)),
        ],
        out_specs=pl.BlockSpec((tm, tn), lambda i, j: (i, j)),
        out_shape=jax.ShapeDtypeStruct((M, D), F32),
        compiler_params=_params(("arbitrary", "arbitrary")),
        name="ffn_down",
    )(x, act, w_all)


def _final_norm_kernel(x_ref, g_ref, o_ref):
    x = x_ref[...]
    r = lax.rsqrt(jnp.mean(x * x, axis=-1, keepdims=True) + EPS)
    o_ref[...] = x * r * g_ref[...]


def _final_norm_skip_kernel(x_ref, g_ref, o_ref):
    x = x_ref[0]
    r = lax.rsqrt(jnp.mean(x * x, axis=-1, keepdims=True) + EPS)
    o_ref[0] = x * r * g_ref[...]


def _final_norm_skip(x3, g, skip, tr):
    B, T, D = x3.shape
    assert skip % 8 == 0 and tr % 8 == 0 and (T - skip) % tr == 0
    return pl.pallas_call(
        _final_norm_skip_kernel,
        grid=(B, (T - skip) // tr),
        in_specs=[pl.BlockSpec((pl.Element(1), pl.Element(tr), pl.Element(D)),
                               lambda b, i: (b, (skip // 8 + i * (tr // 8)) * 8, 0)),
                  pl.BlockSpec((1, D), lambda b, i: (0, 0))],
        out_specs=pl.BlockSpec((1, tr, D), lambda b, i: (b, i, 0)),
        out_shape=jax.ShapeDtypeStruct((B, T - skip, D), F32),
        compiler_params=_params(("arbitrary", "arbitrary")),
        name="final_norm_skip",
    )(x3, g)


def _final_norm(x, g, tm):
    M, D = x.shape
    return pl.pallas_call(
        _final_norm_kernel,
        grid=(M // tm,),
        in_specs=[pl.BlockSpec((tm, D), lambda i: (i, 0)), pl.BlockSpec((1, D), lambda i: (0, 0))],
        out_specs=pl.BlockSpec((tm, D), lambda i: (i, 0)),
        out_shape=jax.ShapeDtypeStruct((M, D), F32),
        compiler_params=_params(("arbitrary",)),
        name="final_norm",
    )(x, g)


def _rwkv_kernel(prkv_ref, plow_ref, shr_in, shl_in, s0_ref, mur_ref, mul_ref, w0_ref, wup_ref, a0_ref, aup_ref,
                 gup_ref, kk_ref, ka_ref, rk_ref, ln_ref, e_ref, acc_ref,
                 y_ref, sout_ref, shr_out, shl_out,
                 at_scr, rt_scr, kt_scr, bt_scr, v_scr, gam_scr, y_scr, *, G, C, t0, U):
    R = G * C
    W = R_WIDTH

    @pl.when(pl.program_id(1) == 0)
    def _():
        sout_ref[...] = s0_ref[...]
        shr_out[...] = shr_in[...]
        shl_out[...] = shl_in[...]

    def token_shift(p_ref, car_ref, mu_ref, w):
        p = p_ref[...].reshape(R, w)
        tl = _tloc(G, C, w)
        carry = jnp.concatenate([jnp.broadcast_to(car_ref[g], (C, w)) for g in range(G)], axis=0)
        prev = jnp.where(tl == t0, carry, pltpu.roll(p, 1, 0))
        for g in range(G):
            car_ref[g] = p[g * C + C - 1:g * C + C, :]
        return p + (prev - p) * mu_ref[...]

    ps = token_shift(prkv_ref, shr_out, mur_ref, 3 * W)
    lo = token_shift(plow_ref, shl_out, mul_ref, R_LOW)
    r = ps[:, 0:W]
    k = ps[:, W:2 * W]
    v = ps[:, 2 * W:3 * W]
    lw = -RWKV_DECAY_SCALE * jax.nn.sigmoid(w0_ref[...] + _dot(jnp.tanh(lo).astype(BF16), wup_ref[...]))
    a = jax.nn.sigmoid(a0_ref[...] + _dot(lo.astype(BF16), aup_ref[...]))
    gate = _dot(jax.nn.sigmoid(lo).astype(BF16), gup_ref[...])
    e = e_ref[...]
    kk = k * kk_ref[...]
    kk = kk / jnp.maximum(jnp.sqrt(_segsum(kk * kk, e)), 1e-12)
    k2 = k * (1.0 + (a - 1.0) * ka_ref[...])
    ka = kk * a
    tl = _tloc(G, C, W)
    if t0 > 0:
        valid = tl >= t0
        lw = jnp.where(valid, lw, 0.0)
        kk = jnp.where(valid, kk, 0.0)
        ka = jnp.where(valid, ka, 0.0)
        k2 = jnp.where(valid, k2, 0.0)
        v = jnp.where(valid, v, 0.0)
    cum = _cumsum_rows(lw, tl, C)
    gam = jnp.exp(cum)
    inv = jnp.exp(-cum)
    at_scr[...] = kk * jnp.exp(cum - lw)
    rt_scr[...] = r * gam
    kt_scr[...] = k2 * inv
    bt_scr[...] = ka * inv
    v_scr[...] = v
    gam_scr[...] = gam

    row = lax.broadcasted_iota(jnp.int32, (C, C), 0)
    col = lax.broadcasted_iota(jnp.int32, (C, C), 1)
    low_s = row > col
    low_i = row >= col

    def seq_body(it, carry):
        gs = [it * U + i for i in range(U)]
        lanes = [(i, h) for i in range(U) for h in range(R_HEADS)]
        data = []
        for g in gs:
            off = pl.multiple_of(g * C, 8)
            rows = pl.ds(off, C)
            data.append((at_scr[rows, :], rt_scr[rows, :], kt_scr[rows, :], bt_scr[rows, :], v_scr[rows, :],
                         gam_scr[pl.ds(off + C - 8, 8), :][7:8, :]))
        s_in = [sout_ref[gs[i], h] for i, h in lanes]

        def head(x, h):
            return x[:, R_HEAD_DIM * h:R_HEAD_DIM * (h + 1)]

        ar = [jnp.concatenate([head(data[i][0], h), head(data[i][1], h)], axis=0).astype(BF16) for i, h in lanes]
        ktb = [head(data[i][2], h).astype(BF16) for i, h in lanes]
        btb = [head(data[i][3], h).astype(BF16) for i, h in lanes]
        vb = [head(data[i][4], h).astype(BF16) for i, h in lanes]
        n = range(len(lanes))
        m_k = [_dg(ar[j], ktb[j], NT) for j in n]
        m_b = [_dg(ar[j], btb[j], NT) for j in n]
        n_pow = [jnp.where(low_s, m_b[j][0:C], 0.0) for j in n]
        acc = [-n_pow[j] for j in n]
        pre_rhs = [_dot(jnp.where(low_s, m_k[j][0:C], 0.0).astype(BF16), vb[j]) for j in n]
        pre_y = [_dot(jnp.where(low_i, m_k[j][C:2 * C], 0.0).astype(BF16), vb[j]) for j in n]
        pre_ds = [_dg(vb[j], ktb[j], TN) for j in n]
        trb = [jnp.where(low_i, m_b[j][C:2 * C], 0.0).astype(BF16) for j in n]
        span = 2
        nb = [n_pow[j].astype(BF16) for j in n]
        n_pow = [_dot(nb[j], nb[j]) for j in n]
        while span < C:
            nb = [n_pow[j].astype(BF16) for j in n]
            span *= 2
            prod = [_dot(acc[j].astype(BF16), nb[j]) for j in n]
            if span < C:
                nxt = [_dot(nb[j], nb[j]) for j in n]
            acc = [acc[j] + n_pow[j] + prod[j] for j in n]
            if span < C:
                n_pow = nxt
        accb = [acc[j].astype(BF16) for j in n]
        m_s = [_dg(ar[j], s_in[j].astype(BF16), NT) for j in n]
        rhs = [m_s[j][0:C] + pre_rhs[j] for j in n]
        u = [rhs[j] + _dot(accb[j], rhs[j].astype(BF16)) for j in n]
        ub = [u[j].astype(BF16) for j in n]
        yh = [m_s[j][C:2 * C] + pre_y[j] - _dot(trb[j], ub[j]) for j in n]
        ds = [pre_ds[j] - _dg(ub[j], btb[j], TN) for j in n]
        for j, (i, h) in enumerate(lanes):
            sout_ref[gs[i], h] = (s_in[j] + ds[j]) * head(data[i][5], h)
        for i, g in enumerate(gs):
            rows = pl.ds(pl.multiple_of(g * C, 8), C)
            y_scr[rows, :] = jnp.concatenate(yh[i * R_HEADS:(i + 1) * R_HEADS], axis=1)
        return carry

    lax.fori_loop(0, G // U, seq_body, 0)

    y = y_scr[...]
    inv_n = 1.0 / R_HEAD_DIM
    d = y - _segsum(y, e) * inv_n
    var = _segsum(d * d, e) * inv_n
    y = d * lax.rsqrt(var + RWKV_GN_EPS) * ln_ref[...]
    y = y + _segsum(r * k2 * rk_ref[...], e) * v
    y_ref[...] = (y * gate).reshape(G, C, W)


def _rwkv(p3, st, acc, l_in, prm, l, cfg):
    B, T, _ = p3.shape
    G, C = cfg.g_mix, cfg.c_rwkv
    W = R_WIDTH
    shr_in, shl_in, s0 = st

    def pspec(w, off):
        return pl.BlockSpec((G, C, w), lambda i, c: (i, c, off // w))

    def lay(*shape):
        return pl.BlockSpec((None,) + shape, lambda i, c: (l,) + (0,) * len(shape))

    kern, alias_specs, alias_args, aliases = _alias_stacked(
        functools.partial(_rwkv_kernel, G=G, C=C, t0=cfg.t0, U=cfg.u_rwkv), 17, acc)
    R = G * C
    return pl.pallas_call(
        kern,
        grid=(B // G, T // C),
        in_specs=[
            pspec(3 * W, P_R), pspec(R_LOW, P_RL),
            pl.BlockSpec((None, G, 1, 3 * W), lambda i, c: (l_in, i, 0, 0)),
            pl.BlockSpec((None, G, 1, R_LOW), lambda i, c: (l_in, i, 0, 0)),
            pl.BlockSpec((None, G, R_HEADS, R_HEAD_DIM, R_HEAD_DIM), lambda i, c: (l_in, i, 0, 0, 0)),
            lay(1, 3 * W), lay(1, R_LOW), lay(1, W), lay(R_LOW, W), lay(1, W), lay(R_LOW, W), lay(R_LOW, W),
            lay(1, W), lay(1, W), lay(1, W), lay(1, W),
            pl.BlockSpec((W, W), lambda i, c: (0, 0)),
        ] + alias_specs,
        input_output_aliases=aliases,
        out_specs=[
            pl.BlockSpec((G, C, W), lambda i, c: (i, c, 0)),
            pl.BlockSpec((None, G, R_HEADS, R_HEAD_DIM, R_HEAD_DIM), lambda i, c: (l, i, 0, 0, 0)),
            pl.BlockSpec((G, 1, 3 * W), lambda i, c: (i, 0, 0)),
            pl.BlockSpec((G, 1, R_LOW), lambda i, c: (i, 0, 0)),
        ],
        out_shape=[
            jax.ShapeDtypeStruct((B, T, W), F32),
            jax.ShapeDtypeStruct((DEPTH, B, R_HEADS, R_HEAD_DIM, R_HEAD_DIM), F32),
            jax.ShapeDtypeStruct((B, 1, 3 * W), F32),
            jax.ShapeDtypeStruct((B, 1, R_LOW), F32),
        ],
        scratch_shapes=[pltpu.VMEM((R, W), F32)] * 7,
        compiler_params=_params(("arbitrary", "arbitrary")),
        name="rwkv7",
    )(p3, p3, shr_in, shl_in, s0, prm["mu_rkv"], prm["mu_low"], prm["rwkv_w0"], prm["rwkv_w_up"], prm["rwkv_a0"],
      prm["rwkv_a_up"], prm["rwkv_g_up"], prm["rwkv_k_k"], prm["rwkv_k_a"], prm["rwkv_r_k"], prm["rwkv_ln"],
      prm["e_head"], *alias_args)


def _s5_kernel(u_ref, h0r_ref, h0i_ref, bbhr_ref, bblr_ref, bbhi_ref, bbli_ref, ccr_ref, cci_ref, ap_ref, pw_ref,
               d_ref, wglu_ref, bglu_ref,
               y_ref, hr_out, hi_out, hre_scr, him_scr, *, G, C, t0, split):
    R = G * C
    NB = S5_FLAT // S5_CHUNKS
    UB = S5_WIDTH // S5_CHUNKS

    @pl.when(pl.program_id(1) == 0)
    def _():
        hr_out[...] = h0r_ref[...]
        hi_out[...] = h0i_ref[...]

    u = u_ref[...].reshape(R, S5_WIDTH)
    uh = u.astype(BF16)
    ul = (u - uh.astype(F32)).astype(BF16)
    valid = _tloc(G, C, NB) >= t0
    for j in range(S5_CHUNKS):
        us = slice(UB * j, UB * (j + 1))
        br = _dot(uh[:, us], bbhr_ref[j])
        bi = _dot(uh[:, us], bbhi_ref[j])
        if split:
            br = br + _dot(ul[:, us], bbhr_ref[j]) + _dot(uh[:, us], bblr_ref[j])
            bi = bi + _dot(ul[:, us], bbhi_ref[j]) + _dot(uh[:, us], bbli_ref[j])
        if t0 > 0:
            br = jnp.where(valid, br, 0.0)
            bi = jnp.where(valid, bi, 0.0)
        hre_scr[:, NB * j:NB * (j + 1)] = br
        him_scr[:, NB * j:NB * (j + 1)] = bi

    steps = [(1 << k, ap_ref[S5_SCAN * 2 * k:S5_SCAN * (2 * k + 1), :],
              ap_ref[S5_SCAN * (2 * k + 1):S5_SCAN * (2 * k + 2), :]) for k in range(3)]
    pwr = pw_ref[0:S5_SCAN, :]
    pwi = pw_ref[S5_SCAN:2 * S5_SCAN, :]

    def seq_body(g, carry):
        def tile_body(i, hc):
            hcr, hci = hc
            rows = pl.ds(pl.multiple_of(g * C + i * S5_SCAN, S5_SCAN), S5_SCAN)
            xr = hre_scr[rows, :]
            xi = him_scr[rows, :]
            for sh, ar, ai in steps:
                sr = pltpu.roll(xr, sh, 0)
                si = pltpu.roll(xi, sh, 0)
                xr, xi = xr + ar * sr - ai * si, xi + ar * si + ai * sr
            xr, xi = xr + pwr * hcr - pwi * hci, xi + pwr * hci + pwi * hcr
            hre_scr[rows, :] = xr
            him_scr[rows, :] = xi
            return xr[S5_SCAN - 1:S5_SCAN, :], xi[S5_SCAN - 1:S5_SCAN, :]

        hcr, hci = lax.fori_loop(0, C // S5_SCAN, tile_body, (hr_out[g], hi_out[g]))
        hr_out[g] = hcr
        hi_out[g] = hci
        return carry

    lax.fori_loop(0, G, seq_body, 0)

    ys = []
    for j in range(S5_CHUNKS):
        hs = slice(NB * j, NB * (j + 1))
        ys.append(_dot(hre_scr[:, hs].astype(BF16), ccr_ref[j]) - _dot(him_scr[:, hs].astype(BF16), cci_ref[j]))
    y = jnp.concatenate(ys, axis=1) + d_ref[...] * u
    y = _gelu(y)
    y = y * jax.nn.sigmoid(_dot(y.astype(BF16), wglu_ref[...]) + bglu_ref[...])
    y_ref[...] = y.reshape(G, C, S5_WIDTH)


def _s5(p3, st, l_in, prm, l, cfg):
    B, T, _ = p3.shape
    G, C = cfg.g_s5, cfg.c_s5
    h0r, h0i = st
    NB = S5_FLAT // S5_CHUNKS
    UB = S5_WIDTH // S5_CHUNKS

    def lay(*shape):
        return pl.BlockSpec((None,) + shape, lambda i, c: (l,) + (0,) * len(shape))

    st_spec = pl.BlockSpec((None, G, 1, S5_FLAT), lambda i, c: (l_in, i, 0, 0))
    out_st = pl.BlockSpec((G, 1, S5_FLAT), lambda i, c: (i, 0, 0))
    kern = functools.partial(_s5_kernel, G=G, C=C, t0=cfg.t0, split=cfg.s5_split)
    pw = prm["s5_pw"][cfg.t0]
    return pl.pallas_call(
        kern,
        grid=(B // G, T // C),
        in_specs=[
            pl.BlockSpec((G, C, S5_WIDTH), lambda i, c: (i, c, P_S5 // S5_WIDTH)),
            st_spec, st_spec,
            lay(S5_CHUNKS, UB, NB), lay(S5_CHUNKS, UB, NB), lay(S5_CHUNKS, UB, NB), lay(S5_CHUNKS, UB, NB),
            lay(S5_CHUNKS, NB, UB), lay(S5_CHUNKS, NB, UB),
            lay(6 * S5_SCAN, S5_FLAT), lay(2 * S5_SCAN, S5_FLAT),
            lay(1, S5_WIDTH), lay(S5_WIDTH, S5_WIDTH), lay(1, S5_WIDTH),
        ],
        out_specs=[pl.BlockSpec((G, C, S5_WIDTH), lambda i, c: (i, c, 0)), out_st, out_st],
        out_shape=[
            jax.ShapeDtypeStruct((B, T, S5_WIDTH), F32),
            jax.ShapeDtypeStruct((B, 1, S5_FLAT), F32),
            jax.ShapeDtypeStruct((B, 1, S5_FLAT), F32),
        ],
        scratch_shapes=[pltpu.VMEM((G * C, S5_FLAT), F32)] * 2,
        compiler_params=_params(("arbitrary", "arbitrary")),
        name="s5",
    )(p3, h0r, h0i, prm["s5_bbh_re"], prm["s5_bbl_re"], prm["s5_bbh_im"], prm["s5_bbl_im"], prm["s5_cc_re"],
      prm["s5_cc_im"], prm["s5_apow"], pw, prm["s5_D"], prm["s5_w_glu"], prm["s5_b_glu"])


def _gla_core(q, kx, gdec, v, s0_ref, sout_ref, st_scr, q_scr, k_scr, b_scr, v_scr, o_scr, *, G, CB, C, t0, K, KS):
    assert CB % C == 0 and (CB == C or C & (C - 1) == 0) and (t0 == 0 or CB == C)
    H = G_HEADS
    V = G_VAL_DIM
    HK = H * K
    c = pl.program_id(1)

    @pl.when(c == 0)
    def _():
        for g in range(G):
            for h in range(H):
                s = s0_ref[g, h].T
                if KS < K:
                    s = jnp.concatenate([s, jnp.zeros((V, K - KS), F32)], axis=1)
                st_scr[g, h] = s

    if CB == C:
        tl = _tloc(G, C, HK)
    else:
        tl = lax.broadcasted_iota(jnp.int32, (G * CB, HK), 0) & (C - 1)
    if t0 > 0:
        valid = tl >= t0
        gdec = jnp.where(valid, gdec, 0.0)
        kx = jnp.where(valid, kx, 0.0)
    q_scr[...] = q
    k_scr[...] = kx
    b_scr[...] = _cumsum_rows(gdec, tl, C)
    v_scr[...] = v
    ti = lax.broadcasted_iota(jnp.int32, (C, HK), 0)

    row = lax.broadcasted_iota(jnp.int32, (C, C), 0)
    col = lax.broadcasted_iota(jnp.int32, (C, C), 1)
    lanes = [(g, h) for g in range(G) for h in range(H)]

    def ksl(h):
        return slice(K * h, K * (h + 1))

    def vsl(h):
        return slice(V * h, V * (h + 1))

    def chunk_body(j, carry):
        rows = [pl.ds(pl.multiple_of(g * CB + j * C, 8), C) for g in range(G)]
        qg = [q_scr[r, :] for r in rows]
        kg = [k_scr[r, :] for r in rows]
        bg = [b_scr[r, :] for r in rows]
        vg = [v_scr[r, :] for r in rows]
        vb = [x.astype(BF16) for x in vg]
        b_last = [b[C - 1:C, :] for b in bg]
        qe = [(qg[g] * jnp.exp(bg[g])).astype(BF16) for g in range(G)]
        kd = [(kg[g] * jnp.exp(b_last[g] - bg[g])).astype(BF16) for g in range(G)]
        dec_last = [jnp.exp(b) for b in b_last]
        s_in = [st_scr[g, h] for g, h in lanes]
        inter = [_dg(qe[g][:, ksl(h)], s_in[n].astype(BF16), NT) for n, (g, h) in enumerate(lanes)]
        upd = [_dg(vb[g][:, vsl(h)], kd[g][:, ksl(h)], TN) for g, h in lanes]
        b_min = b_last[0]
        for b in b_last[1:]:
            b_min = jnp.minimum(b_min, b)

        def factored():
            kn = [(kg[g] * jnp.exp(-bg[g])).astype(BF16) for g in range(G)]
            att = [_dg(qe[g][:, ksl(h)], kn[g][:, ksl(h)], NT) for g, h in lanes]
            att = [jnp.where(row >= col, a, 0.0).astype(BF16) for a in att]
            outs = [_dot(att[n], vb[g][:, vsl(h)]) for n, (g, h) in enumerate(lanes)]
            return [jnp.concatenate(outs[g * H:(g + 1) * H], axis=1) for g in range(G)]

        def pairwise():
            res = []
            for g in range(G):
                base = g * CB + j * C

                def key_row(jj, intra, g=g, base=base):
                    kj = k_scr[pl.ds(base + jj, 1), :]
                    bj = b_scr[pl.ds(base + jj, 1), :]
                    vj = v_scr[pl.ds(base + jj, 1), :]
                    causal = ti >= jj
                    e = jnp.where(causal, qg[g] * kj * jnp.exp(jnp.where(causal, bg[g] - bj, 0.0)), 0.0)
                    return tuple(intra[h] + jnp.sum(e[:, ksl(h)], axis=-1, keepdims=True) * vj[:, vsl(h)]
                                 for h in range(H))

                intra = lax.fori_loop(0, C, key_row, tuple(jnp.zeros((C, V), F32) for _ in range(H)))
                res.append(jnp.concatenate(intra, axis=1))
            return res

        intra = lax.cond(jnp.min(b_min) >= -GLA_SAFE_DECAY, factored, pairwise)
        for n, (g, h) in enumerate(lanes):
            st_scr[g, h] = s_in[n] * dec_last[g][:, ksl(h)] + upd[n]
        for g in range(G):
            o_scr[rows[g], :] = jnp.concatenate(inter[g * H:(g + 1) * H], axis=1) + intra[g]
        return carry

    lax.fori_loop(0, CB // C, chunk_body, 0)

    @pl.when(c == pl.num_programs(1) - 1)
    def _():
        for g in range(G):
            for h in range(H):
                sout_ref[g, h] = st_scr[g, h][:, 0:KS].T


def _head_norm_gate(o, norm, gate_raw):
    V = G_VAL_DIM
    ys = []
    for h in range(G_HEADS):
        oh = o[:, V * h:V * (h + 1)]
        ys.append(oh * lax.rsqrt(jnp.mean(oh * oh, axis=-1, keepdims=True) + EPS))
    y = jnp.concatenate(ys, axis=1) * norm
    return y * (gate_raw * jax.nn.sigmoid(gate_raw))


def _hgrn_kernel(q_ref, f_ref, i_ref, g_ref, lb_ref, norm_ref, s0_ref, acc_ref, y_ref, sout_ref,
                 st_scr, q_scr, k_scr, b_scr, v_scr, o_scr, *, G, C, chunk, t0, layer):
    R = G * C
    W = H_WIDTH
    lbr = lb_ref[...]
    ex = jnp.exp(lbr - jnp.max(lbr, axis=0, keepdims=True))
    sm = ex / jnp.sum(ex, axis=0, keepdims=True)
    lb = jnp.zeros((1, W), F32)
    for i in range(1, layer + 1):
        lb = lb + sm[i:i + 1, :]
    qr = q_ref[...].reshape(R, W)
    kx = jnp.minimum((1.0 - lb) * jax.nn.sigmoid(-f_ref[...].reshape(R, W)), HGRN_MAX_INPUT)
    _gla_core(qr * jax.nn.sigmoid(qr), kx, jnp.log1p(-kx), i_ref[...].reshape(R, W), s0_ref, sout_ref,
              st_scr, q_scr, k_scr, b_scr, v_scr, o_scr, G=G, CB=C, C=chunk, t0=t0, K=H_HEAD_DIM, KS=H_HEAD_DIM)
    y_ref[...] = _head_norm_gate(o_scr[...], norm_ref[...], g_ref[...].reshape(R, W)).reshape(G, C, W)


def _gla_kernel(v_ref, gate_ref, q_ref, k_ref, gl_ref, gkup_ref, gkb_ref, norm_ref, s0_ref, acc_ref, y_ref,
                sout_ref,
                st_scr, q_scr, k_scr, b_scr, v_scr, o_scr, *, G, C, chunk, t0):
    R = G * C
    z = _dot(gl_ref[...].reshape(R, GL_PAD).astype(BF16), gkup_ref[...]) + gkb_ref[...]
    gdec = -(jnp.maximum(-z, 0.0) + jnp.log1p(jnp.exp(-jnp.abs(z)))) / GLA_GATE_NORM
    _gla_core(q_ref[...].reshape(R, G_QK_PAD) * (G_KEY_DIM ** -0.5), k_ref[...].reshape(R, G_QK_PAD), gdec,
              v_ref[...].reshape(R, G_WIDTH), s0_ref, sout_ref, st_scr, q_scr, k_scr, b_scr, v_scr, o_scr,
              G=G, CB=C, C=chunk, t0=t0, K=G_KEY_PAD, KS=G_KEY_DIM)
    y_ref[...] = _head_norm_gate(o_scr[...], norm_ref[...], gate_ref[...].reshape(R, G_WIDTH)).reshape(G, C, G_WIDTH)


def _gla_like(p3, s0, acc, l_in, prm, l, cfg, hgrn):
    B, T, _ = p3.shape
    G, C = cfg.g_mix, cfg.cb_gla
    H, V = G_HEADS, G_VAL_DIM
    K = H_HEAD_DIM if hgrn else G_KEY_PAD
    KS = H_HEAD_DIM if hgrn else G_KEY_DIM
    HK = H * K
    R = G * C

    def pspec(w, off):
        return pl.BlockSpec((G, C, w), lambda i, c: (i, c, off // w))

    def lay(*shape):
        return pl.BlockSpec((None,) + shape, lambda i, c: (l,) + (0,) * len(shape))

    st_in = pl.BlockSpec((None, G, H, KS, V), lambda i, c: (l_in, i, 0, 0, 0))
    if hgrn:
        kern = functools.partial(_hgrn_kernel, G=G, C=C, chunk=cfg.c_gla, t0=cfg.t0, layer=l)
        in_specs = [pspec(HK, P_H), pspec(HK, P_H + HK), pspec(HK, P_H + 2 * HK), pspec(HK, P_H + 3 * HK),
                    pl.BlockSpec((DEPTH, HK), lambda i, c: (0, 0)), lay(1, HK), st_in]
        args = (p3, p3, p3, p3, prm["hgrn_lb_raw"], prm["hgrn_norm"], s0)
        name = "hgrn2"
    else:
        kern = functools.partial(_gla_kernel, G=G, C=C, chunk=cfg.c_gla, t0=cfg.t0)
        in_specs = [pspec(G_WIDTH, P_GV), pspec(G_WIDTH, P_GG), pspec(HK, P_GQ), pspec(HK, P_GK),
                    pspec(GL_PAD, P_GL), lay(GL_PAD, HK), lay(1, HK), lay(1, G_WIDTH), st_in]
        args = (p3, p3, p3, p3, p3, prm["gla_gk_up"], prm["gla_gk_b"], prm["gla_norm"], s0)
        name = "gla"
    kern, alias_specs, alias_args, aliases = _alias_stacked(kern, len(args), acc)
    return pl.pallas_call(
        kern,
        grid=(B // G, T // C),
        in_specs=in_specs + alias_specs,
        out_specs=[pl.BlockSpec((G, C, H * V), lambda i, c: (i, c, 0)),
                   pl.BlockSpec((None, G, H, KS, V), lambda i, c: (l, i, 0, 0, 0))],
        out_shape=[jax.ShapeDtypeStruct((B, T, H * V), F32), jax.ShapeDtypeStruct((DEPTH, B, H, KS, V), F32)],
        input_output_aliases=aliases,
        scratch_shapes=[pltpu.VMEM((G, H, V, K), F32), pltpu.VMEM((R, HK), F32), pltpu.VMEM((R, HK), F32),
                        pltpu.VMEM((R, HK), F32), pltpu.VMEM((R, H * V), F32), pltpu.VMEM((R, H * V), F32)],
        compiler_params=_params(("arbitrary", "arbitrary")),
        name=name,
    )(*args, *alias_args)


def _s5_tables(A_re, A_im, log_dt, B_re, B_im, C_re, C_im, t0s):
    L = A_re.shape[0]
    A_re = A_re.astype(F32)
    A_im = A_im.astype(F32)
    dt = jnp.exp(log_dt.astype(F32))[..., None]
    mag = jnp.exp(A_re * dt)
    ab_re = mag * jnp.cos(A_im * dt)
    ab_im = mag * jnp.sin(A_im * dt)
    den = A_re * A_re + A_im * A_im
    n_re = ab_re - 1.0
    co_re = (n_re * A_re + ab_im * A_im) / den
    co_im = (ab_im * A_re - n_re * A_im) / den
    B_re = B_re.astype(F32)
    B_im = B_im.astype(F32)
    bb_re = co_re[..., None] * B_re - co_im[..., None] * B_im
    bb_im = co_re[..., None] * B_im + co_im[..., None] * B_re
    gpc = S5_GROUPS // S5_CHUNKS
    eye = jnp.eye(gpc, dtype=F32)

    def block_in(bb):
        t = bb.reshape(L, S5_CHUNKS, gpc, S5_STATE, S5_GROUP_CH).transpose(0, 1, 2, 4, 3)
        t = t[:, :, :, :, None, :] * eye[None, None, :, None, :, None]
        return t.reshape(L, S5_CHUNKS, gpc * S5_GROUP_CH, gpc * S5_STATE)

    def block_out(cc):
        t = cc.astype(F32).reshape(L, S5_CHUNKS, gpc, S5_GROUP_CH, S5_STATE).transpose(0, 1, 2, 4, 3)
        t = t[:, :, :, :, None, :] * eye[None, None, :, None, :, None]
        return t.reshape(L, S5_CHUNKS, gpc * S5_STATE, gpc * S5_GROUP_CH)

    def split(x):
        hi = x.astype(BF16)
        return hi, (x - hi.astype(F32)).astype(BF16)

    bbh_re, bbl_re = split(block_in(bb_re))
    bbh_im, bbl_im = split(block_in(bb_im))

    def power(m):
        mg = jnp.exp(m * dt * A_re)
        return (mg * jnp.cos(m * dt * A_im)).reshape(L, S5_FLAT), (mg * jnp.sin(m * dt * A_im)).reshape(L, S5_FLAT)

    zero = jnp.zeros((L, S5_FLAT), F32)
    levels = []
    for sh in (1, 2, 4):
        pr, pi = power(float(sh))
        levels += [zero if r < sh else pr for r in range(S5_SCAN)]
        levels += [zero if r < sh else pi for r in range(S5_SCAN)]
    apow = jnp.stack(levels, axis=1)
    pws = {}
    for t0 in t0s:
        first = t0 % S5_SCAN
        res, ims = [], []
        for r in range(S5_SCAN):
            if r < first:
                res.append(zero)
                ims.append(zero)
            else:
                pr, pi = power(float(r - first + 1))
                res.append(pr)
                ims.append(pi)
        pws[t0] = jnp.stack(res + ims, axis=1)
    return dict(s5_bbh_re=bbh_re, s5_bbl_re=bbl_re, s5_bbh_im=bbh_im, s5_bbl_im=bbl_im,
                s5_cc_re=block_out(C_re).astype(BF16), s5_cc_im=block_out(C_im).astype(BF16),
                s5_apow=apow, s5_pw=pws)


def _prep_params(raw, t0s):
    L = raw["w_in"].shape[0]
    W = R_WIDTH
    w_in = raw["w_in"]
    o_s5 = R_COLS
    o_h = o_s5 + S5_WIDTH
    o_g = o_h + 4 * H_WIDTH
    o_gv = o_g + 2 * G_QK
    o_gl = o_gv + G_WIDTH
    o_gg = o_gl + G_GATE_RANK
    def key_pad(w):
        w = w.reshape(w.shape[:-1] + (G_HEADS, G_KEY_DIM))
        w = jnp.pad(w, ((0, 0),) * (w.ndim - 1) + ((0, G_KEY_PAD - G_KEY_DIM),))
        return w.reshape(w.shape[:-2] + (G_QK_PAD,))

    def key_pad_rows(w):
        w = w.reshape(L, G_HEADS, G_KEY_DIM, w.shape[-1])
        w = jnp.pad(w, ((0, 0), (0, 0), (0, G_KEY_PAD - G_KEY_DIM), (0, 0)))
        return w.reshape(L, G_QK_PAD, w.shape[-1])

    w_t = jnp.swapaxes(w_in, 1, 2)
    pad = jnp.zeros((L, GL_PAD - G_GATE_RANK, w_t.shape[-1]), w_in.dtype)
    w_in_p = jnp.concatenate([
        w_t[:, 0:3 * W], w_t[:, o_s5:o_h], w_t[:, o_h:o_g], w_t[:, o_gv:o_gl],
        w_t[:, o_gg:o_gg + G_WIDTH], key_pad_rows(w_t[:, o_g:o_g + G_QK]), key_pad_rows(w_t[:, o_g + G_QK:o_gv]),
        w_t[:, 3 * W:R_COLS], w_t[:, o_gl:o_gg], pad], axis=1).astype(BF16)
    assert w_in_p.shape[1] == P_COLS

    def row3(x):
        return x.reshape(L, 1, -1).astype(F32)

    def low_pad(w, off):
        return jnp.pad(w, ((0, 0), (off, R_LOW - off - w.shape[1]), (0, 0))).astype(BF16)

    ff = FF_PAD - D_FF
    w_up = raw["ffn_w_up"]
    head = jnp.arange(W) // R_HEAD_DIM
    prm = dict(
        norm_mix=row3(raw["norm_mix"]), norm_ffn=row3(raw["norm_ffn"]),
        w_in=w_in_p, w_out=raw["w_out"].astype(BF16),
        mu_rkv=row3(raw["rwkv_mu"][:, 0:3 * W]), mu_low=row3(raw["rwkv_mu"][:, 3 * W:]),
        rwkv_w0=row3(raw["rwkv_w0"]), rwkv_a0=row3(raw["rwkv_a0"]),
        rwkv_w_up=low_pad(raw["rwkv_w_up"], 0), rwkv_a_up=low_pad(raw["rwkv_a_up"], R_DECAY_RANK),
        rwkv_g_up=low_pad(raw["rwkv_g_up"], R_DECAY_RANK + R_ICL_RANK),
        rwkv_k_k=row3(raw["rwkv_k_k"]), rwkv_k_a=row3(raw["rwkv_k_a"]), rwkv_r_k=row3(raw["rwkv_r_k"]),
        rwkv_ln=row3(raw["rwkv_ln"]),
        e_head=(head[:, None] == head[None, :]).astype(BF16),
        s5_D=row3(raw["s5_D"]), s5_w_glu=raw["s5_w_glu"].astype(BF16), s5_b_glu=row3(raw["s5_b_glu"]),
        hgrn_lb_raw=raw["hgrn_lower_bounds"].astype(F32), hgrn_norm=row3(raw["hgrn_norm"]),
        gla_gk_up=jnp.pad(key_pad(raw["gla_gk_up"]), ((0, 0), (0, GL_PAD - G_GATE_RANK), (0, 0))).astype(BF16),
        gla_gk_b=row3(key_pad(raw["gla_gk_b"])), gla_norm=row3(raw["gla_norm"]),
        ffn_wu=jnp.pad(w_up[..., :D_FF], ((0, 0), (0, 0), (0, ff))).astype(BF16),
        ffn_wg=jnp.pad(w_up[..., D_FF:], ((0, 0), (0, 0), (0, ff))).astype(BF16),
        ffn_conv_w=jnp.pad(raw["ffn_conv_w"], ((0, 0), (0, 0), (0, ff))).astype(F32),
        ffn_conv_b=jnp.pad(raw["ffn_conv_b"], ((0, 0), (0, ff))).reshape(L, 1, FF_PAD).astype(F32),
        ffn_w_down=raw["ffn_w_down"].astype(BF16),
    )
    prm.update(_s5_tables(raw["s5_A_re"], raw["s5_A_im"], raw["s5_log_dt"], raw["s5_B_re"], raw["s5_B_im"],
                          raw["s5_C_re"], raw["s5_C_im"], t0s))
    return prm


def _layer(x, st, acc, l_in, prm, l, cfg):
    shr, shl, s_rw, h_re, h_im, s_hg, s_gl, buf = st
    a_rw, a_hg, a_gl = acc
    p = _norm_matmul(x, prm["norm_mix"], prm["w_in"], l, cfg.tm, P_COLS // 4)
    p3 = p.reshape(cfg.B, cfg.T, P_COLS)
    y_r, a_rw, shr_n, shl_n = _rwkv(p3, (shr, shl, s_rw), a_rw, l_in, prm, l, cfg)
    y_s, h_re_n, h_im_n = _s5(p3, (h_re, h_im), l_in, prm, l, cfg)
    y_h, a_hg = _gla_like(p3, s_hg, a_hg, l_in, prm, l, cfg, True)
    y_g, a_gl = _gla_like(p3, s_gl, a_gl, l_in, prm, l, cfg, False)
    M = cfg.B * cfg.T
    ys = [y.reshape(M, GROUP_WIDTH) for y in (y_r, y_s, y_h, y_g)]
    x = _out_proj(x, ys, prm["w_out"], l, cfg.tm, 1024)
    act, ust = _ffn_up(x, prm["norm_ffn"], prm["ffn_wu"], prm["ffn_wg"], prm["ffn_conv_w"], prm["ffn_conv_b"], buf,
                       l, l_in, cfg, 512)
    x = _ffn_down(x, act, prm["ffn_w_down"], l, cfg.tm, 512)
    return x, (shr_n, shl_n, h_re_n, h_im_n, ust), (a_rw, a_hg, a_gl)


def _state_outputs(new, cfg):
    shr, shl, h_re, h_im, ust = new
    B = cfg.B
    shift = jnp.concatenate([shr, shl], axis=-1).reshape(B, R_COLS)
    if cfg.inject:
        conv = ust.reshape(B, cfg.T, FF_PAD)[:, cfg.T - (CONV_W - 1):, :D_FF]
    else:
        tps = cfg.T // cfg.tm
        conv = ust[tps - 1::tps, :, :D_FF]
    return (shift, h_re.reshape(B, S5_GROUPS, S5_STATE), h_im.reshape(B, S5_GROUPS, S5_STATE), conv)


PROMPT_PAD = 0
SAMPLE_PAD = 4


def kernel(x_prompt, x_sample, state_rwkv, state_rwkv_shift, state_s5_re, state_s5_im, state_hgrn, state_gla, state_ffn_conv, meta_tokens, norm_mix, w_in, w_out, rwkv_mu, rwkv_w0, rwkv_w_up, rwkv_a0, rwkv_a_up, rwkv_g_up, rwkv_k_k, rwkv_k_a, rwkv_r_k, rwkv_ln, s5_A_re, s5_A_im, s5_log_dt, s5_B_re, s5_B_im, s5_C_re, s5_C_im, s5_D, s5_w_glu, s5_b_glu, hgrn_lower_bounds, hgrn_norm, gla_gk_up, gla_gk_b, gla_norm, norm_ffn, ffn_w_up, ffn_conv_w, ffn_conv_b, ffn_w_down, norm_final):
    L = DEPTH
    Bp, Sp, D = x_prompt.shape
    Bs, Ss, _ = x_sample.shape
    Tp = N_META + Sp
    Ts = SAMPLE_PAD + Ss
    raw = dict(norm_mix=norm_mix, w_in=w_in, w_out=w_out, rwkv_mu=rwkv_mu, rwkv_w0=rwkv_w0, rwkv_w_up=rwkv_w_up,
               rwkv_a0=rwkv_a0, rwkv_a_up=rwkv_a_up, rwkv_g_up=rwkv_g_up, rwkv_k_k=rwkv_k_k, rwkv_k_a=rwkv_k_a,
               rwkv_r_k=rwkv_r_k, rwkv_ln=rwkv_ln, s5_A_re=s5_A_re, s5_A_im=s5_A_im, s5_log_dt=s5_log_dt,
               s5_B_re=s5_B_re, s5_B_im=s5_B_im, s5_C_re=s5_C_re, s5_C_im=s5_C_im, s5_D=s5_D, s5_w_glu=s5_w_glu,
               s5_b_glu=s5_b_glu, hgrn_lower_bounds=hgrn_lower_bounds, hgrn_norm=hgrn_norm, gla_gk_up=gla_gk_up,
               gla_gk_b=gla_gk_b, gla_norm=gla_norm, norm_ffn=norm_ffn, ffn_w_up=ffn_w_up, ffn_conv_w=ffn_conv_w,
               ffn_conv_b=ffn_conv_b, ffn_w_down=ffn_w_down)
    prm = _prep_params(raw, (PROMPT_PAD, SAMPLE_PAD))

    cfg_p = _Cfg(B=Bp, T=Tp, t0=PROMPT_PAD, tm=Tp // 3, g_mix=Bp, u_rwkv=4, c_rwkv=48, cb_gla=48, c_gla=48,
                 s5_split=False, g_s5=1, c_s5=Tp // 3, inject=False)
    cfg_s = _Cfg(B=Bs, T=Ts, t0=SAMPLE_PAD, tm=Bs * Ts, g_mix=16, u_rwkv=8, c_rwkv=Ts, cb_gla=Ts, c_gla=Ts,
                 s5_split=True, g_s5=32, c_s5=Ts, inject=True)

    dt = x_prompt.dtype
    meta = jnp.broadcast_to(meta_tokens.astype(dt)[None], (Bp, N_META, D))
    xp = jnp.concatenate([meta, x_prompt], axis=1).reshape(Bp * Tp, D)
    xs = jnp.pad(x_sample, ((0, 0), (SAMPLE_PAD, 0), (0, 0))).reshape(Bs * Ts, D)

    W = R_WIDTH
    st_p = (jnp.zeros((1, Bp, 1, 3 * W), F32), jnp.zeros((1, Bp, 1, R_LOW), F32),
            jnp.zeros((1, Bp, R_HEADS, R_HEAD_DIM, R_HEAD_DIM), F32),
            jnp.zeros((1, Bp, 1, S5_FLAT), F32), jnp.zeros((1, Bp, 1, S5_FLAT), F32),
            jnp.zeros((1, Bp, H_HEADS, H_HEAD_DIM, H_HEAD_DIM), F32),
            jnp.zeros((1, Bp, G_HEADS, G_KEY_DIM, G_VAL_DIM), F32),
            jnp.zeros((1, Bp, CONV_W - 1, FF_PAD), F32))
    buf_s = jnp.pad(state_ffn_conv, ((0, 0), (0, 0), (SAMPLE_PAD - (CONV_W - 1), Ss), (0, FF_PAD - D_FF)))
    st_s = (state_rwkv_shift[:, :, None, 0:3 * W], state_rwkv_shift[:, :, None, 3 * W:], state_rwkv,
            state_s5_re.reshape(L, Bs, 1, S5_FLAT), state_s5_im.reshape(L, Bs, 1, S5_FLAT),
            state_hgrn, state_gla, buf_s.reshape(L, Bs * Ts, FF_PAD))

    acc_p = acc_s = (None, None, None)
    outs_p, outs_s = [], []
    for l in range(L):
        xp, new_p, acc_p = _layer(xp, st_p, acc_p, 0, prm, l, cfg_p)
        xs, new_s, acc_s = _layer(xs, st_s, acc_s, l, prm, l, cfg_s)
        outs_p.append(_state_outputs(new_p, cfg_p))
        outs_s.append(_state_outputs(new_s, cfg_s))

    g_fin = norm_final.reshape(1, D).astype(F32)
    y_prompt = _final_norm_skip(xp.reshape(Bp, Tp, D), g_fin, N_META, 512)
    y_sample = _final_norm(xs, g_fin, cfg_s.tm).reshape(Bs, Ts, D)[:, SAMPLE_PAD:]
    sp = [jnp.stack([o[i] for o in outs_p]) for i in range(4)]
    ss = [jnp.stack([o[i] for o in outs_s]) for i in range(4)]
    return (y_prompt, y_sample, acc_p[0], acc_s[0], sp[0], ss[0], sp[1], ss[1], sp[2], ss[2],
            acc_p[1], acc_s[1], acc_p[2], acc_s[2], sp[3], ss[3])
```
